```python
import jax, jax.numpy as jnp
from jax import lax
import numpy as np

D_MODEL = 2048
BATCH = 8
SEQ = 4096
DEPTH = 4
DEC_BATCH = 32
DEC_SEQ = 32
PAST_LEN = 2048

CHUNK = 64
N_MIXERS = 4
N_MLSTM = (DEPTH + 3) // 4
N_RET = (DEPTH + 2) // 4
N_GLA = (DEPTH + 1) // 4
N_GDN = DEPTH // 4
EPS = 1e-6
F32 = jnp.float32

A_HEADS = 8
A_DQK = D_MODEL // A_HEADS // 2
A_DV = D_MODEL // A_HEADS
A_IN = 2 * A_HEADS * A_DQK + 2 * A_HEADS * A_DV + 2 * A_HEADS

B_HEADS = D_MODEL // 256
B_DK = 256
B_DV = 2 * D_MODEL // B_HEADS
B_IN = 2 * B_HEADS * B_DK + 2 * B_HEADS * B_DV
ROPE_BASE = 10000.0

C_HEADS = 4
C_DK = D_MODEL // 2 // C_HEADS
C_DV = D_MODEL // C_HEADS
C_RANK = 16
C_TAU = 16.0
C_IN = 2 * C_HEADS * C_DK + 2 * C_HEADS * C_DV + C_RANK

D_QK_HEADS = 16
D_V_HEADS = 32
D_HEAD = 128
D_CONV = 4
D_CONV_CH = 2 * D_QK_HEADS * D_HEAD + D_V_HEADS * D_HEAD
D_IN = D_CONV_CH + D_V_HEADS * D_HEAD + 2 * D_V_HEADS

D_FF = 11 * D_MODEL // 4
FFN_CONV = 3

kernel_name = "hybrid_stream_mlstm_ret_gla_gdn"


def rmsnorm(x, g):
    xf = x.astype(F32)
    y = xf * lax.rsqrt(jnp.mean(xf * xf, axis=-1, keepdims=True) + EPS)
    return (y * g.astype(F32)).astype(x.dtype)


def l2norm(x):
    return x * lax.rsqrt(jnp.sum(x * x, axis=-1, keepdims=True) + EPS)


def causal_dwconv(x, buf, w):
    width, T = w.shape[0], x.shape[1]
    xp = jnp.concatenate([buf.astype(x.dtype), x], axis=1)
    y = xp[:, 0:T] * w[0]
    for j in range(1, width):
        y = y + xp[:, j:j + T] * w[j]
    return y, xp[:, -(width - 1):]


def chunk_scan(step, state, xs):
    bsz, T = xs[0].shape[:2]
    L = min(CHUNK, T)
    n = T // L
    xs_c = tuple(jnp.swapaxes(a.reshape(bsz, n, L, *a.shape[2:]), 0, 1) for a in xs)
    state, ys = lax.scan(step, state, xs_c)
    return state, jnp.swapaxes(ys, 0, 1).reshape(bsz, T, *ys.shape[3:])


def rope(x, pos):
    half = x.shape[-1] // 2
    inv = ROPE_BASE ** (-jnp.arange(half, dtype=F32) / half)
    ang = pos.astype(F32)[:, None] * inv[None]
    cos, sin = jnp.cos(ang)[None, :, None], jnp.sin(ang)[None, :, None]
    x1, x2 = x[..., :half], x[..., half:]
    return jnp.concatenate([x1 * cos - x2 * sin, x1 * sin + x2 * cos], axis=-1)


def mlstm_step(carry, xs):
    C, n, m = carry
    q, k, v, ig, lf = xs
    L = q.shape[1]
    tri = jnp.tril(jnp.ones((L, L), bool))[None, :, :, None]
    b = jnp.cumsum(lf, axis=1)
    log_intra = jnp.where(tri, b[:, :, None] - b[:, None, :] + ig[:, None, :], -jnp.inf)
    log_inter = b + m[:, None]
    m_t = jnp.maximum(log_inter, jnp.max(log_intra, axis=2))
    w_intra = jnp.exp(log_intra - m_t[:, :, None])
    w_inter = jnp.exp(log_inter - m_t)
    a = jnp.einsum('bthd,bshd->btsh', q, k) * w_intra
    num = jnp.einsum('btsh,bshe->bthe', a, v) + w_inter[..., None] * jnp.einsum('bthd,bhde->bthe', q, C)
    den = jnp.sum(a, axis=2) + w_inter * jnp.einsum('bthd,bhd->bth', q, n)
    h = num / jnp.maximum(jnp.abs(den), jnp.exp(-m_t))[..., None]
    m_new = m_t[:, -1]
    w_k = jnp.exp(b[:, -1:] - b + ig - m_new[:, None])
    dec = jnp.exp(b[:, -1] + m - m_new)
    C_new = dec[..., None, None] * C + jnp.einsum('bsh,bshd,bshe->bhde', w_k, k, v)
    n_new = dec[..., None] * n + jnp.einsum('bsh,bshd->bhd', w_k, k)
    return (C_new, n_new, m_new), h


def mlstm_mixer(h, C0, n0, m0, p, j):
    bsz, T, _ = h.shape
    H, dk, dv = A_HEADS, A_DQK, A_DV
    z = h @ p['mlstm_w_in'][j]
    q, k, v, o, gates = jnp.split(z, [H * dk, 2 * H * dk, 2 * H * dk + H * dv, 2 * H * dk + 2 * H * dv], axis=-1)
    gates = gates.astype(F32) + p['mlstm_b_gates'][j].astype(F32)
    ig, lf = gates[..., :H], jax.nn.log_sigmoid(gates[..., H:])
    q = q.reshape(bsz, T, H, dk).astype(F32) * dk ** -0.5
    k = k.reshape(bsz, T, H, dk).astype(F32)
    v = v.reshape(bsz, T, H, dv).astype(F32)
    (C, n, m), hh = chunk_scan(mlstm_step, (C0.astype(F32), n0.astype(F32), m0.astype(F32)), (q, k, v, ig, lf))
    hh = rmsnorm(hh, p['mlstm_norm'][j]).reshape(bsz, T, H * dv).astype(h.dtype)
    y = (jax.nn.sigmoid(o) * hh) @ p['mlstm_w_out'][j]
    return y, (C.astype(C0.dtype), n.astype(n0.dtype), m.astype(m0.dtype))


def retention_log_decay():
    return jnp.log1p(-jnp.exp2(-5.0 - jnp.arange(B_HEADS, dtype=F32)))


def retention_step(S, xs):
    q, k, v = xs
    L = q.shape[1]
    lg = retention_log_decay()
    t = jnp.arange(L, dtype=F32)
    rel = t[:, None] - t[None, :]
    dec = jnp.where((rel >= 0)[..., None], jnp.exp(jnp.maximum(rel, 0.0)[..., None] * lg), 0.0)
    a = jnp.einsum('bthd,bshd->btsh', q, k) * dec
    o = jnp.einsum('btsh,bshe->bthe', a, v) + jnp.exp((t[:, None] + 1.0) * lg)[None, :, :, None] * jnp.einsum('bthd,bhde->bthe', q, S)
    S_new = jnp.exp(L * lg)[None, :, None, None] * S + jnp.einsum('sh,bshd,bshe->bhde', jnp.exp((L - 1.0 - t)[:, None] * lg), k, v)
    return S_new, o


def retention_mixer(h, S0, pos0, p, j):
    bsz, T, _ = h.shape
    H, dk, dv = B_HEADS, B_DK, B_DV
    z = h @ p['ret_w_in'][j]
    q, k, v, g = jnp.split(z, [H * dk, 2 * H * dk, 2 * H * dk + H * dv], axis=-1)
    pos = pos0 + jnp.arange(T)
    q = rope(q.reshape(bsz, T, H, dk).astype(F32), pos) * dk ** -0.5
    k = rope(k.reshape(bsz, T, H, dk).astype(F32), pos)
    v = v.reshape(bsz, T, H, dv).astype(F32)
    S, o = chunk_scan(retention_step, S0.astype(F32), (q, k, v))
    o = rmsnorm(o, p['ret_norm'][j]).reshape(bsz, T, H * dv).astype(h.dtype)
    y = (jax.nn.silu(g) * o) @ p['ret_w_out'][j]
    return y, S.astype(S0.dtype)


def gla_step(S, xs):
    q, k, v, la = xs
    L = q.shape[1]
    tri = jnp.tril(jnp.ones((L, L), bool))[None, :, :, None, None]
    b = jnp.cumsum(la, axis=1)
    dec = jnp.exp(jnp.where(tri, b[:, :, None] - b[:, None, :], -jnp.inf))
    a = jnp.einsum('bthd,bshd,btshd->btsh', q, k, dec)
    o = jnp.einsum('btsh,bshe->bthe', a, v) + jnp.einsum('bthd,bhde->bthe', q * jnp.exp(b), S)
    bl = b[:, -1]
    S_new = jnp.exp(bl)[..., None] * S + jnp.einsum('bshd,bshe->bhde', k * jnp.exp(bl[:, None] - b), v)
    return S_new, o


def gla_mixer(h, S0, p, j):
    bsz, T, _ = h.shape
    H, dk, dv = C_HEADS, C_DK, C_DV
    z = h @ p['gla_w_in'][j]
    q, k, v, r, lr = jnp.split(z, [H * dk, 2 * H * dk, 2 * H * dk + H * dv, 2 * H * dk + 2 * H * dv], axis=-1)
    la = jax.nn.log_sigmoid((lr @ p['gla_w_gate2'][j] + p['gla_b_gate'][j]).astype(F32)) / C_TAU
    la = la.reshape(bsz, T, H, dk)
    q = q.reshape(bsz, T, H, dk).astype(F32) * dk ** -0.5
    k = k.reshape(bsz, T, H, dk).astype(F32)
    v = v.reshape(bsz, T, H, dv).astype(F32)
    S, o = chunk_scan(gla_step, S0.astype(F32), (q, k, v, la))
    o = rmsnorm(o, p['gla_norm'][j]).reshape(bsz, T, H * dv).astype(h.dtype)
    y = (jax.nn.silu(r) * o) @ p['gla_w_out'][j]
    return y, S.astype(S0.dtype)


def gdn_step(S, xs):
    q, k, v, beta, la = xs
    L, dv = q.shape[1], v.shape[-1]
    idx = jnp.arange(L)
    causal = (idx[:, None] >= idx[None, :])[None, :, :, None]
    strict = (idx[:, None] > idx[None, :])[None, :, :, None]
    g = jnp.cumsum(la, axis=1)
    dec = jnp.exp(jnp.where(causal, g[:, :, None] - g[:, None, :], -jnp.inf))
    kk = jnp.einsum('bthd,bshd->btsh', k, k)
    mlow = jnp.where(strict, beta[:, :, None] * dec * kk, 0.0)
    lhs = jnp.moveaxis(mlow, 3, 1) + jnp.eye(L, dtype=mlow.dtype)
    rhs = jnp.concatenate([beta[..., None] * v, (beta * jnp.exp(g))[..., None] * k], axis=-1)
    sol = lax.linalg.triangular_solve(lhs, jnp.moveaxis(rhs, 2, 1), left_side=True, lower=True, unit_diagonal=True)
    u = sol[..., :dv] - jnp.einsum('bhtd,bhde->bhte', sol[..., dv:], S)
    qk = jnp.einsum('bthd,bshd->bhts', q, k) * jnp.moveaxis(dec, 3, 1)
    o = jnp.exp(g)[..., None] * jnp.einsum('bthd,bhde->bthe', q, S) + jnp.einsum('bhts,bhse->bthe', qk, u)
    gl = g[:, -1]
    S_new = jnp.exp(gl)[..., None, None] * S + jnp.einsum('bsh,bshd,bhse->bhde', jnp.exp(gl[:, None] - g), k, u)
    return S_new, o


def gdn_mixer(h, S0, conv_buf, p, j):
    bsz, T, _ = h.shape
    Hk, Hv, d = D_QK_HEADS, D_V_HEADS, D_HEAD
    z = h @ p['gdn_w_in'][j]
    qkv, zg, bb, aa = jnp.split(z, [D_CONV_CH, D_CONV_CH + Hv * d, D_CONV_CH + Hv * d + Hv], axis=-1)
    qkv_c, conv_new = causal_dwconv(qkv, conv_buf, p['gdn_conv_w'][j])
    qkv_c = jax.nn.silu(qkv_c).astype(F32)
    q, k, v = jnp.split(qkv_c, [Hk * d, 2 * Hk * d], axis=-1)
    q = jnp.repeat(l2norm(q.reshape(bsz, T, Hk, d)), Hv // Hk, axis=2) * d ** -0.5
    k = jnp.repeat(l2norm(k.reshape(bsz, T, Hk, d)), Hv // Hk, axis=2)
    v = v.reshape(bsz, T, Hv, d)
    beta = jax.nn.sigmoid(bb.astype(F32))
    la = -jnp.exp(p['gdn_A_log'][j].astype(F32)) * jax.nn.softplus(aa.astype(F32) + p['gdn_dt_bias'][j].astype(F32))
    S, o = chunk_scan(gdn_step, S0.astype(F32), (q, k, v, beta, la))
    o = rmsnorm(o, p['gdn_norm'][j]) * jax.nn.silu(zg.reshape(bsz, T, Hv, d).astype(F32))
    y = o.reshape(bsz, T, Hv * d).astype(h.dtype) @ p['gdn_w_out'][j]
    return y, S.astype(S0.dtype), conv_new.astype(conv_buf.dtype)


def conv_ffn(h, buf, p, i):
    gate = h @ p['ffn_w_gate'][i]
    up = h @ p['ffn_w_up'][i]
    gc, buf_new = causal_dwconv(gate, buf, p['ffn_conv_w'][i])
    y = (jax.nn.silu(gc + p['ffn_conv_b'][i]) * up) @ p['ffn_w_down'][i]
    return y, buf_new.astype(buf.dtype)


def trunk(x, states, pos0, p):
    mC, mn, mm, rS, gS, dS, dconv, fconv = states
    out = [[] for _ in range(8)]
    for i in range(DEPTH):
        kind, j = i % N_MIXERS, i // N_MIXERS
        h = rmsnorm(x, p['norm_mix'][i])
        if kind == 0:
            y, (c_, n_, m_) = mlstm_mixer(h, mC[j], mn[j], mm[j], p, j)
            out[0].append(c_); out[1].append(n_); out[2].append(m_)
        elif kind == 1:
            y, s_ = retention_mixer(h, rS[j], pos0, p, j)
            out[3].append(s_)
        elif kind == 2:
            y, s_ = gla_mixer(h, gS[j], p, j)
            out[4].append(s_)
        else:
            y, s_, cb = gdn_mixer(h, dS[j], dconv[j], p, j)
            out[5].append(s_); out[6].append(cb)
        x = x + y
        y, fb = conv_ffn(rmsnorm(x, p['norm_ffn'][i]), fconv[i], p, i)
        out[7].append(fb)
        x = x + y
    return rmsnorm(x, p['norm_final']), tuple(jnp.stack(o) for o in out)


def setup_inputs(seed: int = 0) -> dict:
    key = jax.random.key(seed)
    ks = iter(jax.random.split(key, 64))

    def nrm(shape, scale):
        return jax.random.normal(next(ks), shape, F32) * scale

    D = D_MODEL
    dt = jnp.exp(jax.random.uniform(next(ks), (N_GDN, D_V_HEADS), F32) * (np.log(0.1) - np.log(0.001)) + np.log(0.001))
    return {
        'x_prompt': nrm((BATCH, SEQ, D), 1.0),
        'x_sample': nrm((DEC_BATCH, DEC_SEQ, D), 1.0),
        'state_mlstm_C': nrm((N_MLSTM, DEC_BATCH, A_HEADS, A_DQK, A_DV), 0.5),
        'state_mlstm_n': nrm((N_MLSTM, DEC_BATCH, A_HEADS, A_DQK), 0.5),
        'state_mlstm_m': nrm((N_MLSTM, DEC_BATCH, A_HEADS), 1.0),
        'state_ret_S': nrm((N_RET, DEC_BATCH, B_HEADS, B_DK, B_DV), 0.5),
        'state_gla_S': nrm((N_GLA, DEC_BATCH, C_HEADS, C_DK, C_DV), 0.5),
        'state_gdn_S': nrm((N_GDN, DEC_BATCH, D_V_HEADS, D_HEAD, D_HEAD), 0.1),
        'state_gdn_conv': nrm((N_GDN, DEC_BATCH, D_CONV - 1, D_CONV_CH), 1.0),
        'state_ffn_conv': nrm((DEPTH, DEC_BATCH, FFN_CONV - 1, D_FF), 1.0),
        'norm_mix': 1.0 + nrm((DEPTH, D), 0.02),
        'norm_ffn': 1.0 + nrm((DEPTH, D), 0.02),
        'norm_final': 1.0 + nrm((D,), 0.02),
        'mlstm_w_in': nrm((N_MLSTM, D, A_IN), D ** -0.5),
        'mlstm_b_gates': jnp.concatenate([nrm((N_MLSTM, A_HEADS), 0.1), jnp.linspace(3.0, 6.0, A_HEADS, dtype=F32)[None] + nrm((N_MLSTM, A_HEADS), 0.1)], axis=-1),
        'mlstm_norm': 1.0 + nrm((N_MLSTM, A_DV), 0.02),
        'mlstm_w_out': nrm((N_MLSTM, A_HEADS * A_DV, D), (A_HEADS * A_DV) ** -0.5),
        'ret_w_in': nrm((N_RET, D, B_IN), D ** -0.5),
        'ret_norm': 1.0 + nrm((N_RET, B_DV), 0.02),
        'ret_w_out': nrm((N_RET, B_HEADS * B_DV, D), (B_HEADS * B_DV) ** -0.5),
        'gla_w_in': nrm((N_GLA, D, C_IN), D ** -0.5),
        'gla_w_gate2': nrm((N_GLA, C_RANK, C_HEADS * C_DK), C_RANK ** -0.5),
        'gla_b_gate': nrm((N_GLA, C_HEADS * C_DK), 0.1),
        'gla_norm': 1.0 + nrm((N_GLA, C_DV), 0.02),
        'gla_w_out': nrm((N_GLA, C_HEADS * C_DV, D), (C_HEADS * C_DV) ** -0.5),
        'gdn_w_in': nrm((N_GDN, D, D_IN), D ** -0.5),
        'gdn_conv_w': nrm((N_GDN, D_CONV, D_CONV_CH), D_CONV ** -0.5),
        'gdn_A_log': jnp.log(jax.random.uniform(next(ks), (N_GDN, D_V_HEADS), F32, 1.0, 16.0)),
        'gdn_dt_bias': dt + jnp.log(-jnp.expm1(-dt)),
        'gdn_norm': 1.0 + nrm((N_GDN, D_HEAD), 0.02),
        'gdn_w_out': nrm((N_GDN, D_V_HEADS * D_HEAD, D), (D_V_HEADS * D_HEAD) ** -0.5),
        'ffn_w_gate': nrm((DEPTH, D, D_FF), D ** -0.5),
        'ffn_w_up': nrm((DEPTH, D, D_FF), D ** -0.5),
        'ffn_conv_w': nrm((DEPTH, FFN_CONV, D_FF), FFN_CONV ** -0.5),
        'ffn_conv_b': nrm((DEPTH, D_FF), 0.02),
        'ffn_w_down': nrm((DEPTH, D_FF, D), D_FF ** -0.5),
    }


def reference(x_prompt, x_sample, state_mlstm_C, state_mlstm_n, state_mlstm_m, state_ret_S, state_gla_S,
              state_gdn_S, state_gdn_conv, state_ffn_conv, norm_mix, norm_ffn, norm_final,
              mlstm_w_in, mlstm_b_gates, mlstm_norm, mlstm_w_out, ret_w_in, ret_norm, ret_w_out,
              gla_w_in, gla_w_gate2, gla_b_gate, gla_norm, gla_w_out,
              gdn_w_in, gdn_conv_w, gdn_A_log, gdn_dt_bias, gdn_norm, gdn_w_out,
              ffn_w_gate, ffn_w_up, ffn_conv_w, ffn_conv_b, ffn_w_down):
    p = dict(norm_mix=norm_mix, norm_ffn=norm_ffn, norm_final=norm_final,
             mlstm_w_in=mlstm_w_in, mlstm_b_gates=mlstm_b_gates, mlstm_norm=mlstm_norm, mlstm_w_out=mlstm_w_out,
             ret_w_in=ret_w_in, ret_norm=ret_norm, ret_w_out=ret_w_out,
             gla_w_in=gla_w_in, gla_w_gate2=gla_w_gate2, gla_b_gate=gla_b_gate, gla_norm=gla_norm, gla_w_out=gla_w_out,
             gdn_w_in=gdn_w_in, gdn_conv_w=gdn_conv_w, gdn_A_log=gdn_A_log, gdn_dt_bias=gdn_dt_bias,
             gdn_norm=gdn_norm, gdn_w_out=gdn_w_out,
             ffn_w_gate=ffn_w_gate, ffn_w_up=ffn_w_up, ffn_conv_w=ffn_conv_w, ffn_conv_b=ffn_conv_b,
             ffn_w_down=ffn_w_down)
    bp, dt = x_prompt.shape[0], x_prompt.dtype
    init = (jnp.zeros((N_MLSTM, bp, A_HEADS, A_DQK, A_DV), dt),
            jnp.zeros((N_MLSTM, bp, A_HEADS, A_DQK), dt),
            jnp.zeros((N_MLSTM, bp, A_HEADS), dt),
            jnp.zeros((N_RET, bp, B_HEADS, B_DK, B_DV), dt),
            jnp.zeros((N_GLA, bp, C_HEADS, C_DK, C_DV), dt),
            jnp.zeros((N_GDN, bp, D_V_HEADS, D_HEAD, D_HEAD), dt),
            jnp.zeros((N_GDN, bp, D_CONV - 1, D_CONV_CH), dt),
            jnp.zeros((DEPTH, bp, FFN_CONV - 1, D_FF), dt))
    y_prompt, (p_mlstm_C, p_mlstm_n, p_mlstm_m, p_ret_S, p_gla_S, p_gdn_S, p_gdn_conv, p_ffn_conv) = trunk(x_prompt, init, 0, p)
    cache = (state_mlstm_C, state_mlstm_n, state_mlstm_m, state_ret_S, state_gla_S, state_gdn_S, state_gdn_conv, state_ffn_conv)
    y_sample, (s_mlstm_C, s_mlstm_n, s_mlstm_m, s_ret_S, s_gla_S, s_gdn_S, s_gdn_conv, s_ffn_conv) = trunk(x_sample, cache, PAST_LEN, p)
    return (y_prompt, y_sample,
            p_mlstm_C, p_mlstm_n, p_mlstm_m, p_ret_S, p_gla_S, p_gdn_S, p_gdn_conv, p_ffn_conv,
            s_mlstm_C, s_mlstm_n, s_mlstm_m, s_ret_S, s_gla_S, s_gdn_S, s_gdn_conv, s_ffn_conv)
```

```python
import functools
import math

import jax
import jax.numpy as jnp
from jax import lax
from jax.experimental import pallas as pl
from jax.experimental.pallas import tpu as pltpu

F32 = jnp.float32
BF16 = jnp.bfloat16
EPS = 1e-6
PAST_LEN = 2048
ROPE_BASE = 10000.0
GLA_TAU = 16.0
GLA_RANK = 16
CHUNK = 64
LANES = 128
SUBLANES = 8
VMEM_LIMIT_BYTES = 56 * 1024 * 1024


def _sigmoid(x):
    return 1.0 / (1.0 + jnp.exp(-x))


def _silu(x):
    return x * _sigmoid(x)


def _log1p_exp_neg_abs(x):
    return jnp.log(1.0 + jnp.exp(-jnp.abs(x)))


def _log_sigmoid(x):
    return jnp.minimum(x, 0.0) - _log1p_exp_neg_abs(x)


def _softplus(x):
    return jnp.maximum(x, 0.0) + _log1p_exp_neg_abs(x)


def _rms(x, g):
    return x * lax.rsqrt(jnp.mean(x * x, axis=-1, keepdims=True) + EPS) * g


def _dot(a, b):
    return jnp.dot(a, b, preferred_element_type=F32)


def _dot_nt(a, b):
    return lax.dot_general(a, b, (((1,), (1,)), ((), ())), preferred_element_type=F32)


def _dot_tn(a, b):
    return lax.dot_general(a, b, (((0,), (0,)), ((), ())), preferred_element_type=F32)


def _split3(x):
    hi = x.astype(BF16)
    r = x - hi.astype(F32)
    mid = r.astype(BF16)
    lo = (r - mid.astype(F32)).astype(BF16)
    return hi, mid, lo


def _cumsum_rows(x):
    n = x.shape[0]
    r = lax.broadcasted_iota(jnp.int32, (n, n), 0)
    c = lax.broadcasted_iota(jnp.int32, (n, n), 1)
    tril = jnp.where(r >= c, 1.0, 0.0).astype(BF16)
    hi, mid, lo = _split3(x)
    return _dot(tril, hi) + _dot(tril, mid) + _dot(tril, lo)


def _transpose_rows(x):
    n = x.shape[0]
    if n < LANES:
        x = jnp.concatenate([x, jnp.zeros((LANES - n, x.shape[1]), x.dtype)], axis=0)
    return x.T[:, :n]


def _tri_masks(n):
    r = lax.broadcasted_iota(jnp.int32, (n, n), 0)
    c = lax.broadcasted_iota(jnp.int32, (n, n), 1)
    return r >= c, r > c


def _mm_kernel(*refs, has_norm, has_res):
    it = iter(refs)
    x_ref = next(it)
    g_ref = next(it) if has_norm else None
    w_ref = next(it)
    r_ref = next(it) if has_res else None
    o_ref = next(it)
    h_scr = next(it) if has_norm else None
    if has_norm:
        @pl.when(pl.program_id(1) == 0)
        def _():
            h_scr[...] = _rms(x_ref[...].astype(F32), g_ref[...]).astype(BF16)
        h = h_scr[...]
    else:
        h = x_ref[...]
    acc = _dot(h, w_ref[...])
    if has_res:
        acc = acc + r_ref[...]
    o_ref[...] = acc.astype(o_ref.dtype)


def _pick(n, cands):
    for c in cands:
        if n % c == 0:
            return c
    raise ValueError(f"no tile for {n}")


def _matmul(x, w, *, norm_g=None, residual=None, out_dtype=BF16, bm=1024, bn=None, name="mm"):
    m, k = x.shape
    n = w.shape[1]
    bm = min(bm, m)
    if bn is None:
        bn = _pick(n, (1024, 512, 256, 128)) if k <= 2048 else _pick(n, (512, 256, 128))
    assert m % bm == 0 and n % bn == 0
    has_norm, has_res = norm_g is not None, residual is not None
    in_specs = [pl.BlockSpec((bm, k), lambda i, j: (i, 0))]
    args = [x]
    if has_norm:
        in_specs.append(pl.BlockSpec((1, k), lambda i, j: (0, 0)))
        args.append(norm_g.reshape(1, k).astype(F32))
    in_specs.append(pl.BlockSpec((k, bn), lambda i, j: (0, j)))
    args.append(w)
    if has_res:
        in_specs.append(pl.BlockSpec((bm, bn), lambda i, j: (i, j)))
        args.append(residual)
    return pl.pallas_call(
        functools.partial(_mm_kernel, has_norm=has_norm, has_res=has_res),
        grid=(m // bm, n // bn),
        in_specs=in_specs,
        out_specs=pl.BlockSpec((bm, bn), lambda i, j: (i, j)),
        out_shape=jax.ShapeDtypeStruct((m, n), out_dtype),
        scratch_shapes=[pltpu.VMEM((bm, k), BF16)] if has_norm else [],
        compiler_params=pltpu.CompilerParams(
            dimension_semantics=("arbitrary", "arbitrary"), vmem_limit_bytes=VMEM_LIMIT_BYTES),
        name=name,
    )(*args)


def _gate_mm_kernel(x_ref, g_ref, w_ref, o_ref):
    h = _rms(x_ref[...].astype(F32), g_ref[...])
    hh, hm, _ = _split3(h)
    w = w_ref[...]
    wh = w.astype(BF16)
    wl = (w - wh.astype(F32)).astype(BF16)
    o_ref[...] = _dot(hh, wh) + _dot(hm, wh) + _dot(hh, wl)


def _gate_matmul(x, norm_g, w, *, bm=512, name="gate_mm"):
    m, k = x.shape
    bm = min(bm, m)
    return pl.pallas_call(
        _gate_mm_kernel,
        grid=(m // bm,),
        in_specs=[pl.BlockSpec((bm, k), lambda i: (i, 0)),
                  pl.BlockSpec((1, k), lambda i: (0, 0)),
                  pl.BlockSpec((k, LANES), lambda i: (0, 0))],
        out_specs=pl.BlockSpec((bm, LANES), lambda i: (i, 0)),
        out_shape=jax.ShapeDtypeStruct((m, LANES), F32),
        compiler_params=pltpu.CompilerParams(
            dimension_semantics=("arbitrary",), vmem_limit_bytes=VMEM_LIMIT_BYTES),
        name=name,
    )(x, norm_g.reshape(1, k).astype(F32), w)


def _final_norm_kernel(x_ref, g_ref, o_ref):
    o_ref[...] = _rms(x_ref[...], g_ref[...])


def _final_norm(x, g, *, bm=512):
    m, k = x.shape
    bm = min(bm, m)
    return pl.pallas_call(
        _final_norm_kernel,
        grid=(m // bm,),
        in_specs=[pl.BlockSpec((bm, k), lambda i: (i, 0)), pl.BlockSpec((1, k), lambda i: (0, 0))],
        out_specs=pl.BlockSpec((bm, k), lambda i: (i, 0)),
        out_shape=jax.ShapeDtypeStruct((m, k), F32),
        compiler_params=pltpu.CompilerParams(
            dimension_semantics=("arbitrary",), vmem_limit_bytes=VMEM_LIMIT_BYTES),
        name="final_norm",
    )(x, g.reshape(1, k).astype(F32))


def _ffn_act_kernel(gate_ref, up_ref, prev_ref, buf_ref, cw_ref, cb_ref, o_ref, *, tiles_per_seq):
    i = pl.program_id(0)
    g = gate_ref[...].astype(F32)
    first = (i % tiles_per_seq) == 0
    halo = jnp.where(first, buf_ref[0], prev_ref[...].astype(F32))
    row = lax.broadcasted_iota(jnp.int32, g.shape, 0)
    g1 = pltpu.roll(g, 1, 0)
    g1 = jnp.where(row == 0, halo[7:8], g1)
    g2 = pltpu.roll(g, 2, 0)
    g2 = jnp.where(row == 0, halo[6:7], jnp.where(row == 1, halo[7:8], g2))
    cw = cw_ref[...]
    gc = g2 * cw[0:1] + g1 * cw[1:2] + g * cw[2:3] + cb_ref[...]
    o_ref[...] = (_silu(gc) * up_ref[...].astype(F32)).astype(o_ref.dtype)


def _ffn_act(gu, buf8, conv_w, conv_b, *, seq_len, bf=512):
    m, f2 = gu.shape
    f = f2 // 2
    bm = min(seq_len, 1024)
    assert f % bf == 0 and seq_len % bm == 0 and bm % SUBLANES == 0
    nf = f // bf
    tiles_per_seq = seq_len // bm
    rows8 = bm // SUBLANES
    return pl.pallas_call(
        functools.partial(_ffn_act_kernel, tiles_per_seq=tiles_per_seq),
        grid=(m // bm, nf),
        in_specs=[
            pl.BlockSpec((bm, bf), lambda i, j: (i, j)),
            pl.BlockSpec((bm, bf), lambda i, j: (i, nf + j)),
            pl.BlockSpec((SUBLANES, bf), lambda i, j: (jnp.maximum(i * rows8 - 1, 0), j)),
            pl.BlockSpec((1, SUBLANES, bf), lambda i, j: (i // tiles_per_seq, 0, j)),
            pl.BlockSpec((3, bf), lambda i, j: (0, j)),
            pl.BlockSpec((1, bf), lambda i, j: (0, j)),
        ],
        out_specs=pl.BlockSpec((bm, bf), lambda i, j: (i, j)),
        out_shape=jax.ShapeDtypeStruct((m, f), BF16),
        compiler_params=pltpu.CompilerParams(
            dimension_semantics=("arbitrary", "arbitrary"), vmem_limit_bytes=VMEM_LIMIT_BYTES),
        name="ffn_act",
    )(gu, gu, gu, buf8, conv_w.astype(F32), conv_b.reshape(1, f).astype(F32))


def _mlstm_kernel(*refs, heads, dk, dv, has_init):
    it = iter(refs)
    z_ref, zg_ref, bias_ref, norm_ref = next(it), next(it), next(it), next(it)
    if has_init:
        c0_ref, n0_ref, m0_ref = next(it), next(it), next(it)
    o_ref, c_ref, n_ref, m_ref = next(it), next(it), next(it), next(it)
    ch = pl.program_id(1)
    L = z_ref.shape[0]

    @pl.when(ch == 0)
    def _():
        if has_init:
            c_ref[...] = c0_ref[...]
            n_ref[...] = n0_ref[...]
            m_ref[...] = m0_ref[...]
        else:
            c_ref[...] = jnp.zeros_like(c_ref)
            n_ref[...] = jnp.zeros_like(n_ref)
            m_ref[...] = jnp.zeros_like(m_ref)

    g = zg_ref[...] + bias_ref[...]
    lane = lax.broadcasted_iota(jnp.int32, g.shape, 1)
    u = jnp.where(lane < heads, g, _cumsum_rows(_log_sigmoid(g)))
    ut = _transpose_rows(u)
    tri, _ = _tri_masks(L)
    m_all = m_ref[0]
    m_lane = lax.broadcasted_iota(jnp.int32, m_all.shape, 1)
    qoff, koff, voff, ooff = 0, heads * dk, 2 * heads * dk, 2 * heads * dk + heads * dv
    for h in range(heads):
        ig_col, ig_row = u[:, h:h + 1], ut[h:h + 1, :]
        b_col, b_row = u[:, heads + h:heads + h + 1], ut[heads + h:heads + h + 1, :]
        m_prev = m_all[:, h:h + 1]
        q = (z_ref[:, qoff + h * dk:qoff + (h + 1) * dk].astype(F32) * dk ** -0.5).astype(BF16)
        k = z_ref[:, koff + h * dk:koff + (h + 1) * dk]
        v = z_ref[:, voff + h * dv:voff + (h + 1) * dv]
        og = z_ref[:, ooff + h * dv:ooff + (h + 1) * dv].astype(F32)
        d = jnp.where(tri, b_col - b_row + ig_row, -jnp.inf)
        li = b_col + m_prev
        m_t = jnp.maximum(li, jnp.max(d, axis=1, keepdims=True))
        w_intra = jnp.exp(d - m_t)
        w_inter = jnp.exp(li - m_t)
        a = _dot_nt(q, k) * w_intra
        c_old = c_ref[0, h]
        n_old = n_ref[0, h:h + 1, :]
        num = _dot(a.astype(BF16), v) + w_inter * _dot(q, c_old.astype(BF16))
        den = jnp.sum(a, axis=1, keepdims=True) + w_inter * jnp.sum(q.astype(F32) * n_old, axis=1, keepdims=True)
        hh = num / jnp.maximum(jnp.abs(den), jnp.exp(-m_t))
        m_new = m_t[L - 1:L, :]
        b_last = b_col[L - 1:L, :]
        w_k = jnp.exp(b_last - b_col + ig_col - m_new)
        dec = jnp.exp(b_last + m_prev - m_new)
        kw = k.astype(F32) * w_k
        c_ref[0, h] = dec * c_old + _dot_tn(kw.astype(BF16), v)
        n_ref[0, h:h + 1, :] = dec * n_old + jnp.sum(kw, axis=0, keepdims=True)
        m_all = jnp.where(m_lane == h, m_new, m_all)
        y = _sigmoid(og) * _rms(hh, norm_ref[...])
        o_ref[:, h * dv:(h + 1) * dv] = y.astype(o_ref.dtype)
    m_ref[0] = m_all


def _mlstm(z, zg, bias, norm, init, *, batch, seq_len, heads, dk, dv):
    L = min(CHUNK, seq_len)
    nc = seq_len // L
    has_init = init is not None
    row = lambda b, c: (b * nc + c, 0)
    in_specs = [pl.BlockSpec((L, z.shape[1]), row),
                pl.BlockSpec((L, LANES), row),
                pl.BlockSpec((1, LANES), lambda b, c: (0, 0)),
                pl.BlockSpec((1, dv), lambda b, c: (0, 0))]
    args = [z, zg, bias, norm.reshape(1, dv).astype(F32)]
    state_specs = [pl.BlockSpec((1, heads, dk, dv), lambda b, c: (b, 0, 0, 0)),
                   pl.BlockSpec((1, heads, dk), lambda b, c: (b, 0, 0)),
                   pl.BlockSpec((1, 1, heads), lambda b, c: (b, 0, 0))]
    if has_init:
        in_specs += state_specs
        args += [init[0], init[1], init[2].reshape(batch, 1, heads)]
    o, c_new, n_new, m_new = pl.pallas_call(
        functools.partial(_mlstm_kernel, heads=heads, dk=dk, dv=dv, has_init=has_init),
        grid=(batch, nc),
        in_specs=in_specs,
        out_specs=[pl.BlockSpec((L, heads * dv), row)] + state_specs,
        out_shape=[jax.ShapeDtypeStruct((batch * seq_len, heads * dv), BF16),
                   jax.ShapeDtypeStruct((batch, heads, dk, dv), F32),
                   jax.ShapeDtypeStruct((batch, heads, dk), F32),
                   jax.ShapeDtypeStruct((batch, 1, heads), F32)],
        compiler_params=pltpu.CompilerParams(
            dimension_semantics=("arbitrary", "arbitrary"), vmem_limit_bytes=VMEM_LIMIT_BYTES),
        name="mlstm",
    )(*args)
    return o, (c_new, n_new, m_new.reshape(batch, heads))


def _ret_kernel(*refs, heads, dk, dv, has_init):
    it = iter(refs)
    cos_ref, sin_ref, z_ref, norm_ref = next(it), next(it), next(it), next(it)
    s0_ref = next(it) if has_init else None
    o_ref, s_ref = next(it), next(it)
    ch = pl.program_id(1)
    L = z_ref.shape[0]
    half = dk // 2

    @pl.when(ch == 0)
    def _():
        s_ref[...] = s0_ref[...] if has_init else jnp.zeros_like(s_ref)

    cos, sin = cos_ref[...], sin_ref[...]
    t_col = lax.broadcasted_iota(jnp.int32, (L, 1), 0).astype(F32)
    rel = (lax.broadcasted_iota(jnp.int32, (L, L), 0) - lax.broadcasted_iota(jnp.int32, (L, L), 1)).astype(F32)
    koff, voff, goff = heads * dk, 2 * heads * dk, 2 * heads * dk + heads * dv

    def rope(x):
        x1, x2 = x[:, :half], x[:, half:]
        return jnp.concatenate([x1 * cos - x2 * sin, x1 * sin + x2 * cos], axis=-1)

    for h in range(heads):
        lg = math.log1p(-2.0 ** (-5.0 - h))
        q = rope(z_ref[:, h * dk:(h + 1) * dk].astype(F32)) * dk ** -0.5
        k = rope(z_ref[:, koff + h * dk:koff + (h + 1) * dk].astype(F32))
        v = z_ref[:, voff + h * dv:voff + (h + 1) * dv]
        gate = z_ref[:, goff + h * dv:goff + (h + 1) * dv].astype(F32)
        qb = q.astype(BF16)
        dec = jnp.where(rel >= 0, jnp.exp(jnp.maximum(rel, 0.0) * lg), 0.0)
        a = _dot_nt(qb, k.astype(BF16)) * dec
        s_old = s_ref[0, h]
        o = _dot(a.astype(BF16), v) + jnp.exp((t_col + 1.0) * lg) * _dot(qb, s_old.astype(BF16))
        kd = k * jnp.exp((L - 1.0 - t_col) * lg)
        s_ref[0, h] = math.exp(L * lg) * s_old + _dot_tn(kd.astype(BF16), v)
        y = _silu(gate) * _rms(o, norm_ref[...])
        o_ref[:, h * dv:(h + 1) * dv] = y.astype(o_ref.dtype)


def _retention(z, norm, init, pos0, *, batch, seq_len, heads, dk, dv):
    L = min(CHUNK, seq_len)
    nc = seq_len // L
    has_init = init is not None
    half = dk // 2
    inv = ROPE_BASE ** (-jnp.arange(half, dtype=F32) / half)
    ang = (pos0 + jnp.arange(seq_len)).astype(F32)[:, None] * inv[None]
    row = lambda b, c: (b * nc + c, 0)
    in_specs = [pl.BlockSpec((L, half), lambda b, c: (c, 0)),
                pl.BlockSpec((L, half), lambda b, c: (c, 0)),
                pl.BlockSpec((L, z.shape[1]), row),
                pl.BlockSpec((1, dv), lambda b, c: (0, 0))]
    args = [jnp.cos(ang), jnp.sin(ang), z, norm.reshape(1, dv).astype(F32)]
    state_spec = pl.BlockSpec((1, heads, dk, dv), lambda b, c: (b, 0, 0, 0))
    if has_init:
        in_specs.append(state_spec)
        args.append(init)
    o, s_new = pl.pallas_call(
        functools.partial(_ret_kernel, heads=heads, dk=dk, dv=dv, has_init=has_init),
        grid=(batch, nc),
        in_specs=in_specs,
        out_specs=[pl.BlockSpec((L, heads * dv), row), state_spec],
        out_shape=[jax.ShapeDtypeStruct((batch * seq_len, heads * dv), BF16),
                   jax.ShapeDtypeStruct((batch, heads, dk, dv), F32)],
        compiler_params=pltpu.CompilerParams(
            dimension_semantics=("arbitrary", "arbitrary"), vmem_limit_bytes=VMEM_LIMIT_BYTES),
        name="retention",
    )(*args)
    return o, s_new


def _gla_kernel(*refs, heads, dk, dv, has_init):
    it = iter(refs)
    z_ref, zlr_ref, w2_ref, bg_ref, norm_ref = next(it), next(it), next(it), next(it), next(it)
    s0_ref = next(it) if has_init else None
    o_ref, s_ref, st_scr = next(it), next(it), next(it)
    ch = pl.program_id(1)
    nch = pl.num_programs(1)
    L = z_ref.shape[0]

    @pl.when(ch == 0)
    def _():
        for h in range(heads):
            st_scr[h] = s0_ref[0, h].T if has_init else jnp.zeros((dv, dk), F32)

    la = _log_sigmoid(_dot(zlr_ref[...].astype(BF16), w2_ref[...]) + bg_ref[...]) / GLA_TAU
    b = _cumsum_rows(la)
    tri, _ = _tri_masks(L)
    koff, voff, roff = heads * dk, 2 * heads * dk, 2 * heads * dk + heads * dv
    for h in range(heads):
        bh = b[:, h * dk:(h + 1) * dk]
        bl = bh[L - 1:L, :]
        q = z_ref[:, h * dk:(h + 1) * dk].astype(F32) * dk ** -0.5
        k = z_ref[:, koff + h * dk:koff + (h + 1) * dk].astype(F32)
        v = z_ref[:, voff + h * dv:voff + (h + 1) * dv]
        r = z_ref[:, roff + h * dv:roff + (h + 1) * dv].astype(F32)
        qd = (q * jnp.exp(bh)).astype(BF16)
        kd = (k * jnp.exp(-bh)).astype(BF16)
        a = jnp.where(tri, _dot_nt(qd, kd), 0.0)
        st_old = st_scr[h]
        o = _dot(a.astype(BF16), v) + _dot_nt(qd, st_old.astype(BF16))
        ke = (k * jnp.exp(bl - bh)).astype(BF16)
        st_scr[h] = jnp.exp(bl) * st_old + _dot_tn(v, ke)
        y = _silu(r) * _rms(o, norm_ref[...])
        o_ref[:, h * dv:(h + 1) * dv] = y.astype(o_ref.dtype)

    @pl.when(ch == nch - 1)
    def _():
        for h in range(heads):
            s_ref[0, h] = st_scr[h].T


def _gla(z, zlr, w_gate2, b_gate, norm, init, *, batch, seq_len, heads, dk, dv):
    L = min(CHUNK, seq_len)
    nc = seq_len // L
    has_init = init is not None
    hk = heads * dk
    w2 = jnp.zeros((LANES, hk), BF16).at[:GLA_RANK].set(w_gate2.astype(BF16))
    row = lambda b, c: (b * nc + c, 0)
    in_specs = [pl.BlockSpec((L, z.shape[1]), row),
                pl.BlockSpec((L, LANES), row),
                pl.BlockSpec((LANES, hk), lambda b, c: (0, 0)),
                pl.BlockSpec((1, hk), lambda b, c: (0, 0)),
                pl.BlockSpec((1, dv), lambda b, c: (0, 0))]
    args = [z, zlr, w2, b_gate.reshape(1, hk).astype(F32), norm.reshape(1, dv).astype(F32)]
    state_spec = pl.BlockSpec((1, heads, dk, dv), lambda b, c: (b, 0, 0, 0))
    if has_init:
        in_specs.append(state_spec)
        args.append(init)
    o, s_new = pl.pallas_call(
        functools.partial(_gla_kernel, heads=heads, dk=dk, dv=dv, has_init=has_init),
        grid=(batch, nc),
        in_specs=in_specs,
        out_specs=[pl.BlockSpec((L, heads * dv), row), state_spec],
        out_shape=[jax.ShapeDtypeStruct((batch * seq_len, heads * dv), BF16),
                   jax.ShapeDtypeStruct((batch, heads, dk, dv), F32)],
        scratch_shapes=[pltpu.VMEM((heads, dv, dk), F32)],
        compiler_params=pltpu.CompilerParams(
            dimension_semantics=("arbitrary", "arbitrary"), vmem_limit_bytes=VMEM_LIMIT_BYTES),
        name="gla",
    )(*args)
    return o, s_new


def _unit_lower_inverse(a):
    n = a.shape[0]
    r = lax.broadcasted_iota(jnp.int32, (n, n), 0)
    c = lax.broadcasted_iota(jnp.int32, (n, n), 1)
    p = jnp.where(r == c, 1.0, 0.0) - a
    pw = _dot(a.astype(BF16), a.astype(BF16))
    covered = 2
    while covered < n:
        if 2 * covered < n:
            both = _dot(jnp.concatenate([p, pw], axis=0).astype(BF16), pw.astype(BF16))
            p = p + both[:n]
            pw = both[n:]
        else:
            p = p + _dot(p.astype(BF16), pw.astype(BF16))
        covered *= 2
    return p


def _gdn_kernel(*refs, qk_heads, v_heads, d, has_init):
    it = iter(refs)
    z_ref, zba_ref, cw_ref, aneg_ref, dtb_ref, norm_ref = (next(it) for _ in range(6))
    if has_init:
        s0_ref, buf0_ref = next(it), next(it)
    o_ref, s_ref = next(it), next(it)
    xs_scr, gc_scr, gr_scr, bc_scr = next(it), next(it), next(it), next(it)
    ch = pl.program_id(1)
    L = z_ref.shape[0]
    rep = v_heads // qk_heads
    conv_ch = (2 * qk_heads + v_heads) * d
    width = cw_ref.shape[0]
    pad = SUBLANES

    @pl.when(ch == 0)
    def _():
        s_ref[...] = s0_ref[...] if has_init else jnp.zeros_like(s_ref)
        xs_scr[0:pad, :] = jnp.zeros((pad, conv_ch), F32)
        if has_init:
            xs_scr[pad - width + 1:pad, :] = buf0_ref[0]

    ba = zba_ref[...]
    lane = lax.broadcasted_iota(jnp.int32, ba.shape, 1)
    beta = _sigmoid(ba)
    la = aneg_ref[...] * _softplus(ba + dtb_ref[...])
    g = _cumsum_rows(jnp.where(lane >= v_heads, la, 0.0))
    gt = _transpose_rows(g)
    for j in range(v_heads):
        gc_scr[j] = jnp.broadcast_to(g[:, v_heads + j:v_heads + j + 1], (L, LANES))
        bc_scr[j] = jnp.broadcast_to(beta[:, j:j + 1], (L, LANES))
        gr_scr[j] = jnp.broadcast_to(gt[v_heads + j:v_heads + j + 1, :], (SUBLANES, L))

    causal, strict = _tri_masks(L)

    def conv_head(col):
        sl = pl.ds(col, d)
        xs_scr[pad:pad + L, sl] = z_ref[:, sl].astype(F32)
        w = cw_ref[:, sl]
        acc = xs_scr[pad:pad + L, sl] * w[width - 1:width]
        for t in range(1, width):
            acc = acc + xs_scr[pad - t:pad - t + L, sl] * w[width - 1 - t:width - t]
        xs_scr[0:pad, sl] = xs_scr[L:L + pad, sl]
        return _silu(acc)

    def l2n(x):
        return x * lax.rsqrt(jnp.sum(x * x, axis=-1, keepdims=True) + EPS)

    def head_pair(kh, carry):
        q = l2n(conv_head(pl.multiple_of(kh * d, d))) * d ** -0.5
        k = l2n(conv_head(pl.multiple_of((qk_heads + kh) * d, d)))
        qb, kb = q.astype(BF16), k.astype(BF16)
        kk = _dot_nt(kb, kb)
        qk = _dot_nt(qb, kb)
        for r in range(rep):
            j = kh * rep + r
            v = conv_head(pl.multiple_of((2 * qk_heads + j) * d, d))
            zg = z_ref[:, pl.ds(pl.multiple_of(conv_ch + j * d, d), d)].astype(F32)
            g_col = gc_scr[j]
            g_row = gr_scr[j][0:1, :]
            beta_col = bc_scr[j]
            dmat = jnp.exp(jnp.where(causal, g_col[:, :L] - g_row, -jnp.inf))
            a = jnp.where(strict, beta_col[:, :L] * dmat * kk, 0.0)
            tinv = _unit_lower_inverse(a)
            eg = jnp.exp(g_col)
            rhs = jnp.concatenate([beta_col * v, beta_col * eg * k], axis=-1)
            sol = _dot(tinv.astype(BF16), rhs.astype(BF16))
            s_old = s_ref[0, j]
            sb = s_old.astype(BF16)
            u = sol[:, :d] - _dot(sol[:, d:].astype(BF16), sb)
            ub = u.astype(BF16)
            o = eg * _dot(qb, sb) + _dot((qk * dmat).astype(BF16), ub)
            g_last = g_col[L - 1:L, :]
            s_ref[0, j] = jnp.exp(g_last) * s_old + _dot_tn((jnp.exp(g_last - g_col) * k).astype(BF16), ub)
            y = _rms(o, norm_ref[...]) * _silu(zg)
            o_ref[:, pl.ds(pl.multiple_of(j * d, d), d)] = y.astype(o_ref.dtype)
        return carry

    lax.fori_loop(0, qk_heads, head_pair, 0)


def _gdn(z, zba, conv_w, a_log, dt_bias, norm, init, *, batch, seq_len, qk_heads, v_heads, d):
    L = min(CHUNK, seq_len)
    nc = seq_len // L
    has_init = init is not None
    conv_ch = (2 * qk_heads + v_heads) * d
    width = conv_w.shape[0]
    lane_pad = lambda x: jnp.zeros((1, LANES), F32).at[0, v_heads:2 * v_heads].set(x.astype(F32))
    row = lambda b, c: (b * nc + c, 0)
    const = lambda b, c: (0, 0)
    in_specs = [pl.BlockSpec((L, z.shape[1]), row),
                pl.BlockSpec((L, LANES), row),
                pl.BlockSpec((width, conv_ch), const),
                pl.BlockSpec((1, LANES), const),
                pl.BlockSpec((1, LANES), const),
                pl.BlockSpec((1, d), const)]
    args = [z, zba, conv_w.astype(F32), lane_pad(-jnp.exp(a_log)), lane_pad(dt_bias),
            norm.reshape(1, d).astype(F32)]
    state_spec = pl.BlockSpec((1, v_heads, d, d), lambda b, c: (b, 0, 0, 0))
    if has_init:
        in_specs += [state_spec, pl.BlockSpec((1, width - 1, conv_ch), lambda b, c: (b, 0, 0))]
        args += [init[0], init[1]]
    o, s_new = pl.pallas_call(
        functools.partial(_gdn_kernel, qk_heads=qk_heads, v_heads=v_heads, d=d, has_init=has_init),
        grid=(batch, nc),
        in_specs=in_specs,
        out_specs=[pl.BlockSpec((L, v_heads * d), row), state_spec],
        out_shape=[jax.ShapeDtypeStruct((batch * seq_len, v_heads * d), BF16),
                   jax.ShapeDtypeStruct((batch, v_heads, d, d), F32)],
        scratch_shapes=[pltpu.VMEM((L + SUBLANES, conv_ch), F32),
                        pltpu.VMEM((v_heads, L, LANES), F32),
                        pltpu.VMEM((v_heads, SUBLANES, L), F32),
                        pltpu.VMEM((v_heads, L, LANES), F32)],
        compiler_params=pltpu.CompilerParams(
            dimension_semantics=("arbitrary", "arbitrary"), vmem_limit_bytes=VMEM_LIMIT_BYTES),
        name="gdn",
    )(*args)
    return o, s_new


def _pad_cols(w, n):
    return jnp.zeros((w.shape[0], n), F32).at[:, :w.shape[1]].set(w.astype(F32))


def _trunk(x, states, pos0, p, *, batch, seq_len):
    m, dm = x.shape
    has_init = states is not None
    if has_init:
        mC, mn, mm, rS, gS, dS, dconv, fconv = states
    depth = p['norm_mix'].shape[0]
    outs = [[] for _ in range(8)]
    for i in range(depth):
        kind, j = i % 4, i // 4
        g_mix = p['norm_mix'][i]
        if kind == 0:
            heads, dk, dv = p['mlstm_n_shape']
            w_in = p['mlstm_w_in'][j]
            nm = 2 * heads * dk + 2 * heads * dv
            z = _matmul(x, w_in[:, :nm].astype(BF16), norm_g=g_mix, name="mlstm_in")
            zg = _gate_matmul(x, g_mix, _pad_cols(w_in[:, nm:], LANES), name="mlstm_gates")
            bias = _pad_cols(p['mlstm_b_gates'][j][None], LANES)
            init = (mC[j], mn[j], mm[j]) if has_init else None
            a, (c_, n_, m_) = _mlstm(z, zg, bias, p['mlstm_norm'][j], init,
                                     batch=batch, seq_len=seq_len, heads=heads, dk=dk, dv=dv)
            outs[0].append(c_); outs[1].append(n_); outs[2].append(m_)
            w_out = p['mlstm_w_out'][j]
        elif kind == 1:
            heads, dk, dv = p['ret_shape']
            z = _matmul(x, p['ret_w_in'][j].astype(BF16), norm_g=g_mix, name="ret_in")
            a, s_ = _retention(z, p['ret_norm'][j], rS[j] if has_init else None, pos0,
                               batch=batch, seq_len=seq_len, heads=heads, dk=dk, dv=dv)
            outs[3].append(s_)
            w_out = p['ret_w_out'][j]
        elif kind == 2:
            heads, dk, dv = p['gla_shape']
            w_in = p['gla_w_in'][j]
            nm = 2 * heads * dk + 2 * heads * dv
            z = _matmul(x, w_in[:, :nm].astype(BF16), norm_g=g_mix, name="gla_in")
            zlr = _gate_matmul(x, g_mix, _pad_cols(w_in[:, nm:], LANES), name="gla_lowrank")
            a, s_ = _gla(z, zlr, p['gla_w_gate2'][j], p['gla_b_gate'][j], p['gla_norm'][j],
                         gS[j] if has_init else None,
                         batch=batch, seq_len=seq_len, heads=heads, dk=dk, dv=dv)
            outs[4].append(s_)
            w_out = p['gla_w_out'][j]
        else:
            qk_heads, v_heads, d = p['gdn_shape']
            w_in = p['gdn_w_in'][j]
            conv_ch = (2 * qk_heads + v_heads) * d
            nm = conv_ch + v_heads * d
            width = p['gdn_conv_w'].shape[1]
            z = _matmul(x, w_in[:, :nm].astype(BF16), norm_g=g_mix, name="gdn_in")
            zba = _gate_matmul(x, g_mix, _pad_cols(w_in[:, nm:], LANES), name="gdn_gates")
            init = (dS[j], dconv[j]) if has_init else None
            a, s_ = _gdn(z, zba, p['gdn_conv_w'][j], p['gdn_A_log'][j], p['gdn_dt_bias'][j], p['gdn_norm'][j],
                         init, batch=batch, seq_len=seq_len, qk_heads=qk_heads, v_heads=v_heads, d=d)
            outs[5].append(s_)
            outs[6].append(z.reshape(batch, seq_len, nm)[:, seq_len - (width - 1):, :conv_ch].astype(F32))
            w_out = p['gdn_w_out'][j]
        x = _matmul(a, w_out.astype(BF16), residual=x, out_dtype=F32, name="mix_out")

        f = p['ffn_w_gate'].shape[2]
        fw = p['ffn_conv_w'].shape[1]
        w_gu = jnp.concatenate([p['ffn_w_gate'][i], p['ffn_w_up'][i]], axis=1).astype(BF16)
        gu = _matmul(x, w_gu, norm_g=p['norm_ffn'][i], name="ffn_in")
        buf8 = jnp.zeros((batch, SUBLANES, f), F32)
        if has_init:
            buf8 = buf8.at[:, SUBLANES - (fw - 1):].set(fconv[i].astype(F32))
        act = _ffn_act(gu, buf8, p['ffn_conv_w'][i], p['ffn_conv_b'][i], seq_len=seq_len)
        outs[7].append(gu.reshape(batch, seq_len, 2 * f)[:, seq_len - (fw - 1):, :f].astype(F32))
        x = _matmul(act, p['ffn_w_down'][i].astype(BF16), residual=x, out_dtype=F32, name="ffn_out")
    y = _final_norm(x, p['norm_final'])
    return y.reshape(batch, seq_len, dm), tuple(jnp.stack(o) for o in outs)


def kernel(x_prompt, x_sample, state_mlstm_C, state_mlstm_n, state_mlstm_m, state_ret_S, state_gla_S, state_gdn_S, state_gdn_conv, state_ffn_conv, norm_mix, norm_ffn, norm_final, mlstm_w_in, mlstm_b_gates, mlstm_norm, mlstm_w_out, ret_w_in, ret_norm, ret_w_out, gla_w_in, gla_w_gate2, gla_b_gate, gla_norm, gla_w_out, gdn_w_in, gdn_conv_w, gdn_A_log, gdn_dt_bias, gdn_norm, gdn_w_out, ffn_w_gate, ffn_w_up, ffn_conv_w, ffn_conv_b, ffn_w_down):
    p = dict(norm_mix=norm_mix, norm_ffn=norm_ffn, norm_final=norm_final,
             mlstm_w_in=mlstm_w_in, mlstm_b_gates=mlstm_b_gates, mlstm_norm=mlstm_norm, mlstm_w_out=mlstm_w_out,
             ret_w_in=ret_w_in, ret_norm=ret_norm, ret_w_out=ret_w_out,
             gla_w_in=gla_w_in, gla_w_gate2=gla_w_gate2, gla_b_gate=gla_b_gate, gla_norm=gla_norm, gla_w_out=gla_w_out,
             gdn_w_in=gdn_w_in, gdn_conv_w=gdn_conv_w, gdn_A_log=gdn_A_log, gdn_dt_bias=gdn_dt_bias,
             gdn_norm=gdn_norm, gdn_w_out=gdn_w_out,
             ffn_w_gate=ffn_w_gate, ffn_w_up=ffn_w_up, ffn_conv_w=ffn_conv_w, ffn_conv_b=ffn_conv_b,
             ffn_w_down=ffn_w_down)
    p['mlstm_n_shape'] = state_mlstm_C.shape[2:]
    p['ret_shape'] = state_ret_S.shape[2:]
    p['gla_shape'] = state_gla_S.shape[2:]
    d = state_gdn_S.shape[-1]
    v_heads = state_gdn_S.shape[2]
    p['gdn_shape'] = ((state_gdn_conv.shape[-1] // d - v_heads) // 2, v_heads, d)

    bp, tp, dm = x_prompt.shape
    bs, ts, _ = x_sample.shape
    y_prompt, p_states = _trunk(x_prompt.reshape(bp * tp, dm), None, 0, p, batch=bp, seq_len=tp)
    cache = (state_mlstm_C, state_mlstm_n, state_mlstm_m, state_ret_S, state_gla_S, state_gdn_S,
             state_gdn_conv, state_ffn_conv)
    y_sample, s_states = _trunk(x_sample.reshape(bs * ts, dm), cache, PAST_LEN, p, batch=bs, seq_len=ts)
    return (y_prompt, y_sample) + p_states + s_states
```

```python
import functools
import math

import jax
import jax.numpy as jnp
from jax import lax
from jax.experimental import pallas as pl
from jax.experimental.pallas import tpu as pltpu

F32 = jnp.float32
BF16 = jnp.bfloat16
EPS = 1e-6
PAST_LEN = 2048
ROPE_BASE = 10000.0
GLA_TAU = 16.0
GLA_RANK = 16
CHUNK = 64
GDN_V_HEADS_PER_STEP = 16
FFN_ROW_CHUNK = 256
LANES = 128
SUBLANES = 8
VMEM_LIMIT_BYTES = 56 * 1024 * 1024


def _sigmoid(x):
    return 1.0 / (1.0 + jnp.exp(-x))


def _silu(x):
    return x * _sigmoid(x)


def _log1p_exp_neg_abs(x):
    return jnp.log(1.0 + jnp.exp(-jnp.abs(x)))


def _log_sigmoid(x):
    return jnp.minimum(x, 0.0) - _log1p_exp_neg_abs(x)


def _softplus(x):
    return jnp.maximum(x, 0.0) + _log1p_exp_neg_abs(x)


def _rms(x, g):
    return x * lax.rsqrt(jnp.mean(x * x, axis=-1, keepdims=True) + EPS) * g


def _dot(a, b):
    return jnp.dot(a, b, preferred_element_type=F32)


def _dot_nt(a, b):
    return lax.dot_general(a, b, (((1,), (1,)), ((), ())), preferred_element_type=F32)


def _dot_tn(a, b):
    return lax.dot_general(a, b, (((0,), (0,)), ((), ())), preferred_element_type=F32)


def _split3(x):
    hi = x.astype(BF16)
    r = x - hi.astype(F32)
    mid = r.astype(BF16)
    lo = (r - mid.astype(F32)).astype(BF16)
    return hi, mid, lo


def _cumsum_rows(x):
    n = x.shape[0]
    r = lax.broadcasted_iota(jnp.int32, (n, n), 0)
    c = lax.broadcasted_iota(jnp.int32, (n, n), 1)
    tril = jnp.where(r >= c, 1.0, 0.0).astype(BF16)
    hi, mid, lo = _split3(x)
    return _dot(tril, hi) + _dot(tril, mid) + _dot(tril, lo)


def _transpose_rows(x):
    n = x.shape[0]
    if n < LANES:
        x = jnp.concatenate([x, jnp.zeros((LANES - n, x.shape[1]), x.dtype)], axis=0)
    return x.T[:, :n]


def _tri_masks(n):
    r = lax.broadcasted_iota(jnp.int32, (n, n), 0)
    c = lax.broadcasted_iota(jnp.int32, (n, n), 1)
    return r >= c, r > c


def _mm_kernel(*refs, has_norm, has_res, has_gate):
    it = iter(refs)
    x_ref = next(it)
    g_ref = next(it) if has_norm else None
    w_ref = next(it)
    wgate_ref = next(it) if has_gate else None
    r_ref = next(it) if has_res else None
    o_ref = next(it)
    ogate_ref = next(it) if has_gate else None
    h_scr = next(it) if has_norm else None
    if has_norm:
        @pl.when(pl.program_id(1) == 0)
        def _():
            h = _rms(x_ref[...].astype(F32), g_ref[...])
            h_scr[...] = h.astype(BF16)
            if has_gate:
                hh, hm, _ = _split3(h)
                wg = wgate_ref[...]
                wh = wg.astype(BF16)
                wl = (wg - wh.astype(F32)).astype(BF16)
                ogate_ref[...] = _dot(hh, wh) + _dot(hm, wh) + _dot(hh, wl)
        h = h_scr[...]
    else:
        h = x_ref[...]
    acc = _dot(h, w_ref[...])
    if has_res:
        acc = acc + r_ref[...]
    o_ref[...] = acc.astype(o_ref.dtype)


def _pick(n, cands):
    for c in cands:
        if n % c == 0:
            return c
    raise ValueError(f"no tile for {n}")


def _matmul(x, w, *, norm_g=None, gate_w=None, residual=None, out_dtype=BF16, bm=1024, bn=None, name="mm"):
    m, k = x.shape
    n = w.shape[1]
    bm = min(bm, m)
    if bn is None:
        bn = _pick(n, (1024, 512, 256, 128)) if k <= 2048 else _pick(n, (512, 256, 128))
    assert m % bm == 0 and n % bn == 0
    has_norm, has_res, has_gate = norm_g is not None, residual is not None, gate_w is not None
    assert has_norm or not has_gate
    in_specs = [pl.BlockSpec((bm, k), lambda i, j: (i, 0))]
    args = [x]
    if has_norm:
        in_specs.append(pl.BlockSpec((1, k), lambda i, j: (0, 0)))
        args.append(norm_g.reshape(1, k).astype(F32))
    in_specs.append(pl.BlockSpec((k, bn), lambda i, j: (0, j)))
    args.append(w)
    if has_gate:
        in_specs.append(pl.BlockSpec((k, LANES), lambda i, j: (0, 0)))
        args.append(gate_w)
    if has_res:
        in_specs.append(pl.BlockSpec((bm, bn), lambda i, j: (i, j)))
        args.append(residual)
    out_specs = [pl.BlockSpec((bm, bn), lambda i, j: (i, j))]
    out_shape = [jax.ShapeDtypeStruct((m, n), out_dtype)]
    if has_gate:
        out_specs.append(pl.BlockSpec((bm, LANES), lambda i, j: (i, 0)))
        out_shape.append(jax.ShapeDtypeStruct((m, LANES), F32))
    outs = pl.pallas_call(
        functools.partial(_mm_kernel, has_norm=has_norm, has_res=has_res, has_gate=has_gate),
        grid=(m // bm, n // bn),
        in_specs=in_specs,
        out_specs=out_specs,
        out_shape=out_shape,
        scratch_shapes=[pltpu.VMEM((bm, k), BF16)] if has_norm else [],
        compiler_params=pltpu.CompilerParams(
            dimension_semantics=("arbitrary", "arbitrary"), vmem_limit_bytes=VMEM_LIMIT_BYTES),
        name=name,
    )(*args)
    return outs if has_gate else outs[0]


def _final_norm_kernel(x_ref, g_ref, o_ref):
    o_ref[...] = _rms(x_ref[...], g_ref[...])


def _final_norm(x, g, *, bm=512):
    m, k = x.shape
    bm = min(bm, m)
    return pl.pallas_call(
        _final_norm_kernel,
        grid=(m // bm,),
        in_specs=[pl.BlockSpec((bm, k), lambda i: (i, 0)), pl.BlockSpec((1, k), lambda i: (0, 0))],
        out_specs=pl.BlockSpec((bm, k), lambda i: (i, 0)),
        out_shape=jax.ShapeDtypeStruct((m, k), F32),
        compiler_params=pltpu.CompilerParams(
            dimension_semantics=("arbitrary",), vmem_limit_bytes=VMEM_LIMIT_BYTES),
        name="final_norm",
    )(x, g.reshape(1, k).astype(F32))


def _ffn_in_kernel(x_ref, g_ref, wg_ref, wu_ref, buf_ref, cw_ref, cb_ref, act_ref, tail_ref, h_scr, halo_scr,
                   *, tiles_per_seq, seq_rows):
    i, j = pl.program_id(0), pl.program_id(1)
    bm, bn = act_ref.shape

    @pl.when(j == 0)
    def _():
        h_scr[...] = _rms(x_ref[...], g_ref[...]).astype(BF16)

    nseq = bm // seq_rows
    first = (i % tiles_per_seq) == 0
    halo = jnp.where(first, buf_ref[...], halo_scr[j])
    prev2, prev1 = halo[:, 6:7], halo[:, 7:8]
    cw, cb = cw_ref[...], cb_ref[...]
    wg, wu = wg_ref[...], wu_ref[...]
    rc = min(FFN_ROW_CHUNK, bm)
    spc = max(rc // seq_rows, 1)
    tt = rc // spc
    t = lax.broadcasted_iota(jnp.int32, (spc, tt, bn), 1)
    tails = []
    for r in range(bm // rc):
        rows = slice(r * rc, (r + 1) * rc)
        h = h_scr[rows, :]
        g3 = _dot(h, wg).reshape(spc, tt, bn)
        u3 = _dot(h, wu).reshape(spc, tt, bn)
        p2, p1 = (prev2, prev1) if nseq == 1 else (prev2[r * spc:(r + 1) * spc], prev1[r * spc:(r + 1) * spc])
        g1 = jnp.where(t == 0, p1, pltpu.roll(g3, 1, 1))
        g2 = jnp.where(t == 0, p2, jnp.where(t == 1, p1, pltpu.roll(g3, 2, 1)))
        gc = g2 * cw[0:1] + g1 * cw[1:2] + g3 * cw[2:3] + cb
        act_ref[rows, :] = (_silu(gc) * u3).reshape(rc, bn).astype(act_ref.dtype)
        if nseq == 1:
            prev2, prev1 = g3[:, tt - 2:tt - 1], g3[:, tt - 1:tt]
            tails = [g3[:, tt - SUBLANES:]]
        else:
            tails.append(g3[:, tt - SUBLANES:])
    tail = tails[0] if len(tails) == 1 else jnp.concatenate(tails, axis=0)
    halo_scr[j] = tail
    tail_ref[...] = tail


def _ffn_in(x, norm_g, w_gate, w_up, buf8, conv_w, conv_b, *, seq_len, bm=1024, bn=512):
    m, k = x.shape
    f = w_gate.shape[1]
    batch = m // seq_len
    bm = min(bm, m)
    assert f % bn == 0 and m % bm == 0 and (seq_len % bm == 0 or bm % seq_len == 0)
    nf = f // bn
    tiles_per_seq = max(seq_len // bm, 1)
    seq_rows = min(seq_len, bm)
    nseq = bm // seq_rows
    seq_blk = lambda i, j: (i // tiles_per_seq, 0, j)
    act, tails = pl.pallas_call(
        functools.partial(_ffn_in_kernel, tiles_per_seq=tiles_per_seq, seq_rows=seq_rows),
        grid=(m // bm, nf),
        in_specs=[
            pl.BlockSpec((bm, k), lambda i, j: (i, 0)),
            pl.BlockSpec((1, k), lambda i, j: (0, 0)),
            pl.BlockSpec((k, bn), lambda i, j: (0, j)),
            pl.BlockSpec((k, bn), lambda i, j: (0, j)),
            pl.BlockSpec((nseq, SUBLANES, bn), seq_blk),
            pl.BlockSpec((3, bn), lambda i, j: (0, j)),
            pl.BlockSpec((1, bn), lambda i, j: (0, j)),
        ],
        out_specs=[pl.BlockSpec((bm, bn), lambda i, j: (i, j)),
                   pl.BlockSpec((nseq, SUBLANES, bn), lambda i, j: (i, 0, j))],
        out_shape=[jax.ShapeDtypeStruct((m, f), BF16),
                   jax.ShapeDtypeStruct((m // bm * nseq, SUBLANES, f), F32)],
        scratch_shapes=[pltpu.VMEM((bm, k), BF16), pltpu.VMEM((nf, nseq, SUBLANES, bn), F32)],
        compiler_params=pltpu.CompilerParams(
            dimension_semantics=("arbitrary", "arbitrary"), vmem_limit_bytes=VMEM_LIMIT_BYTES),
        name="ffn_in",
    )(x, norm_g.reshape(1, k).astype(F32), w_gate, w_up, buf8, conv_w.astype(F32),
      conv_b.reshape(1, f).astype(F32))
    return act, tails.reshape(batch, tiles_per_seq, SUBLANES, f)[:, tiles_per_seq - 1]


def _mlstm_kernel(*refs, heads, dk, dv, has_init):
    it = iter(refs)
    z_ref, zg_ref, bias_ref, norm_ref = next(it), next(it), next(it), next(it)
    if has_init:
        c0_ref, n0_ref, m0_ref = next(it), next(it), next(it)
    o_ref, c_ref, n_ref, m_ref = next(it), next(it), next(it), next(it)
    ch = pl.program_id(1)
    L = z_ref.shape[0]

    @pl.when(ch == 0)
    def _():
        if has_init:
            c_ref[...] = c0_ref[...]
            n_ref[...] = n0_ref[...]
            m_ref[...] = m0_ref[...]
        else:
            c_ref[...] = jnp.zeros_like(c_ref)
            n_ref[...] = jnp.zeros_like(n_ref)
            m_ref[...] = jnp.zeros_like(m_ref)

    g = zg_ref[...] + bias_ref[...]
    lane = lax.broadcasted_iota(jnp.int32, g.shape, 1)
    u = jnp.where(lane < heads, g, _cumsum_rows(_log_sigmoid(g)))
    ut = _transpose_rows(u)
    tri, _ = _tri_masks(L)
    m_all = m_ref[0]
    m_lane = lax.broadcasted_iota(jnp.int32, m_all.shape, 1)
    qoff, koff, voff, ooff = 0, heads * dk, 2 * heads * dk, 2 * heads * dk + heads * dv
    for h in range(heads):
        ig_col, ig_row = u[:, h:h + 1], ut[h:h + 1, :]
        b_col, b_row = u[:, heads + h:heads + h + 1], ut[heads + h:heads + h + 1, :]
        m_prev = m_all[:, h:h + 1]
        q = (z_ref[:, qoff + h * dk:qoff + (h + 1) * dk].astype(F32) * dk ** -0.5).astype(BF16)
        k = z_ref[:, koff + h * dk:koff + (h + 1) * dk]
        v = z_ref[:, voff + h * dv:voff + (h + 1) * dv]
        og = z_ref[:, ooff + h * dv:ooff + (h + 1) * dv].astype(F32)
        d = jnp.where(tri, b_col - b_row + ig_row, -jnp.inf)
        li = b_col + m_prev
        m_t = jnp.maximum(li, jnp.max(d, axis=1, keepdims=True))
        w_intra = jnp.exp(d - m_t)
        w_inter = jnp.exp(li - m_t)
        a = _dot_nt(q, k) * w_intra
        c_old = c_ref[0, h]
        n_old = n_ref[0, h:h + 1, :]
        num = _dot(a.astype(BF16), v) + w_inter * _dot(q, c_old.astype(BF16))
        den = jnp.sum(a, axis=1, keepdims=True) + w_inter * jnp.sum(q.astype(F32) * n_old, axis=1, keepdims=True)
        hh = num / jnp.maximum(jnp.abs(den), jnp.exp(-m_t))
        m_new = m_t[L - 1:L, :]
        b_last = b_col[L - 1:L, :]
        w_k = jnp.exp(b_last - b_col + ig_col - m_new)
        dec = jnp.exp(b_last + m_prev - m_new)
        kw = k.astype(F32) * w_k
        c_ref[0, h] = dec * c_old + _dot_tn(kw.astype(BF16), v)
        n_ref[0, h:h + 1, :] = dec * n_old + jnp.sum(kw, axis=0, keepdims=True)
        m_all = jnp.where(m_lane == h, m_new, m_all)
        y = _sigmoid(og) * _rms(hh, norm_ref[...])
        o_ref[:, h * dv:(h + 1) * dv] = y.astype(o_ref.dtype)
    m_ref[0] = m_all


def _mlstm(z, zg, bias, norm, init, *, batch, seq_len, heads, dk, dv):
    L = min(CHUNK, seq_len)
    nc = seq_len // L
    has_init = init is not None
    row = lambda b, c: (b * nc + c, 0)
    in_specs = [pl.BlockSpec((L, z.shape[1]), row),
                pl.BlockSpec((L, LANES), row),
                pl.BlockSpec((1, LANES), lambda b, c: (0, 0)),
                pl.BlockSpec((1, dv), lambda b, c: (0, 0))]
    args = [z, zg, bias, norm.reshape(1, dv).astype(F32)]
    state_specs = [pl.BlockSpec((1, heads, dk, dv), lambda b, c: (b, 0, 0, 0)),
                   pl.BlockSpec((1, heads, dk), lambda b, c: (b, 0, 0)),
                   pl.BlockSpec((1, 1, heads), lambda b, c: (b, 0, 0))]
    if has_init:
        in_specs += state_specs
        args += [init[0], init[1], init[2].reshape(batch, 1, heads)]
    o, c_new, n_new, m_new = pl.pallas_call(
        functools.partial(_mlstm_kernel, heads=heads, dk=dk, dv=dv, has_init=has_init),
        grid=(batch, nc),
        in_specs=in_specs,
        out_specs=[pl.BlockSpec((L, heads * dv), row)] + state_specs,
        out_shape=[jax.ShapeDtypeStruct((batch * seq_len, heads * dv), BF16),
                   jax.ShapeDtypeStruct((batch, heads, dk, dv), F32),
                   jax.ShapeDtypeStruct((batch, heads, dk), F32),
                   jax.ShapeDtypeStruct((batch, 1, heads), F32)],
        compiler_params=pltpu.CompilerParams(
            dimension_semantics=("arbitrary", "arbitrary"), vmem_limit_bytes=VMEM_LIMIT_BYTES),
        name="mlstm",
    )(*args)
    return o, (c_new, n_new, m_new.reshape(batch, heads))


def _ret_kernel(*refs, heads, dk, dv, has_init):
    it = iter(refs)
    cos_ref, sin_ref, z_ref, norm_ref = next(it), next(it), next(it), next(it)
    s0_ref = next(it) if has_init else None
    o_ref, s_ref = next(it), next(it)
    ch = pl.program_id(1)
    L = z_ref.shape[0]
    half = dk // 2

    @pl.when(ch == 0)
    def _():
        s_ref[...] = s0_ref[...] if has_init else jnp.zeros_like(s_ref)

    cos, sin = cos_ref[...], sin_ref[...]
    t_col = lax.broadcasted_iota(jnp.int32, (L, 1), 0).astype(F32)
    rel = (lax.broadcasted_iota(jnp.int32, (L, L), 0) - lax.broadcasted_iota(jnp.int32, (L, L), 1)).astype(F32)
    koff, voff, goff = heads * dk, 2 * heads * dk, 2 * heads * dk + heads * dv

    def rope(x):
        x1, x2 = x[:, :half], x[:, half:]
        return jnp.concatenate([x1 * cos - x2 * sin, x1 * sin + x2 * cos], axis=-1)

    for h in range(heads):
        lg = math.log1p(-2.0 ** (-5.0 - h))
        q = rope(z_ref[:, h * dk:(h + 1) * dk].astype(F32)) * dk ** -0.5
        k = rope(z_ref[:, koff + h * dk:koff + (h + 1) * dk].astype(F32))
        v = z_ref[:, voff + h * dv:voff + (h + 1) * dv]
        gate = z_ref[:, goff + h * dv:goff + (h + 1) * dv].astype(F32)
        qb = q.astype(BF16)
        dec = jnp.where(rel >= 0, jnp.exp(jnp.maximum(rel, 0.0) * lg), 0.0)
        a = _dot_nt(qb, k.astype(BF16)) * dec
        s_old = s_ref[0, h]
        o = _dot(a.astype(BF16), v) + jnp.exp((t_col + 1.0) * lg) * _dot(qb, s_old.astype(BF16))
        kd = k * jnp.exp((L - 1.0 - t_col) * lg)
        s_ref[0, h] = math.exp(L * lg) * s_old + _dot_tn(kd.astype(BF16), v)
        y = _silu(gate) * _rms(o, norm_ref[...])
        o_ref[:, h * dv:(h + 1) * dv] = y.astype(o_ref.dtype)


def _retention(z, norm, init, pos0, *, batch, seq_len, heads, dk, dv):
    L = min(CHUNK, seq_len)
    nc = seq_len // L
    has_init = init is not None
    half = dk // 2
    inv = ROPE_BASE ** (-jnp.arange(half, dtype=F32) / half)
    ang = (pos0 + jnp.arange(seq_len)).astype(F32)[:, None] * inv[None]
    row = lambda b, c: (b * nc + c, 0)
    in_specs = [pl.BlockSpec((L, half), lambda b, c: (c, 0)),
                pl.BlockSpec((L, half), lambda b, c: (c, 0)),
                pl.BlockSpec((L, z.shape[1]), row),
                pl.BlockSpec((1, dv), lambda b, c: (0, 0))]
    args = [jnp.cos(ang), jnp.sin(ang), z, norm.reshape(1, dv).astype(F32)]
    state_spec = pl.BlockSpec((1, heads, dk, dv), lambda b, c: (b, 0, 0, 0))
    if has_init:
        in_specs.append(state_spec)
        args.append(init)
    o, s_new = pl.pallas_call(
        functools.partial(_ret_kernel, heads=heads, dk=dk, dv=dv, has_init=has_init),
        grid=(batch, nc),
        in_specs=in_specs,
        out_specs=[pl.BlockSpec((L, heads * dv), row), state_spec],
        out_shape=[jax.ShapeDtypeStruct((batch * seq_len, heads * dv), BF16),
                   jax.ShapeDtypeStruct((batch, heads, dk, dv), F32)],
        compiler_params=pltpu.CompilerParams(
            dimension_semantics=("arbitrary", "arbitrary"), vmem_limit_bytes=VMEM_LIMIT_BYTES),
        name="retention",
    )(*args)
    return o, s_new


def _gla_kernel(*refs, heads, dk, dv, has_init):
    it = iter(refs)
    z_ref, zlr_ref, w2_ref, bg_ref, norm_ref = next(it), next(it), next(it), next(it), next(it)
    s0_ref = next(it) if has_init else None
    o_ref, s_ref, st_scr = next(it), next(it), next(it)
    ch = pl.program_id(1)
    nch = pl.num_programs(1)
    L = z_ref.shape[0]

    @pl.when(ch == 0)
    def _():
        for h in range(heads):
            st_scr[h] = s0_ref[0, h].T if has_init else jnp.zeros((dv, dk), F32)

    la = _log_sigmoid(_dot(zlr_ref[...].astype(BF16), w2_ref[...]) + bg_ref[...]) / GLA_TAU
    b = _cumsum_rows(la)
    tri, _ = _tri_masks(L)
    koff, voff, roff = heads * dk, 2 * heads * dk, 2 * heads * dk + heads * dv
    for h in range(heads):
        bh = b[:, h * dk:(h + 1) * dk]
        bl = bh[L - 1:L, :]
        q = z_ref[:, h * dk:(h + 1) * dk].astype(F32) * dk ** -0.5
        k = z_ref[:, koff + h * dk:koff + (h + 1) * dk].astype(F32)
        v = z_ref[:, voff + h * dv:voff + (h + 1) * dv]
        r = z_ref[:, roff + h * dv:roff + (h + 1) * dv].astype(F32)
        qd = (q * jnp.exp(bh)).astype(BF16)
        kd = (k * jnp.exp(-bh)).astype(BF16)
        a = jnp.where(tri, _dot_nt(qd, kd), 0.0)
        st_old = st_scr[h]
        o = _dot(a.astype(BF16), v) + _dot_nt(qd, st_old.astype(BF16))
        ke = (k * jnp.exp(bl - bh)).astype(BF16)
        st_scr[h] = jnp.exp(bl) * st_old + _dot_tn(v, ke)
        y = _silu(r) * _rms(o, norm_ref[...])
        o_ref[:, h * dv:(h + 1) * dv] = y.astype(o_ref.dtype)

    @pl.when(ch == nch - 1)
    def _():
        for h in range(heads):
            s_ref[0, h] = st_scr[h].T


def _gla(z, zlr, w_gate2, b_gate, norm, init, *, batch, seq_len, heads, dk, dv):
    L = min(CHUNK, seq_len)
    nc = seq_len // L
    has_init = init is not None
    hk = heads * dk
    w2 = jnp.zeros((LANES, hk), BF16).at[:GLA_RANK].set(w_gate2.astype(BF16))
    row = lambda b, c: (b * nc + c, 0)
    in_specs = [pl.BlockSpec((L, z.shape[1]), row),
                pl.BlockSpec((L, LANES), row),
                pl.BlockSpec((LANES, hk), lambda b, c: (0, 0)),
                pl.BlockSpec((1, hk), lambda b, c: (0, 0)),
                pl.BlockSpec((1, dv), lambda b, c: (0, 0))]
    args = [z, zlr, w2, b_gate.reshape(1, hk).astype(F32), norm.reshape(1, dv).astype(F32)]
    state_spec = pl.BlockSpec((1, heads, dk, dv), lambda b, c: (b, 0, 0, 0))
    if has_init:
        in_specs.append(state_spec)
        args.append(init)
    o, s_new = pl.pallas_call(
        functools.partial(_gla_kernel, heads=heads, dk=dk, dv=dv, has_init=has_init),
        grid=(batch, nc),
        in_specs=in_specs,
        out_specs=[pl.BlockSpec((L, heads * dv), row), state_spec],
        out_shape=[jax.ShapeDtypeStruct((batch * seq_len, heads * dv), BF16),
                   jax.ShapeDtypeStruct((batch, heads, dk, dv), F32)],
        scratch_shapes=[pltpu.VMEM((heads, dv, dk), F32)],
        compiler_params=pltpu.CompilerParams(
            dimension_semantics=("arbitrary", "arbitrary"), vmem_limit_bytes=VMEM_LIMIT_BYTES),
        name="gla",
    )(*args)
    return o, s_new


def _unit_lower_inverses(mats):
    n = mats[0].shape[0]
    r = lax.broadcasted_iota(jnp.int32, (n, n), 0)
    c = lax.broadcasted_iota(jnp.int32, (n, n), 1)
    eye = jnp.where(r == c, 1.0, 0.0)
    ps = [eye - a for a in mats]
    pws = [_dot(a.astype(BF16), a.astype(BF16)) for a in mats]
    covered = 2
    while covered < n:
        if 2 * covered < n:
            both = [_dot(jnp.concatenate([p, pw], axis=0).astype(BF16), pw.astype(BF16)) for p, pw in zip(ps, pws)]
            ps = [p + b[:n] for p, b in zip(ps, both)]
            pws = [b[n:] for b in both]
        else:
            ps = [p + _dot(p.astype(BF16), pw.astype(BF16)) for p, pw in zip(ps, pws)]
        covered *= 2
    return ps


def _gdn_kernel(*refs, qk_heads, v_heads, d, has_init):
    it = iter(refs)
    z_ref, zba_ref, cw_ref, aneg_ref, dtb_ref, norm_ref = (next(it) for _ in range(6))
    if has_init:
        s0_ref, buf0_ref = next(it), next(it)
    o_ref, s_ref = next(it), next(it)
    xs_scr, qkv_scr, gc_scr, gr_scr, bc_scr = (next(it) for _ in range(5))
    ch = pl.program_id(1)
    L = z_ref.shape[0]
    rep = v_heads // qk_heads
    conv_ch = (2 * qk_heads + v_heads) * d
    width = cw_ref.shape[0]
    pad = SUBLANES

    @pl.when(ch == 0)
    def _():
        s_ref[...] = s0_ref[...] if has_init else jnp.zeros_like(s_ref)
        xs_scr[0:pad, :] = jnp.zeros((pad, conv_ch), F32)
        if has_init:
            xs_scr[pad - width + 1:pad, :] = buf0_ref[0]

    ba = zba_ref[...]
    lane = lax.broadcasted_iota(jnp.int32, ba.shape, 1)
    beta = _sigmoid(ba)
    la = aneg_ref[...] * _softplus(ba + dtb_ref[...])
    g = _cumsum_rows(jnp.where(lane >= v_heads, la, 0.0))
    gt = _transpose_rows(g)
    for j in range(v_heads):
        gc_scr[j] = jnp.broadcast_to(g[:, v_heads + j:v_heads + j + 1], (L, LANES))
        bc_scr[j] = jnp.broadcast_to(beta[:, j:j + 1], (L, LANES))
        gr_scr[j] = jnp.broadcast_to(gt[v_heads + j:v_heads + j + 1, :], (SUBLANES, L))

    for blk in range(conv_ch // d):
        sl = slice(blk * d, (blk + 1) * d)
        xs_scr[pad:pad + L, sl] = z_ref[:, sl].astype(F32)
        w = cw_ref[:, sl]
        acc = xs_scr[pad:pad + L, sl] * w[width - 1:width]
        for t in range(1, width):
            acc = acc + xs_scr[pad - t:pad - t + L, sl] * w[width - 1 - t:width - t]
        xs_scr[0:pad, sl] = xs_scr[L:L + pad, sl]
        x = _silu(acc)
        if blk < 2 * qk_heads:
            x = x * lax.rsqrt(jnp.sum(x * x, axis=-1, keepdims=True) + EPS)
            if blk < qk_heads:
                x = x * d ** -0.5
        qkv_scr[:, sl] = x

    causal, strict = _tri_masks(L)
    gv = GDN_V_HEADS_PER_STEP
    gk = gv // rep

    def head_group(grp):
        def cols(base):
            return slice(base * d, (base + 1) * d)
        q = [qkv_scr[:, cols(grp * gk + a)] for a in range(gk)]
        k = [qkv_scr[:, cols(qk_heads + grp * gk + a)] for a in range(gk)]
        qb = [x.astype(BF16) for x in q]
        kb = [x.astype(BF16) for x in k]
        kk = [_dot_nt(x, x) for x in kb]
        qk = [_dot_nt(x, y) for x, y in zip(qb, kb)]
        heads = range(gv)
        js = [grp * gv + r for r in heads]
        v = [qkv_scr[:, cols(2 * qk_heads + j)] for j in js]
        g_col = [gc_scr[j] for j in js]
        g_row = [gr_scr[j][0:1, :] for j in js]
        b_col = [bc_scr[j] for j in js]
        s_old = [s_ref[0, j] for j in js]
        dmat = [jnp.exp(jnp.where(causal, g_col[r][:, :L] - g_row[r], -jnp.inf)) for r in heads]
        tinv = _unit_lower_inverses(
            [jnp.where(strict, b_col[r][:, :L] * dmat[r] * kk[r // rep], 0.0) for r in heads])
        eg = [jnp.exp(x) for x in g_col]
        rhs = [jnp.concatenate([b_col[r] * v[r], b_col[r] * eg[r] * k[r // rep]], axis=-1).astype(BF16)
               for r in heads]
        sol = [_dot(tinv[r].astype(BF16), rhs[r]) for r in heads]
        sb = [x.astype(BF16) for x in s_old]
        ub = [(sol[r][:, :d] - _dot(sol[r][:, d:].astype(BF16), sb[r])).astype(BF16) for r in heads]
        o = [eg[r] * _dot(qb[r // rep], sb[r]) + _dot((qk[r // rep] * dmat[r]).astype(BF16), ub[r])
             for r in heads]
        for r in heads:
            g_last = g_col[r][L - 1:L, :]
            kd = (jnp.exp(g_last - g_col[r]) * k[r // rep]).astype(BF16)
            s_ref[0, js[r]] = jnp.exp(g_last) * s_old[r] + _dot_tn(kd, ub[r])
        for r in heads:
            zg = z_ref[:, cols(conv_ch // d + js[r])].astype(F32)
            y = _rms(o[r], norm_ref[...]) * _silu(zg)
            o_ref[:, cols(js[r])] = y.astype(o_ref.dtype)

    assert v_heads % gv == 0 and gv % rep == 0
    for grp in range(v_heads // gv):
        head_group(grp)


def _gdn(z, zba, conv_w, a_log, dt_bias, norm, init, *, batch, seq_len, qk_heads, v_heads, d):
    L = min(CHUNK, seq_len)
    nc = seq_len // L
    has_init = init is not None
    conv_ch = (2 * qk_heads + v_heads) * d
    width = conv_w.shape[0]
    lane_pad = lambda x: jnp.zeros((1, LANES), F32).at[0, v_heads:2 * v_heads].set(x.astype(F32))
    row = lambda b, c: (b * nc + c, 0)
    const = lambda b, c: (0, 0)
    in_specs = [pl.BlockSpec((L, z.shape[1]), row),
                pl.BlockSpec((L, LANES), row),
                pl.BlockSpec((width, conv_ch), const),
                pl.BlockSpec((1, LANES), const),
                pl.BlockSpec((1, LANES), const),
                pl.BlockSpec((1, d), const)]
    args = [z, zba, conv_w.astype(F32), lane_pad(-jnp.exp(a_log)), lane_pad(dt_bias),
            norm.reshape(1, d).astype(F32)]
    state_spec = pl.BlockSpec((1, v_heads, d, d), lambda b, c: (b, 0, 0, 0))
    if has_init:
        in_specs += [state_spec, pl.BlockSpec((1, width - 1, conv_ch), lambda b, c: (b, 0, 0))]
        args += [init[0], init[1]]
    o, s_new = pl.pallas_call(
        functools.partial(_gdn_kernel, qk_heads=qk_heads, v_heads=v_heads, d=d, has_init=has_init),
        grid=(batch, nc),
        in_specs=in_specs,
        out_specs=[pl.BlockSpec((L, v_heads * d), row), state_spec],
        out_shape=[jax.ShapeDtypeStruct((batch * seq_len, v_heads * d), BF16),
                   jax.ShapeDtypeStruct((batch, v_heads, d, d), F32)],
        scratch_shapes=[pltpu.VMEM((L + SUBLANES, conv_ch), F32),
                        pltpu.VMEM((L, conv_ch), F32),
                        pltpu.VMEM((v_heads, L, LANES), F32),
                        pltpu.VMEM((v_heads, SUBLANES, L), F32),
                        pltpu.VMEM((v_heads, L, LANES), F32)],
        compiler_params=pltpu.CompilerParams(
            dimension_semantics=("arbitrary", "arbitrary"), vmem_limit_bytes=VMEM_LIMIT_BYTES),
        name="gdn",
    )(*args)
    return o, s_new


def _pad_cols(w, n):
    return jnp.zeros((w.shape[0], n), F32).at[:, :w.shape[1]].set(w.astype(F32))


def _trunk(x, states, pos0, p, *, batch, seq_len):
    m, dm = x.shape
    has_init = states is not None
    if has_init:
        mC, mn, mm, rS, gS, dS, dconv, fconv = states
    depth = p['norm_mix'].shape[0]
    outs = [[] for _ in range(8)]
    for i in range(depth):
        kind, j = i % 4, i // 4
        g_mix = p['norm_mix'][i]
        if kind == 0:
            heads, dk, dv = p['mlstm_n_shape']
            w_in = p['mlstm_w_in'][j]
            nm = 2 * heads * dk + 2 * heads * dv
            z, zg = _matmul(x, w_in[:, :nm].astype(BF16), norm_g=g_mix, gate_w=_pad_cols(w_in[:, nm:], LANES),
                            name="mlstm_in")
            bias = _pad_cols(p['mlstm_b_gates'][j][None], LANES)
            init = (mC[j], mn[j], mm[j]) if has_init else None
            a, (c_, n_, m_) = _mlstm(z, zg, bias, p['mlstm_norm'][j], init,
                                     batch=batch, seq_len=seq_len, heads=heads, dk=dk, dv=dv)
            outs[0].append(c_); outs[1].append(n_); outs[2].append(m_)
            w_out = p['mlstm_w_out'][j]
        elif kind == 1:
            heads, dk, dv = p['ret_shape']
            z = _matmul(x, p['ret_w_in'][j].astype(BF16), norm_g=g_mix, name="ret_in")
            a, s_ = _retention(z, p['ret_norm'][j], rS[j] if has_init else None, pos0,
                               batch=batch, seq_len=seq_len, heads=heads, dk=dk, dv=dv)
            outs[3].append(s_)
            w_out = p['ret_w_out'][j]
        elif kind == 2:
            heads, dk, dv = p['gla_shape']
            w_in = p['gla_w_in'][j]
            nm = 2 * heads * dk + 2 * heads * dv
            z, zlr = _matmul(x, w_in[:, :nm].astype(BF16), norm_g=g_mix, gate_w=_pad_cols(w_in[:, nm:], LANES),
                             name="gla_in")
            a, s_ = _gla(z, zlr, p['gla_w_gate2'][j], p['gla_b_gate'][j], p['gla_norm'][j],
                         gS[j] if has_init else None,
                         batch=batch, seq_len=seq_len, heads=heads, dk=dk, dv=dv)
            outs[4].append(s_)
            w_out = p['gla_w_out'][j]
        else:
            qk_heads, v_heads, d = p['gdn_shape']
            w_in = p['gdn_w_in'][j]
            conv_ch = (2 * qk_heads + v_heads) * d
            nm = conv_ch + v_heads * d
            width = p['gdn_conv_w'].shape[1]
            z, zba = _matmul(x, w_in[:, :nm].astype(BF16), norm_g=g_mix, gate_w=_pad_cols(w_in[:, nm:], LANES),
                             name="gdn_in")
            init = (dS[j], dconv[j]) if has_init else None
            a, s_ = _gdn(z, zba, p['gdn_conv_w'][j], p['gdn_A_log'][j], p['gdn_dt_bias'][j], p['gdn_norm'][j],
                         init, batch=batch, seq_len=seq_len, qk_heads=qk_heads, v_heads=v_heads, d=d)
            outs[5].append(s_)
            outs[6].append(z.reshape(batch, seq_len, nm)[:, seq_len - (width - 1):, :conv_ch].astype(F32))
            w_out = p['gdn_w_out'][j]
        x = _matmul(a, w_out.astype(BF16), residual=x, out_dtype=F32, name="mix_out")

        f = p['ffn_w_gate'].shape[2]
        fw = p['ffn_conv_w'].shape[1]
        buf8 = jnp.zeros((batch, SUBLANES, f), F32)
        if has_init:
            buf8 = buf8.at[:, SUBLANES - (fw - 1):].set(fconv[i].astype(F32))
        act, tail = _ffn_in(x, p['norm_ffn'][i], p['ffn_w_gate'][i].astype(BF16), p['ffn_w_up'][i].astype(BF16),
                            buf8, p['ffn_conv_w'][i], p['ffn_conv_b'][i], seq_len=seq_len)
        outs[7].append(tail[:, SUBLANES - (fw - 1):])
        x = _matmul(act, p['ffn_w_down'][i].astype(BF16), residual=x, out_dtype=F32, name="ffn_out")
    y = _final_norm(x, p['norm_final'])
    return y.reshape(batch, seq_len, dm), tuple(jnp.stack(o) for o in outs)


def kernel(x_prompt, x_sample, state_mlstm_C, state_mlstm_n, state_mlstm_m, state_ret_S, state_gla_S, state_gdn_S, state_gdn_conv, state_ffn_conv, norm_mix, norm_ffn, norm_final, mlstm_w_in, mlstm_b_gates, mlstm_norm, mlstm_w_out, ret_w_in, ret_norm, ret_w_out, gla_w_in, gla_w_gate2, gla_b_gate, gla_norm, gla_w_out, gdn_w_in, gdn_conv_w, gdn_A_log, gdn_dt_bias, gdn_norm, gdn_w_out, ffn_w_gate, ffn_w_up, ffn_conv_w, ffn_conv_b, ffn_w_down):
    p = dict(norm_mix=norm_mix, norm_ffn=norm_ffn, norm_final=norm_final,
             mlstm_w_in=mlstm_w_in, mlstm_b_gates=mlstm_b_gates, mlstm_norm=mlstm_norm, mlstm_w_out=mlstm_w_out,
             ret_w_in=ret_w_in, ret_norm=ret_norm, ret_w_out=ret_w_out,
             gla_w_in=gla_w_in, gla_w_gate2=gla_w_gate2, gla_b_gate=gla_b_gate, gla_norm=gla_norm, gla_w_out=gla_w_out,
             gdn_w_in=gdn_w_in, gdn_conv_w=gdn_conv_w, gdn_A_log=gdn_A_log, gdn_dt_bias=gdn_dt_bias,
             gdn_norm=gdn_norm, gdn_w_out=gdn_w_out,
             ffn_w_gate=ffn_w_gate, ffn_w_up=ffn_w_up, ffn_conv_w=ffn_conv_w, ffn_conv_b=ffn_conv_b,
             ffn_w_down=ffn_w_down)
    p['mlstm_n_shape'] = state_mlstm_C.shape[2:]
    p['ret_shape'] = state_ret_S.shape[2:]
    p['gla_shape'] = state_gla_S.shape[2:]
    d = state_gdn_S.shape[-1]
    v_heads = state_gdn_S.shape[2]
    p['gdn_shape'] = ((state_gdn_conv.shape[-1] // d - v_heads) // 2, v_heads, d)

    bp, tp, dm = x_prompt.shape
    bs, ts, _ = x_sample.shape
    y_prompt, p_states = _trunk(x_prompt.reshape(bp * tp, dm), None, 0, p, batch=bp, seq_len=tp)
    cache = (state_mlstm_C, state_mlstm_n, state_mlstm_m, state_ret_S, state_gla_S, state_gdn_S,
             state_gdn_conv, state_ffn_conv)
    y_sample, s_states = _trunk(x_sample.reshape(bs * ts, dm), cache, PAST_LEN, p, batch=bs, seq_len=ts)
    return (y_prompt, y_sample) + p_states + s_states
```

```python
import functools
import math

import jax
import jax.numpy as jnp
from jax import lax
from jax.experimental import pallas as pl
from jax.experimental.pallas import tpu as pltpu

F32 = jnp.float32
BF16 = jnp.bfloat16
EPS = 1e-6
PAST_LEN = 2048
ROPE_BASE = 10000.0
GLA_TAU = 16.0
GLA_RANK = 16
CHUNK = 64
MLSTM_CHUNK = 256
RET_CHUNK = 256
GDN_V_HEADS_PER_STEP = 16
FFN_ROW_CHUNK = 128
LANES = 128
SUBLANES = 8
VMEM_LIMIT_BYTES = 56 * 1024 * 1024


def _sigmoid(x):
    return 1.0 / (1.0 + jnp.exp(-x))


def _silu(x):
    return x * _sigmoid(x)


def _log1p_exp_neg_abs(x):
    return jnp.log(1.0 + jnp.exp(-jnp.abs(x)))


def _log_sigmoid(x):
    return jnp.minimum(x, 0.0) - _log1p_exp_neg_abs(x)


def _softplus(x):
    return jnp.maximum(x, 0.0) + _log1p_exp_neg_abs(x)


def _rms(x, g):
    return x * lax.rsqrt(jnp.mean(x * x, axis=-1, keepdims=True) + EPS) * g


def _dot(a, b):
    return jnp.dot(a, b, preferred_element_type=F32)


def _dot_nt(a, b):
    return lax.dot_general(a, b, (((1,), (1,)), ((), ())), preferred_element_type=F32)


def _dot_tn(a, b):
    return lax.dot_general(a, b, (((0,), (0,)), ((), ())), preferred_element_type=F32)


def _split3(x):
    hi = x.astype(BF16)
    r = x - hi.astype(F32)
    mid = r.astype(BF16)
    lo = (r - mid.astype(F32)).astype(BF16)
    return hi, mid, lo


def _cumsum_rows(x):
    n = x.shape[0]
    r = lax.broadcasted_iota(jnp.int32, (n, n), 0)
    c = lax.broadcasted_iota(jnp.int32, (n, n), 1)
    tril = jnp.where(r >= c, 1.0, 0.0).astype(BF16)
    hi, mid, lo = _split3(x)
    return _dot(tril, hi) + _dot(tril, mid) + _dot(tril, lo)


def _transpose_rows(x):
    n = x.shape[0]
    if n < LANES:
        x = jnp.concatenate([x, jnp.zeros((LANES - n, x.shape[1]), x.dtype)], axis=0)
    return x.T[:, :n]


def _causal_taps(x3, prev, cw):
    width = cw.shape[0]
    t8 = lax.broadcasted_iota(jnp.int32, (x3.shape[0], SUBLANES, x3.shape[2]), 1)
    acc = x3 * cw[width - 1:width]
    for s in range(1, width):
        sh = pltpu.roll(x3, s, 1)
        head = sh[:, :SUBLANES]
        for e in range(s):
            head = jnp.where(t8 == e, prev[s - 1 - e], head)
        sh = jnp.concatenate([head, sh[:, SUBLANES:]], axis=1)
        acc = acc + sh * cw[width - 1 - s:width - s]
    return acc


def _tri_masks(n):
    r = lax.broadcasted_iota(jnp.int32, (n, n), 0)
    c = lax.broadcasted_iota(jnp.int32, (n, n), 1)
    return r >= c, r > c


def _mm_kernel(*refs, has_norm, has_res, has_gate):
    it = iter(refs)
    x_ref = next(it)
    g_ref = next(it) if has_norm else None
    w_ref = next(it)
    wgate_ref = next(it) if has_gate else None
    r_ref = next(it) if has_res else None
    o_ref = next(it)
    ogate_ref = next(it) if has_gate else None
    h_scr = next(it) if has_norm else None
    if has_norm:
        @pl.when(pl.program_id(1) == 0)
        def _():
            h = _rms(x_ref[...].astype(F32), g_ref[...])
            h_scr[...] = h.astype(BF16)
            if has_gate:
                hh, hm, _ = _split3(h)
                wg = wgate_ref[...]
                wh = wg.astype(BF16)
                wl = (wg - wh.astype(F32)).astype(BF16)
                ogate_ref[...] = _dot(hh, wh) + _dot(hm, wh) + _dot(hh, wl)
        h = h_scr[...]
    else:
        h = x_ref[...]
    acc = _dot(h, w_ref[...])
    if has_res:
        acc = acc + r_ref[...]
    o_ref[...] = acc.astype(o_ref.dtype)


def _pick(n, cands):
    for c in cands:
        if n % c == 0:
            return c
    raise ValueError(f"no tile for {n}")


def _matmul(x, w, *, norm_g=None, gate_w=None, residual=None, out_dtype=BF16, bm=1024, bn=None, name="mm"):
    m, k = x.shape
    n = w.shape[1]
    bm = min(bm, m)
    if bn is None:
        bn = _pick(n, (1024, 512, 256, 128)) if k <= 2048 else _pick(n, (512, 256, 128))
    assert m % bm == 0 and n % bn == 0
    has_norm, has_res, has_gate = norm_g is not None, residual is not None, gate_w is not None
    assert has_norm or not has_gate
    in_specs = [pl.BlockSpec((bm, k), lambda i, j: (i, 0))]
    args = [x]
    if has_norm:
        in_specs.append(pl.BlockSpec((1, k), lambda i, j: (0, 0)))
        args.append(norm_g.reshape(1, k).astype(F32))
    in_specs.append(pl.BlockSpec((k, bn), lambda i, j: (0, j)))
    args.append(w)
    if has_gate:
        in_specs.append(pl.BlockSpec((k, LANES), lambda i, j: (0, 0)))
        args.append(gate_w)
    if has_res:
        in_specs.append(pl.BlockSpec((bm, bn), lambda i, j: (i, j)))
        args.append(residual)
    out_specs = [pl.BlockSpec((bm, bn), lambda i, j: (i, j))]
    out_shape = [jax.ShapeDtypeStruct((m, n), out_dtype)]
    if has_gate:
        out_specs.append(pl.BlockSpec((bm, LANES), lambda i, j: (i, 0)))
        out_shape.append(jax.ShapeDtypeStruct((m, LANES), F32))
    outs = pl.pallas_call(
        functools.partial(_mm_kernel, has_norm=has_norm, has_res=has_res, has_gate=has_gate),
        grid=(m // bm, n // bn),
        in_specs=in_specs,
        out_specs=out_specs,
        out_shape=out_shape,
        scratch_shapes=[pltpu.VMEM((bm, k), BF16)] if has_norm else [],
        compiler_params=pltpu.CompilerParams(
            dimension_semantics=("arbitrary", "arbitrary"), vmem_limit_bytes=VMEM_LIMIT_BYTES),
        name=name,
    )(*args)
    return outs if has_gate else outs[0]


def _final_norm_kernel(x_ref, g_ref, o_ref):
    o_ref[...] = _rms(x_ref[...], g_ref[...])


def _final_norm(x, g, *, bm=512):
    m, k = x.shape
    bm = min(bm, m)
    return pl.pallas_call(
        _final_norm_kernel,
        grid=(m // bm,),
        in_specs=[pl.BlockSpec((bm, k), lambda i: (i, 0)), pl.BlockSpec((1, k), lambda i: (0, 0))],
        out_specs=pl.BlockSpec((bm, k), lambda i: (i, 0)),
        out_shape=jax.ShapeDtypeStruct((m, k), F32),
        compiler_params=pltpu.CompilerParams(
            dimension_semantics=("arbitrary",), vmem_limit_bytes=VMEM_LIMIT_BYTES),
        name="final_norm",
    )(x, g.reshape(1, k).astype(F32))


def _ffn_in_kernel(x_ref, g_ref, wg_ref, wu_ref, buf_ref, cw_ref, cb_ref, act_ref, tail_ref, h_scr, halo_scr,
                   *, tiles_per_seq, seq_rows):
    i, j = pl.program_id(0), pl.program_id(1)
    bm, bn = act_ref.shape

    @pl.when(j == 0)
    def _():
        h_scr[...] = _rms(x_ref[...], g_ref[...]).astype(BF16)

    nseq = bm // seq_rows
    first = (i % tiles_per_seq) == 0
    halo = jnp.where(first, buf_ref[...], halo_scr[j])
    prev2, prev1 = halo[:, 6:7], halo[:, 7:8]
    cw, cb = cw_ref[...], cb_ref[...]
    wg, wu = wg_ref[...], wu_ref[...]
    rc = min(FFN_ROW_CHUNK, bm)
    spc = max(rc // seq_rows, 1)
    tt = rc // spc
    tails = []
    for r in range(bm // rc):
        rows = slice(r * rc, (r + 1) * rc)
        h = h_scr[rows, :]
        g3 = _dot(h, wg).reshape(spc, tt, bn)
        u3 = _dot(h, wu).reshape(spc, tt, bn)
        pv = [prev1, prev2] if nseq == 1 else [prev1[r * spc:(r + 1) * spc], prev2[r * spc:(r + 1) * spc]]
        gc = _causal_taps(g3, pv, cw) + cb
        act_ref[rows, :] = (_silu(gc) * u3).reshape(rc, bn).astype(act_ref.dtype)
        if nseq == 1:
            prev2, prev1 = g3[:, tt - 2:tt - 1], g3[:, tt - 1:tt]
            tails = [g3[:, tt - SUBLANES:]]
        else:
            tails.append(g3[:, tt - SUBLANES:])
    tail = tails[0] if len(tails) == 1 else jnp.concatenate(tails, axis=0)
    halo_scr[j] = tail
    tail_ref[...] = tail


def _ffn_in(x, norm_g, w_gate, w_up, buf8, conv_w, conv_b, *, seq_len, bm=1024, bn=512):
    m, k = x.shape
    f = w_gate.shape[1]
    batch = m // seq_len
    bm = min(bm, m)
    assert f % bn == 0 and m % bm == 0 and (seq_len % bm == 0 or bm % seq_len == 0)
    nf = f // bn
    tiles_per_seq = max(seq_len // bm, 1)
    seq_rows = min(seq_len, bm)
    nseq = bm // seq_rows
    seq_blk = lambda i, j: (i // tiles_per_seq, 0, j)
    act, tails = pl.pallas_call(
        functools.partial(_ffn_in_kernel, tiles_per_seq=tiles_per_seq, seq_rows=seq_rows),
        grid=(m // bm, nf),
        in_specs=[
            pl.BlockSpec((bm, k), lambda i, j: (i, 0)),
            pl.BlockSpec((1, k), lambda i, j: (0, 0)),
            pl.BlockSpec((k, bn), lambda i, j: (0, j)),
            pl.BlockSpec((k, bn), lambda i, j: (0, j)),
            pl.BlockSpec((nseq, SUBLANES, bn), seq_blk),
            pl.BlockSpec((3, bn), lambda i, j: (0, j)),
            pl.BlockSpec((1, bn), lambda i, j: (0, j)),
        ],
        out_specs=[pl.BlockSpec((bm, bn), lambda i, j: (i, j)),
                   pl.BlockSpec((nseq, SUBLANES, bn), lambda i, j: (i, 0, j))],
        out_shape=[jax.ShapeDtypeStruct((m, f), BF16),
                   jax.ShapeDtypeStruct((m // bm * nseq, SUBLANES, f), F32)],
        scratch_shapes=[pltpu.VMEM((bm, k), BF16), pltpu.VMEM((nf, nseq, SUBLANES, bn), F32)],
        compiler_params=pltpu.CompilerParams(
            dimension_semantics=("arbitrary", "arbitrary"), vmem_limit_bytes=VMEM_LIMIT_BYTES),
        name="ffn_in",
    )(x, norm_g.reshape(1, k).astype(F32), w_gate, w_up, buf8, conv_w.astype(F32),
      conv_b.reshape(1, f).astype(F32))
    return act, tails.reshape(batch, tiles_per_seq, SUBLANES, f)[:, tiles_per_seq - 1]


def _mlstm_kernel(*refs, heads, dk, dv, has_init):
    it = iter(refs)
    z_ref, zg_ref, bias_ref, norm_ref = next(it), next(it), next(it), next(it)
    if has_init:
        c0_ref, n0_ref, m0_ref = next(it), next(it), next(it)
    o_ref, c_ref, n_ref, m_ref = next(it), next(it), next(it), next(it)
    ch = pl.program_id(1)
    L = z_ref.shape[0]

    @pl.when(ch == 0)
    def _():
        if has_init:
            c_ref[...] = c0_ref[...]
            n_ref[...] = n0_ref[...]
            m_ref[...] = m0_ref[...]
        else:
            c_ref[...] = jnp.zeros_like(c_ref)
            n_ref[...] = jnp.zeros_like(n_ref)
            m_ref[...] = jnp.zeros_like(m_ref)

    g = zg_ref[...] + bias_ref[...]
    lane = lax.broadcasted_iota(jnp.int32, g.shape, 1)
    u = jnp.where(lane < heads, g, _cumsum_rows(_log_sigmoid(g)))
    ut = _transpose_rows(u)
    tri, _ = _tri_masks(L)
    m_all = m_ref[0]
    m_lane = lax.broadcasted_iota(jnp.int32, m_all.shape, 1)
    qoff, koff, voff, ooff = 0, heads * dk, 2 * heads * dk, 2 * heads * dk + heads * dv
    hs = range(heads)
    ig_col = [u[:, h:h + 1] for h in hs]
    ig_row = [ut[h:h + 1, :] for h in hs]
    b_col = [u[:, heads + h:heads + h + 1] for h in hs]
    b_row = [ut[heads + h:heads + h + 1, :] for h in hs]
    m_prev = [m_all[:, h:h + 1] for h in hs]
    q = [(z_ref[:, qoff + h * dk:qoff + (h + 1) * dk].astype(F32) * dk ** -0.5).astype(BF16) for h in hs]
    k = [z_ref[:, koff + h * dk:koff + (h + 1) * dk] for h in hs]
    v = [z_ref[:, voff + h * dv:voff + (h + 1) * dv] for h in hs]
    c_old = [c_ref[0, h] for h in hs]
    n_old = [n_ref[0, h:h + 1, :] for h in hs]
    d = [jnp.where(tri, b_col[h] - b_row[h] + ig_row[h], -jnp.inf) for h in hs]
    li = [b_col[h] + m_prev[h] for h in hs]
    m_t = [jnp.maximum(li[h], jnp.max(d[h], axis=1, keepdims=True)) for h in hs]
    w_inter = [jnp.exp(li[h] - m_t[h]) for h in hs]
    a = [_dot_nt(q[h], k[h]) * jnp.exp(d[h] - m_t[h]) for h in hs]
    num = [_dot(a[h].astype(BF16), v[h]) + w_inter[h] * _dot(q[h], c_old[h].astype(BF16)) for h in hs]
    den = [jnp.sum(a[h], axis=1, keepdims=True)
           + w_inter[h] * jnp.sum(q[h].astype(F32) * n_old[h], axis=1, keepdims=True) for h in hs]
    hh = [num[h] / jnp.maximum(jnp.abs(den[h]), jnp.exp(-m_t[h])) for h in hs]
    for h in hs:
        m_new = m_t[h][L - 1:L, :]
        b_last = b_col[h][L - 1:L, :]
        w_k = jnp.exp(b_last - b_col[h] + ig_col[h] - m_new)
        dec = jnp.exp(b_last + m_prev[h] - m_new)
        kw = k[h].astype(F32) * w_k
        c_ref[0, h] = dec * c_old[h] + _dot_tn(kw.astype(BF16), v[h])
        n_ref[0, h:h + 1, :] = dec * n_old[h] + jnp.sum(kw, axis=0, keepdims=True)
        m_all = jnp.where(m_lane == h, m_new, m_all)
    for h in hs:
        og = z_ref[:, ooff + h * dv:ooff + (h + 1) * dv].astype(F32)
        y = _sigmoid(og) * _rms(hh[h], norm_ref[...])
        o_ref[:, h * dv:(h + 1) * dv] = y.astype(o_ref.dtype)
    m_ref[0] = m_all


def _mlstm(z, zg, bias, norm, init, *, batch, seq_len, heads, dk, dv):
    L = min(MLSTM_CHUNK, seq_len)
    nc = seq_len // L
    has_init = init is not None
    row = lambda b, c: (b * nc + c, 0)
    in_specs = [pl.BlockSpec((L, z.shape[1]), row),
                pl.BlockSpec((L, LANES), row),
                pl.BlockSpec((1, LANES), lambda b, c: (0, 0)),
                pl.BlockSpec((1, dv), lambda b, c: (0, 0))]
    args = [z, zg, bias, norm.reshape(1, dv).astype(F32)]
    state_specs = [pl.BlockSpec((1, heads, dk, dv), lambda b, c: (b, 0, 0, 0)),
                   pl.BlockSpec((1, heads, dk), lambda b, c: (b, 0, 0)),
                   pl.BlockSpec((1, 1, heads), lambda b, c: (b, 0, 0))]
    if has_init:
        in_specs += state_specs
        args += [init[0], init[1], init[2].reshape(batch, 1, heads)]
    o, c_new, n_new, m_new = pl.pallas_call(
        functools.partial(_mlstm_kernel, heads=heads, dk=dk, dv=dv, has_init=has_init),
        grid=(batch, nc),
        in_specs=in_specs,
        out_specs=[pl.BlockSpec((L, heads * dv), row)] + state_specs,
        out_shape=[jax.ShapeDtypeStruct((batch * seq_len, heads * dv), BF16),
                   jax.ShapeDtypeStruct((batch, heads, dk, dv), F32),
                   jax.ShapeDtypeStruct((batch, heads, dk), F32),
                   jax.ShapeDtypeStruct((batch, 1, heads), F32)],
        compiler_params=pltpu.CompilerParams(
            dimension_semantics=("arbitrary", "arbitrary"), vmem_limit_bytes=VMEM_LIMIT_BYTES),
        name="mlstm",
    )(*args)
    return o, (c_new, n_new, m_new.reshape(batch, heads))


def _ret_kernel(*refs, heads, dk, dv, has_init):
    it = iter(refs)
    cos_ref, sin_ref, z_ref, norm_ref = next(it), next(it), next(it), next(it)
    s0_ref = next(it) if has_init else None
    o_ref, s_ref = next(it), next(it)
    ch = pl.program_id(1)
    L = z_ref.shape[0]
    half = dk // 2

    @pl.when(ch == 0)
    def _():
        s_ref[...] = s0_ref[...] if has_init else jnp.zeros_like(s_ref)

    cos, sin = cos_ref[...], sin_ref[...]
    t_col = lax.broadcasted_iota(jnp.int32, (L, 1), 0).astype(F32)
    rel = (lax.broadcasted_iota(jnp.int32, (L, L), 0) - lax.broadcasted_iota(jnp.int32, (L, L), 1)).astype(F32)
    koff, voff, goff = heads * dk, 2 * heads * dk, 2 * heads * dk + heads * dv

    def rope(x):
        x1, x2 = x[:, :half], x[:, half:]
        return jnp.concatenate([x1 * cos - x2 * sin, x1 * sin + x2 * cos], axis=-1)

    for h in range(heads):
        lg = math.log1p(-2.0 ** (-5.0 - h))
        q = rope(z_ref[:, h * dk:(h + 1) * dk].astype(F32)) * dk ** -0.5
        k = rope(z_ref[:, koff + h * dk:koff + (h + 1) * dk].astype(F32))
        v = z_ref[:, voff + h * dv:voff + (h + 1) * dv]
        gate = z_ref[:, goff + h * dv:goff + (h + 1) * dv].astype(F32)
        qb = q.astype(BF16)
        dec = jnp.where(rel >= 0, jnp.exp(jnp.maximum(rel, 0.0) * lg), 0.0)
        a = _dot_nt(qb, k.astype(BF16)) * dec
        s_old = s_ref[0, h]
        o = _dot(a.astype(BF16), v) + jnp.exp((t_col + 1.0) * lg) * _dot(qb, s_old.astype(BF16))
        kd = k * jnp.exp((L - 1.0 - t_col) * lg)
        s_ref[0, h] = math.exp(L * lg) * s_old + _dot_tn(kd.astype(BF16), v)
        y = _silu(gate) * _rms(o, norm_ref[...])
        o_ref[:, h * dv:(h + 1) * dv] = y.astype(o_ref.dtype)


def _retention(z, norm, init, pos0, *, batch, seq_len, heads, dk, dv):
    L = min(RET_CHUNK, seq_len)
    nc = seq_len // L
    has_init = init is not None
    half = dk // 2
    inv = ROPE_BASE ** (-jnp.arange(half, dtype=F32) / half)
    ang = (pos0 + jnp.arange(seq_len)).astype(F32)[:, None] * inv[None]
    row = lambda b, c: (b * nc + c, 0)
    in_specs = [pl.BlockSpec((L, half), lambda b, c: (c, 0)),
                pl.BlockSpec((L, half), lambda b, c: (c, 0)),
                pl.BlockSpec((L, z.shape[1]), row),
                pl.BlockSpec((1, dv), lambda b, c: (0, 0))]
    args = [jnp.cos(ang), jnp.sin(ang), z, norm.reshape(1, dv).astype(F32)]
    state_spec = pl.BlockSpec((1, heads, dk, dv), lambda b, c: (b, 0, 0, 0))
    if has_init:
        in_specs.append(state_spec)
        args.append(init)
    o, s_new = pl.pallas_call(
        functools.partial(_ret_kernel, heads=heads, dk=dk, dv=dv, has_init=has_init),
        grid=(batch, nc),
        in_specs=in_specs,
        out_specs=[pl.BlockSpec((L, heads * dv), row), state_spec],
        out_shape=[jax.ShapeDtypeStruct((batch * seq_len, heads * dv), BF16),
                   jax.ShapeDtypeStruct((batch, heads, dk, dv), F32)],
        compiler_params=pltpu.CompilerParams(
            dimension_semantics=("arbitrary", "arbitrary"), vmem_limit_bytes=VMEM_LIMIT_BYTES),
        name="retention",
    )(*args)
    return o, s_new


def _gla_kernel(*refs, heads, dk, dv, has_init):
    it = iter(refs)
    z_ref, zlr_ref, w2_ref, bg_ref, norm_ref = next(it), next(it), next(it), next(it), next(it)
    s0_ref = next(it) if has_init else None
    o_ref, s_ref, st_scr = next(it), next(it), next(it)
    ch = pl.program_id(1)
    nch = pl.num_programs(1)
    L = z_ref.shape[0]

    @pl.when(ch == 0)
    def _():
        for h in range(heads):
            st_scr[h] = s0_ref[0, h].T if has_init else jnp.zeros((dv, dk), F32)

    la = _log_sigmoid(_dot(zlr_ref[...].astype(BF16), w2_ref[...]) + bg_ref[...]) / GLA_TAU
    b = _cumsum_rows(la)
    tri, _ = _tri_masks(L)
    koff, voff, roff = heads * dk, 2 * heads * dk, 2 * heads * dk + heads * dv
    for h in range(heads):
        bh = b[:, h * dk:(h + 1) * dk]
        bl = bh[L - 1:L, :]
        q = z_ref[:, h * dk:(h + 1) * dk].astype(F32) * dk ** -0.5
        k = z_ref[:, koff + h * dk:koff + (h + 1) * dk].astype(F32)
        v = z_ref[:, voff + h * dv:voff + (h + 1) * dv]
        r = z_ref[:, roff + h * dv:roff + (h + 1) * dv].astype(F32)
        qd = (q * jnp.exp(bh)).astype(BF16)
        kd = (k * jnp.exp(-bh)).astype(BF16)
        a = jnp.where(tri, _dot_nt(qd, kd), 0.0)
        st_old = st_scr[h]
        o = _dot(a.astype(BF16), v) + _dot_nt(qd, st_old.astype(BF16))
        ke = (k * jnp.exp(bl - bh)).astype(BF16)
        st_scr[h] = jnp.exp(bl) * st_old + _dot_tn(v, ke)
        y = _silu(r) * _rms(o, norm_ref[...])
        o_ref[:, h * dv:(h + 1) * dv] = y.astype(o_ref.dtype)

    @pl.when(ch == nch - 1)
    def _():
        for h in range(heads):
            s_ref[0, h] = st_scr[h].T


def _gla(z, zlr, w_gate2, b_gate, norm, init, *, batch, seq_len, heads, dk, dv):
    L = min(CHUNK, seq_len)
    nc = seq_len // L
    has_init = init is not None
    hk = heads * dk
    w2 = jnp.zeros((LANES, hk), BF16).at[:GLA_RANK].set(w_gate2.astype(BF16))
    row = lambda b, c: (b * nc + c, 0)
    in_specs = [pl.BlockSpec((L, z.shape[1]), row),
                pl.BlockSpec((L, LANES), row),
                pl.BlockSpec((LANES, hk), lambda b, c: (0, 0)),
                pl.BlockSpec((1, hk), lambda b, c: (0, 0)),
                pl.BlockSpec((1, dv), lambda b, c: (0, 0))]
    args = [z, zlr, w2, b_gate.reshape(1, hk).astype(F32), norm.reshape(1, dv).astype(F32)]
    state_spec = pl.BlockSpec((1, heads, dk, dv), lambda b, c: (b, 0, 0, 0))
    if has_init:
        in_specs.append(state_spec)
        args.append(init)
    o, s_new = pl.pallas_call(
        functools.partial(_gla_kernel, heads=heads, dk=dk, dv=dv, has_init=has_init),
        grid=(batch, nc),
        in_specs=in_specs,
        out_specs=[pl.BlockSpec((L, heads * dv), row), state_spec],
        out_shape=[jax.ShapeDtypeStruct((batch * seq_len, heads * dv), BF16),
                   jax.ShapeDtypeStruct((batch, heads, dk, dv), F32)],
        scratch_shapes=[pltpu.VMEM((heads, dv, dk), F32)],
        compiler_params=pltpu.CompilerParams(
            dimension_semantics=("arbitrary", "arbitrary"), vmem_limit_bytes=VMEM_LIMIT_BYTES),
        name="gla",
    )(*args)
    return o, s_new


def _unit_lower_inverses(mats):
    n = mats[0].shape[0]
    r = lax.broadcasted_iota(jnp.int32, (n, n), 0)
    c = lax.broadcasted_iota(jnp.int32, (n, n), 1)
    eye = jnp.where(r == c, 1.0, 0.0)
    ps = [eye - a for a in mats]
    pws = [_dot(a.astype(BF16), a.astype(BF16)) for a in mats]
    covered = 2
    while covered < n:
        if 2 * covered < n:
            both = [_dot(jnp.concatenate([p, pw], axis=0).astype(BF16), pw.astype(BF16)) for p, pw in zip(ps, pws)]
            ps = [p + b[:n] for p, b in zip(ps, both)]
            pws = [b[n:] for b in both]
        else:
            ps = [p + _dot(p.astype(BF16), pw.astype(BF16)) for p, pw in zip(ps, pws)]
        covered *= 2
    return ps


def _gdn_kernel(*refs, qk_heads, v_heads, d, has_init):
    it = iter(refs)
    z_ref, zba_ref, aneg_ref, dtb_ref, norm_ref = (next(it) for _ in range(5))
    s0_ref = next(it) if has_init else None
    o_ref, s_ref = next(it), next(it)
    gc_scr, gr_scr, bc_scr = (next(it) for _ in range(3))
    ch = pl.program_id(1)
    L = z_ref.shape[0]
    rep = v_heads // qk_heads
    conv_ch = (2 * qk_heads + v_heads) * d

    @pl.when(ch == 0)
    def _():
        s_ref[...] = s0_ref[...] if has_init else jnp.zeros_like(s_ref)

    ba = zba_ref[...]
    lane = lax.broadcasted_iota(jnp.int32, ba.shape, 1)
    beta = _sigmoid(ba)
    la = aneg_ref[...] * _softplus(ba + dtb_ref[...])
    g = _cumsum_rows(jnp.where(lane >= v_heads, la, 0.0))
    gt = _transpose_rows(g)
    for j in range(v_heads):
        gc_scr[j] = jnp.broadcast_to(g[:, v_heads + j:v_heads + j + 1], (L, LANES))
        bc_scr[j] = jnp.broadcast_to(beta[:, j:j + 1], (L, LANES))
        gr_scr[j] = jnp.broadcast_to(gt[v_heads + j:v_heads + j + 1, :], (SUBLANES, L))

    causal, strict = _tri_masks(L)
    gv = GDN_V_HEADS_PER_STEP
    gk = gv // rep

    def head_group(grp):
        def cols(base):
            return slice(base * d, (base + 1) * d)
        qb = [z_ref[:, cols(grp * gk + a)] for a in range(gk)]
        kb = [z_ref[:, cols(qk_heads + grp * gk + a)] for a in range(gk)]
        k = [x.astype(F32) for x in kb]
        kk = [_dot_nt(x, x) for x in kb]
        qk = [_dot_nt(x, y) for x, y in zip(qb, kb)]
        heads = range(gv)
        js = [grp * gv + r for r in heads]
        v = [z_ref[:, cols(2 * qk_heads + j)].astype(F32) for j in js]
        g_col = [gc_scr[j] for j in js]
        g_row = [gr_scr[j][0:1, :] for j in js]
        b_col = [bc_scr[j] for j in js]
        s_old = [s_ref[0, j] for j in js]
        dmat = [jnp.exp(jnp.where(causal, g_col[r][:, :L] - g_row[r], -jnp.inf)) for r in heads]
        tinv = _unit_lower_inverses(
            [jnp.where(strict, b_col[r][:, :L] * dmat[r] * kk[r // rep], 0.0) for r in heads])
        eg = [jnp.exp(x) for x in g_col]
        rhs = [jnp.concatenate([b_col[r] * v[r], b_col[r] * eg[r] * k[r // rep]], axis=-1).astype(BF16)
               for r in heads]
        sol = [_dot(tinv[r].astype(BF16), rhs[r]) for r in heads]
        sb = [x.astype(BF16) for x in s_old]
        ub = [(sol[r][:, :d] - _dot(sol[r][:, d:].astype(BF16), sb[r])).astype(BF16) for r in heads]
        o = [eg[r] * _dot(qb[r // rep], sb[r]) + _dot((qk[r // rep] * dmat[r]).astype(BF16), ub[r])
             for r in heads]
        for r in heads:
            g_last = g_col[r][L - 1:L, :]
            kd = (jnp.exp(g_last - g_col[r]) * k[r // rep]).astype(BF16)
            s_ref[0, js[r]] = jnp.exp(g_last) * s_old[r] + _dot_tn(kd, ub[r])
        for r in heads:
            zg = z_ref[:, cols(conv_ch // d + js[r])].astype(F32)
            y = _rms(o[r], norm_ref[...]) * _silu(zg)
            o_ref[:, cols(js[r])] = y.astype(o_ref.dtype)

    assert v_heads % gv == 0 and gv % rep == 0
    for grp in range(v_heads // gv):
        head_group(grp)


def _gdn_in_kernel(x_ref, g_ref, w_ref, wgate_ref, buf_ref, cw_ref, z_ref, zgate_ref, tail_ref, h_scr, halo_scr,
                   *, tiles_per_seq, seq_rows, qk_tiles, conv_tiles, d):
    i, j = pl.program_id(0), pl.program_id(1)
    bm, bn = z_ref.shape
    width = cw_ref.shape[0]

    @pl.when(j == 0)
    def _():
        h = _rms(x_ref[...], g_ref[...])
        h_scr[...] = h.astype(BF16)
        hh, hm, _ = _split3(h)
        wg = wgate_ref[...]
        wh = wg.astype(BF16)
        wl = (wg - wh.astype(F32)).astype(BF16)
        zgate_ref[...] = _dot(hh, wh) + _dot(hm, wh) + _dot(hh, wl)

    nseq = bm // seq_rows
    first = (i % tiles_per_seq) == 0
    rc = min(FFN_ROW_CHUNK, bm)
    spc = max(rc // seq_rows, 1)
    tt = rc // spc

    def tile(kind):
        halo = jnp.where(first, buf_ref[...], halo_scr[j])
        prev = [halo[:, SUBLANES - s:SUBLANES - s + 1] for s in range(1, width)]
        cw = cw_ref[...]
        w = w_ref[...]
        q_scale = jnp.where(j < qk_tiles // 2, d ** -0.5, 1.0)
        tails = []
        nr = bm // rc
        g_next = _dot(h_scr[0:rc, :], w)
        for r in range(nr):
            rows = slice(r * rc, (r + 1) * rc)
            g3 = g_next.reshape(spc, tt, bn)
            if r + 1 < nr:
                g_next = _dot(h_scr[(r + 1) * rc:(r + 2) * rc, :], w)
            if kind == "plain":
                out = g3.reshape(rc, bn)
            else:
                pv = prev if nseq == 1 else [p[r * spc:(r + 1) * spc] for p in prev]
                out = _silu(_causal_taps(g3, pv, cw)).reshape(rc, bn)
                if kind == "qk":
                    out = jnp.concatenate(
                        [out[:, c * d:(c + 1) * d]
                         * lax.rsqrt(jnp.sum(out[:, c * d:(c + 1) * d] ** 2, axis=-1, keepdims=True) + EPS)
                         for c in range(bn // d)], axis=-1) * q_scale
            z_ref[rows, :] = out.astype(z_ref.dtype)
            if nseq == 1:
                prev = [g3[:, tt - s:tt - s + 1] for s in range(1, width)]
                tails = [g3[:, tt - SUBLANES:]]
            else:
                tails.append(g3[:, tt - SUBLANES:])
        tail = tails[0] if len(tails) == 1 else jnp.concatenate(tails, axis=0)
        halo_scr[j] = tail
        tail_ref[...] = tail

    pl.when(j < qk_tiles)(functools.partial(tile, "qk"))
    pl.when(jnp.logical_and(j >= qk_tiles, j < conv_tiles))(functools.partial(tile, "v"))
    pl.when(j >= conv_tiles)(functools.partial(tile, "plain"))


def _gdn_in(x, norm_g, w, gate_w, buf8, conv_w, *, seq_len, qk_heads, v_heads, d, bm=1024, bn=512):
    m, k = x.shape
    n = w.shape[1]
    batch = m // seq_len
    bm = min(bm, m)
    conv_ch = (2 * qk_heads + v_heads) * d
    assert n % bn == 0 and m % bm == 0 and (seq_len % bm == 0 or bm % seq_len == 0)
    assert (2 * qk_heads * d) % bn == 0 and conv_ch % bn == 0 and bn % d == 0 and (qk_heads * d) % bn == 0
    nn = n // bn
    tiles_per_seq = max(seq_len // bm, 1)
    seq_rows = min(seq_len, bm)
    nseq = bm // seq_rows
    width = conv_w.shape[0]
    cw = jnp.zeros((width, n), F32).at[:, :conv_ch].set(conv_w.astype(F32))
    z, zgate, tails = pl.pallas_call(
        functools.partial(_gdn_in_kernel, tiles_per_seq=tiles_per_seq, seq_rows=seq_rows,
                          qk_tiles=2 * qk_heads * d // bn, conv_tiles=conv_ch // bn, d=d),
        grid=(m // bm, nn),
        in_specs=[
            pl.BlockSpec((bm, k), lambda i, j: (i, 0)),
            pl.BlockSpec((1, k), lambda i, j: (0, 0)),
            pl.BlockSpec((k, bn), lambda i, j: (0, j)),
            pl.BlockSpec((k, LANES), lambda i, j: (0, 0)),
            pl.BlockSpec((nseq, SUBLANES, bn), lambda i, j: (i // tiles_per_seq, 0, j)),
            pl.BlockSpec((width, bn), lambda i, j: (0, j)),
        ],
        out_specs=[pl.BlockSpec((bm, bn), lambda i, j: (i, j)),
                   pl.BlockSpec((bm, LANES), lambda i, j: (i, 0)),
                   pl.BlockSpec((nseq, SUBLANES, bn), lambda i, j: (i, 0, j))],
        out_shape=[jax.ShapeDtypeStruct((m, n), BF16),
                   jax.ShapeDtypeStruct((m, LANES), F32),
                   jax.ShapeDtypeStruct((m // bm * nseq, SUBLANES, n), F32)],
        scratch_shapes=[pltpu.VMEM((bm, k), BF16), pltpu.VMEM((nn, nseq, SUBLANES, bn), F32)],
        compiler_params=pltpu.CompilerParams(
            dimension_semantics=("arbitrary", "arbitrary"), vmem_limit_bytes=VMEM_LIMIT_BYTES),
        name="gdn_in",
    )(x, norm_g.reshape(1, k).astype(F32), w, gate_w, buf8, cw)
    return z, zgate, tails.reshape(batch, tiles_per_seq, SUBLANES, n)[:, tiles_per_seq - 1]


def _gdn(z, zba, a_log, dt_bias, norm, init, *, batch, seq_len, qk_heads, v_heads, d):
    L = min(CHUNK, seq_len)
    nc = seq_len // L
    has_init = init is not None
    lane_pad = lambda x: jnp.zeros((1, LANES), F32).at[0, v_heads:2 * v_heads].set(x.astype(F32))
    row = lambda b, c: (b * nc + c, 0)
    const = lambda b, c: (0, 0)
    in_specs = [pl.BlockSpec((L, z.shape[1]), row),
                pl.BlockSpec((L, LANES), row),
                pl.BlockSpec((1, LANES), const),
                pl.BlockSpec((1, LANES), const),
                pl.BlockSpec((1, d), const)]
    args = [z, zba, lane_pad(-jnp.exp(a_log)), lane_pad(dt_bias), norm.reshape(1, d).astype(F32)]
    state_spec = pl.BlockSpec((1, v_heads, d, d), lambda b, c: (b, 0, 0, 0))
    if has_init:
        in_specs.append(state_spec)
        args.append(init)
    o, s_new = pl.pallas_call(
        functools.partial(_gdn_kernel, qk_heads=qk_heads, v_heads=v_heads, d=d, has_init=has_init),
        grid=(batch, nc),
        in_specs=in_specs,
        out_specs=[pl.BlockSpec((L, v_heads * d), row), state_spec],
        out_shape=[jax.ShapeDtypeStruct((batch * seq_len, v_heads * d), BF16),
                   jax.ShapeDtypeStruct((batch, v_heads, d, d), F32)],
        scratch_shapes=[pltpu.VMEM((v_heads, L, LANES), F32),
                        pltpu.VMEM((v_heads, SUBLANES, L), F32),
                        pltpu.VMEM((v_heads, L, LANES), F32)],
        compiler_params=pltpu.CompilerParams(
            dimension_semantics=("arbitrary", "arbitrary"), vmem_limit_bytes=VMEM_LIMIT_BYTES),
        name="gdn",
    )(*args)
    return o, s_new


def _pad_cols(w, n):
    return jnp.zeros((w.shape[0], n), F32).at[:, :w.shape[1]].set(w.astype(F32))


def _trunk(x, states, pos0, p, *, batch, seq_len):
    m, dm = x.shape
    has_init = states is not None
    if has_init:
        mC, mn, mm, rS, gS, dS, dconv, fconv = states
    depth = p['norm_mix'].shape[0]
    outs = [[] for _ in range(8)]
    for i in range(depth):
        kind, j = i % 4, i // 4
        g_mix = p['norm_mix'][i]
        if kind == 0:
            heads, dk, dv = p['mlstm_n_shape']
            w_in = p['mlstm_w_in'][j]
            nm = 2 * heads * dk + 2 * heads * dv
            z, zg = _matmul(x, w_in[:, :nm].astype(BF16), norm_g=g_mix, gate_w=_pad_cols(w_in[:, nm:], LANES),
                            name="mlstm_in")
            bias = _pad_cols(p['mlstm_b_gates'][j][None], LANES)
            init = (mC[j], mn[j], mm[j]) if has_init else None
            a, (c_, n_, m_) = _mlstm(z, zg, bias, p['mlstm_norm'][j], init,
                                     batch=batch, seq_len=seq_len, heads=heads, dk=dk, dv=dv)
            outs[0].append(c_); outs[1].append(n_); outs[2].append(m_)
            w_out = p['mlstm_w_out'][j]
        elif kind == 1:
            heads, dk, dv = p['ret_shape']
            z = _matmul(x, p['ret_w_in'][j].astype(BF16), norm_g=g_mix, name="ret_in")
            a, s_ = _retention(z, p['ret_norm'][j], rS[j] if has_init else None, pos0,
                               batch=batch, seq_len=seq_len, heads=heads, dk=dk, dv=dv)
            outs[3].append(s_)
            w_out = p['ret_w_out'][j]
        elif kind == 2:
            heads, dk, dv = p['gla_shape']
            w_in = p['gla_w_in'][j]
            nm = 2 * heads * dk + 2 * heads * dv
            z, zlr = _matmul(x, w_in[:, :nm].astype(BF16), norm_g=g_mix, gate_w=_pad_cols(w_in[:, nm:], LANES),
                             name="gla_in")
            a, s_ = _gla(z, zlr, p['gla_w_gate2'][j], p['gla_b_gate'][j], p['gla_norm'][j],
                         gS[j] if has_init else None,
                         batch=batch, seq_len=seq_len, heads=heads, dk=dk, dv=dv)
            outs[4].append(s_)
            w_out = p['gla_w_out'][j]
        else:
            qk_heads, v_heads, d = p['gdn_shape']
            w_in = p['gdn_w_in'][j]
            conv_ch = (2 * qk_heads + v_heads) * d
            nm = conv_ch + v_heads * d
            width = p['gdn_conv_w'].shape[1]
            cbuf8 = jnp.zeros((batch, SUBLANES, nm), F32)
            if has_init:
                cbuf8 = cbuf8.at[:, SUBLANES - (width - 1):, :conv_ch].set(dconv[j].astype(F32))
            z, zba, tail = _gdn_in(x, g_mix, w_in[:, :nm].astype(BF16), _pad_cols(w_in[:, nm:], LANES), cbuf8,
                                   p['gdn_conv_w'][j], seq_len=seq_len, qk_heads=qk_heads, v_heads=v_heads, d=d)
            a, s_ = _gdn(z, zba, p['gdn_A_log'][j], p['gdn_dt_bias'][j], p['gdn_norm'][j],
                         dS[j] if has_init else None,
                         batch=batch, seq_len=seq_len, qk_heads=qk_heads, v_heads=v_heads, d=d)
            outs[5].append(s_)
            outs[6].append(tail[:, SUBLANES - (width - 1):, :conv_ch])
            w_out = p['gdn_w_out'][j]
        x = _matmul(a, w_out.astype(BF16), residual=x, out_dtype=F32, name="mix_out")

        f = p['ffn_w_gate'].shape[2]
        fw = p['ffn_conv_w'].shape[1]
        buf8 = jnp.zeros((batch, SUBLANES, f), F32)
        if has_init:
            buf8 = buf8.at[:, SUBLANES - (fw - 1):].set(fconv[i].astype(F32))
        act, tail = _ffn_in(x, p['norm_ffn'][i], p['ffn_w_gate'][i].astype(BF16), p['ffn_w_up'][i].astype(BF16),
                            buf8, p['ffn_conv_w'][i], p['ffn_conv_b'][i], seq_len=seq_len)
        outs[7].append(tail[:, SUBLANES - (fw - 1):])
        x = _matmul(act, p['ffn_w_down'][i].astype(BF16), residual=x, out_dtype=F32, name="ffn_out")
    y = _final_norm(x, p['norm_final'])
    return y.reshape(batch, seq_len, dm), tuple(jnp.stack(o) for o in outs)


def kernel(x_prompt, x_sample, state_mlstm_C, state_mlstm_n, state_mlstm_m, state_ret_S, state_gla_S, state_gdn_S, state_gdn_conv, state_ffn_conv, norm_mix, norm_ffn, norm_final, mlstm_w_in, mlstm_b_gates, mlstm_norm, mlstm_w_out, ret_w_in, ret_norm, ret_w_out, gla_w_in, gla_w_gate2, gla_b_gate, gla_norm, gla_w_out, gdn_w_in, gdn_conv_w, gdn_A_log, gdn_dt_bias, gdn_norm, gdn_w_out, ffn_w_gate, ffn_w_up, ffn_conv_w, ffn_conv_b, ffn_w_down):
    p = dict(norm_mix=norm_mix, norm_ffn=norm_ffn, norm_final=norm_final,
             mlstm_w_in=mlstm_w_in, mlstm_b_gates=mlstm_b_gates, mlstm_norm=mlstm_norm, mlstm_w_out=mlstm_w_out,
             ret_w_in=ret_w_in, ret_norm=ret_norm, ret_w_out=ret_w_out,
             gla_w_in=gla_w_in, gla_w_gate2=gla_w_gate2, gla_b_gate=gla_b_gate, gla_norm=gla_norm, gla_w_out=gla_w_out,
             gdn_w_in=gdn_w_in, gdn_conv_w=gdn_conv_w, gdn_A_log=gdn_A_log, gdn_dt_bias=gdn_dt_bias,
             gdn_norm=gdn_norm, gdn_w_out=gdn_w_out,
             ffn_w_gate=ffn_w_gate, ffn_w_up=ffn_w_up, ffn_conv_w=ffn_conv_w, ffn_conv_b=ffn_conv_b,
             ffn_w_down=ffn_w_down)
    p['mlstm_n_shape'] = state_mlstm_C.shape[2:]
    p['ret_shape'] = state_ret_S.shape[2:]
    p['gla_shape'] = state_gla_S.shape[2:]
    d = state_gdn_S.shape[-1]
    v_heads = state_gdn_S.shape[2]
    p['gdn_shape'] = ((state_gdn_conv.shape[-1] // d - v_heads) // 2, v_heads, d)

    bp, tp, dm = x_prompt.shape
    bs, ts, _ = x_sample.shape
    y_prompt, p_states = _trunk(x_prompt.reshape(bp * tp, dm), None, 0, p, batch=bp, seq_len=tp)
    cache = (state_mlstm_C, state_mlstm_n, state_mlstm_m, state_ret_S, state_gla_S, state_gdn_S,
             state_gdn_conv, state_ffn_conv)
    y_sample, s_states = _trunk(x_sample.reshape(bs * ts, dm), cache, PAST_LEN, p, batch=bs, seq_len=ts)
    return (y_prompt, y_sample) + p_states + s_states
```

```python
import functools
import math

import jax
import jax.numpy as jnp
from jax import lax
from jax.experimental import pallas as pl
from jax.experimental.pallas import tpu as pltpu

F32 = jnp.float32
BF16 = jnp.bfloat16
EPS = 1e-6
PAST_LEN = 2048
ROPE_BASE = 10000.0
GLA_TAU = 16.0
GLA_RANK = 16
CHUNK = 64
MLSTM_CHUNK = 256
RET_CHUNK = 256
GLA_CHUNK = 128
GLA_SUB = 64
GDN_V_HEADS_PER_STEP = 16
FFN_ROW_CHUNK = 256
LANES = 128
SUBLANES = 8
VMEM_LIMIT_BYTES = 56 * 1024 * 1024


def _sigmoid(x):
    return 1.0 / (1.0 + jnp.exp(-x))


def _silu(x):
    return x * _sigmoid(x)


def _log1p_exp_neg_abs(x):
    return jnp.log(1.0 + jnp.exp(-jnp.abs(x)))


def _log_sigmoid(x):
    return jnp.minimum(x, 0.0) - _log1p_exp_neg_abs(x)


def _softplus(x):
    return jnp.maximum(x, 0.0) + _log1p_exp_neg_abs(x)


def _rms(x, g):
    return x * lax.rsqrt(jnp.mean(x * x, axis=-1, keepdims=True) + EPS) * g


def _dot(a, b):
    return jnp.dot(a, b, preferred_element_type=F32)


def _dot_nt(a, b):
    return lax.dot_general(a, b, (((1,), (1,)), ((), ())), preferred_element_type=F32)


def _dot_tn(a, b):
    return lax.dot_general(a, b, (((0,), (0,)), ((), ())), preferred_element_type=F32)


def _split3(x):
    hi = x.astype(BF16)
    r = x - hi.astype(F32)
    mid = r.astype(BF16)
    lo = (r - mid.astype(F32)).astype(BF16)
    return hi, mid, lo


def _cumsum_rows(x):
    n = x.shape[0]
    r = lax.broadcasted_iota(jnp.int32, (n, n), 0)
    c = lax.broadcasted_iota(jnp.int32, (n, n), 1)
    tril = jnp.where(r >= c, 1.0, 0.0).astype(BF16)
    hi, mid, lo = _split3(x)
    return _dot(tril, hi) + _dot(tril, mid) + _dot(tril, lo)


def _transpose_rows(x):
    n = x.shape[0]
    if n < LANES:
        x = jnp.concatenate([x, jnp.zeros((LANES - n, x.shape[1]), x.dtype)], axis=0)
    return x.T[:, :n]


def _causal_taps(x3, prev, cw):
    width = cw.shape[0]
    t8 = lax.broadcasted_iota(jnp.int32, (x3.shape[0], SUBLANES, x3.shape[2]), 1)
    acc = x3 * cw[width - 1:width]
    for s in range(1, width):
        sh = pltpu.roll(x3, s, 1)
        head = sh[:, :SUBLANES]
        for e in range(s):
            head = jnp.where(t8 == e, prev[s - 1 - e], head)
        sh = jnp.concatenate([head, sh[:, SUBLANES:]], axis=1)
        acc = acc + sh * cw[width - 1 - s:width - s]
    return acc


def _tri_masks(n):
    r = lax.broadcasted_iota(jnp.int32, (n, n), 0)
    c = lax.broadcasted_iota(jnp.int32, (n, n), 1)
    return r >= c, r > c


def _mm_kernel(*refs, has_norm, has_res, has_gate):
    it = iter(refs)
    x_ref = next(it)
    g_ref = next(it) if has_norm else None
    w_ref = next(it)
    wgate_ref = next(it) if has_gate else None
    r_ref = next(it) if has_res else None
    o_ref = next(it)
    ogate_ref = next(it) if has_gate else None
    h_scr = next(it) if has_norm else None
    if has_norm:
        @pl.when(pl.program_id(1) == 0)
        def _():
            h = _rms(x_ref[...].astype(F32), g_ref[...])
            h_scr[...] = h.astype(BF16)
            if has_gate:
                both = _dot(h.astype(BF16), wgate_ref[...])
                ogate_ref[...] = both[:, :LANES] + both[:, LANES:]
        h = h_scr[...]
    else:
        h = x_ref[...]
    acc = _dot(h, w_ref[...])
    if has_res:
        acc = acc + r_ref[...]
    o_ref[...] = acc.astype(o_ref.dtype)


def _pick(n, cands):
    for c in cands:
        if n % c == 0:
            return c
    raise ValueError(f"no tile for {n}")


def _matmul(x, w, *, norm_g=None, gate_w=None, residual=None, out_dtype=BF16, bm=1024, bn=None, name="mm"):
    m, k = x.shape
    n = w.shape[1]
    bm = min(bm, m)
    if bn is None:
        bn = _pick(n, (1024, 512, 256, 128)) if k <= 2048 else _pick(n, (512, 256, 128))
    assert m % bm == 0 and n % bn == 0
    has_norm, has_res, has_gate = norm_g is not None, residual is not None, gate_w is not None
    assert has_norm or not has_gate
    in_specs = [pl.BlockSpec((bm, k), lambda i, j: (i, 0))]
    args = [x]
    if has_norm:
        in_specs.append(pl.BlockSpec((1, k), lambda i, j: (0, 0)))
        args.append(norm_g.reshape(1, k).astype(F32))
    in_specs.append(pl.BlockSpec((k, bn), lambda i, j: (0, j)))
    args.append(w)
    if has_gate:
        g_hi = gate_w.astype(BF16)
        g_lo = (gate_w - g_hi.astype(F32)).astype(BF16)
        in_specs.append(pl.BlockSpec((k, 2 * LANES), lambda i, j: (0, 0)))
        args.append(jnp.concatenate([g_hi, g_lo], axis=1))
    if has_res:
        in_specs.append(pl.BlockSpec((bm, bn), lambda i, j: (i, j)))
        args.append(residual)
    out_specs = [pl.BlockSpec((bm, bn), lambda i, j: (i, j))]
    out_shape = [jax.ShapeDtypeStruct((m, n), out_dtype)]
    if has_gate:
        out_specs.append(pl.BlockSpec((bm, LANES), lambda i, j: (i, 0)))
        out_shape.append(jax.ShapeDtypeStruct((m, LANES), F32))
    outs = pl.pallas_call(
        functools.partial(_mm_kernel, has_norm=has_norm, has_res=has_res, has_gate=has_gate),
        grid=(m // bm, n // bn),
        in_specs=in_specs,
        out_specs=out_specs,
        out_shape=out_shape,
        scratch_shapes=[pltpu.VMEM((bm, k), BF16)] if has_norm else [],
        compiler_params=pltpu.CompilerParams(
            dimension_semantics=("arbitrary", "arbitrary"), vmem_limit_bytes=VMEM_LIMIT_BYTES),
        name=name,
    )(*args)
    return outs if has_gate else outs[0]


def _final_norm_kernel(x_ref, g_ref, o_ref):
    o_ref[...] = _rms(x_ref[...], g_ref[...])


def _final_norm(x, g, *, bm=512):
    m, k = x.shape
    bm = min(bm, m)
    return pl.pallas_call(
        _final_norm_kernel,
        grid=(m // bm,),
        in_specs=[pl.BlockSpec((bm, k), lambda i: (i, 0)), pl.BlockSpec((1, k), lambda i: (0, 0))],
        out_specs=pl.BlockSpec((bm, k), lambda i: (i, 0)),
        out_shape=jax.ShapeDtypeStruct((m, k), F32),
        compiler_params=pltpu.CompilerParams(
            dimension_semantics=("arbitrary",), vmem_limit_bytes=VMEM_LIMIT_BYTES),
        name="final_norm",
    )(x, g.reshape(1, k).astype(F32))


def _ffn_in_kernel(x_ref, g_ref, wg_ref, wu_ref, buf_ref, cw_ref, cb_ref, act_ref, tail_ref, h_scr, halo_scr,
                   *, tiles_per_seq, seq_rows):
    i, j = pl.program_id(0), pl.program_id(1)
    bm, bn = act_ref.shape

    @pl.when(j == 0)
    def _():
        h_scr[...] = _rms(x_ref[...], g_ref[...]).astype(BF16)

    nseq = bm // seq_rows
    first = (i % tiles_per_seq) == 0
    halo = jnp.where(first, buf_ref[...], halo_scr[j])
    prev2, prev1 = halo[:, 6:7], halo[:, 7:8]
    cw, cb = cw_ref[...], cb_ref[...]
    wg, wu = wg_ref[...], wu_ref[...]
    rc = min(FFN_ROW_CHUNK, bm)
    spc = max(rc // seq_rows, 1)
    tt = rc // spc
    tails = []
    for r in range(bm // rc):
        rows = slice(r * rc, (r + 1) * rc)
        h = h_scr[rows, :]
        g3 = _dot(h, wg).reshape(spc, tt, bn)
        u3 = _dot(h, wu).reshape(spc, tt, bn)
        pv = [prev1, prev2] if nseq == 1 else [prev1[r * spc:(r + 1) * spc], prev2[r * spc:(r + 1) * spc]]
        gc = _causal_taps(g3, pv, cw) + cb
        act_ref[rows, :] = (_silu(gc) * u3).reshape(rc, bn).astype(act_ref.dtype)
        if nseq == 1:
            prev2, prev1 = g3[:, tt - 2:tt - 1], g3[:, tt - 1:tt]
            tails = [g3[:, tt - SUBLANES:]]
        else:
            tails.append(g3[:, tt - SUBLANES:])
    tail = tails[0] if len(tails) == 1 else jnp.concatenate(tails, axis=0)
    halo_scr[j] = tail
    tail_ref[...] = tail


def _ffn_in(x, norm_g, w_gate, w_up, buf8, conv_w, conv_b, *, seq_len, bm=1024, bn=512):
    m, k = x.shape
    f = w_gate.shape[1]
    batch = m // seq_len
    bm = min(bm, m)
    assert f % bn == 0 and m % bm == 0 and (seq_len % bm == 0 or bm % seq_len == 0)
    nf = f // bn
    tiles_per_seq = max(seq_len // bm, 1)
    seq_rows = min(seq_len, bm)
    nseq = bm // seq_rows
    seq_blk = lambda i, j: (i // tiles_per_seq, 0, j)
    act, tails = pl.pallas_call(
        functools.partial(_ffn_in_kernel, tiles_per_seq=tiles_per_seq, seq_rows=seq_rows),
        grid=(m // bm, nf),
        in_specs=[
            pl.BlockSpec((bm, k), lambda i, j: (i, 0)),
            pl.BlockSpec((1, k), lambda i, j: (0, 0)),
            pl.BlockSpec((k, bn), lambda i, j: (0, j)),
            pl.BlockSpec((k, bn), lambda i, j: (0, j)),
            pl.BlockSpec((nseq, SUBLANES, bn), seq_blk),
            pl.BlockSpec((3, bn), lambda i, j: (0, j)),
            pl.BlockSpec((1, bn), lambda i, j: (0, j)),
        ],
        out_specs=[pl.BlockSpec((bm, bn), lambda i, j: (i, j)),
                   pl.BlockSpec((nseq, SUBLANES, bn), lambda i, j: (i, 0, j))],
        out_shape=[jax.ShapeDtypeStruct((m, f), BF16),
                   jax.ShapeDtypeStruct((m // bm * nseq, SUBLANES, f), F32)],
        scratch_shapes=[pltpu.VMEM((bm, k), BF16), pltpu.VMEM((nf, nseq, SUBLANES, bn), F32)],
        compiler_params=pltpu.CompilerParams(
            dimension_semantics=("arbitrary", "arbitrary"), vmem_limit_bytes=VMEM_LIMIT_BYTES),
        name="ffn_in",
    )(x, norm_g.reshape(1, k).astype(F32), w_gate, w_up, buf8, conv_w.astype(F32),
      conv_b.reshape(1, f).astype(F32))
    return act, tails.reshape(batch, tiles_per_seq, SUBLANES, f)[:, tiles_per_seq - 1]


def _mlstm_kernel(*refs, heads, dk, dv, has_init):
    it = iter(refs)
    z_ref, zg_ref, bias_ref, norm_ref = next(it), next(it), next(it), next(it)
    if has_init:
        c0_ref, n0_ref, m0_ref = next(it), next(it), next(it)
    o_ref, c_ref, n_ref, m_ref = next(it), next(it), next(it), next(it)
    ch = pl.program_id(1)
    L = z_ref.shape[0]

    @pl.when(ch == 0)
    def _():
        if has_init:
            c_ref[...] = c0_ref[...]
            n_ref[...] = n0_ref[...]
            m_ref[...] = m0_ref[...]
        else:
            c_ref[...] = jnp.zeros_like(c_ref)
            n_ref[...] = jnp.zeros_like(n_ref)
            m_ref[...] = jnp.zeros_like(m_ref)

    g = zg_ref[...] + bias_ref[...]
    lane = lax.broadcasted_iota(jnp.int32, g.shape, 1)
    u = jnp.where(lane < heads, g, _cumsum_rows(_log_sigmoid(g)))
    ut = _transpose_rows(u)
    tri, _ = _tri_masks(L)
    m_all = m_ref[0]
    m_lane = lax.broadcasted_iota(jnp.int32, m_all.shape, 1)
    qoff, koff, voff, ooff = 0, heads * dk, 2 * heads * dk, 2 * heads * dk + heads * dv
    hs = range(heads)
    ig_col = [u[:, h:h + 1] for h in hs]
    ig_row = [ut[h:h + 1, :] for h in hs]
    b_col = [u[:, heads + h:heads + h + 1] for h in hs]
    b_row = [ut[heads + h:heads + h + 1, :] for h in hs]
    m_prev = [m_all[:, h:h + 1] for h in hs]
    q = [(z_ref[:, qoff + h * dk:qoff + (h + 1) * dk].astype(F32) * dk ** -0.5).astype(BF16) for h in hs]
    k = [z_ref[:, koff + h * dk:koff + (h + 1) * dk] for h in hs]
    v = [z_ref[:, voff + h * dv:voff + (h + 1) * dv] for h in hs]
    c_old = [c_ref[0, h] for h in hs]
    n_old = [n_ref[0, h:h + 1, :] for h in hs]
    d = [jnp.where(tri, b_col[h] - b_row[h] + ig_row[h], -jnp.inf) for h in hs]
    li = [b_col[h] + m_prev[h] for h in hs]
    m_t = [jnp.maximum(li[h], jnp.max(d[h], axis=1, keepdims=True)) for h in hs]
    w_inter = [jnp.exp(li[h] - m_t[h]) for h in hs]
    a = [_dot_nt(q[h], k[h]) * jnp.exp(d[h] - m_t[h]) for h in hs]
    num = [_dot(a[h].astype(BF16), v[h]) + w_inter[h] * _dot(q[h], c_old[h].astype(BF16)) for h in hs]
    den = [jnp.sum(a[h], axis=1, keepdims=True)
           + w_inter[h] * jnp.sum(q[h].astype(F32) * n_old[h], axis=1, keepdims=True) for h in hs]
    hh = [num[h] / jnp.maximum(jnp.abs(den[h]), jnp.exp(-m_t[h])) for h in hs]
    for h in hs:
        m_new = m_t[h][L - 1:L, :]
        b_last = b_col[h][L - 1:L, :]
        w_k = jnp.exp(b_last - b_col[h] + ig_col[h] - m_new)
        dec = jnp.exp(b_last + m_prev[h] - m_new)
        kw = k[h].astype(F32) * w_k
        c_ref[0, h] = dec * c_old[h] + _dot_tn(kw.astype(BF16), v[h])
        n_ref[0, h:h + 1, :] = dec * n_old[h] + jnp.sum(kw, axis=0, keepdims=True)
        m_all = jnp.where(m_lane == h, m_new, m_all)
    for h in hs:
        og = z_ref[:, ooff + h * dv:ooff + (h + 1) * dv].astype(F32)
        y = _sigmoid(og) * _rms(hh[h], norm_ref[...])
        o_ref[:, h * dv:(h + 1) * dv] = y.astype(o_ref.dtype)
    m_ref[0] = m_all


def _mlstm(z, zg, bias, norm, init, *, batch, seq_len, heads, dk, dv):
    L = min(MLSTM_CHUNK, seq_len)
    nc = seq_len // L
    has_init = init is not None
    row = lambda b, c: (b * nc + c, 0)
    in_specs = [pl.BlockSpec((L, z.shape[1]), row),
                pl.BlockSpec((L, LANES), row),
                pl.BlockSpec((1, LANES), lambda b, c: (0, 0)),
                pl.BlockSpec((1, dv), lambda b, c: (0, 0))]
    args = [z, zg, bias, norm.reshape(1, dv).astype(F32)]
    state_specs = [pl.BlockSpec((1, heads, dk, dv), lambda b, c: (b, 0, 0, 0)),
                   pl.BlockSpec((1, heads, dk), lambda b, c: (b, 0, 0)),
                   pl.BlockSpec((1, 1, heads), lambda b, c: (b, 0, 0))]
    if has_init:
        in_specs += state_specs
        args += [init[0], init[1], init[2].reshape(batch, 1, heads)]
    o, c_new, n_new, m_new = pl.pallas_call(
        functools.partial(_mlstm_kernel, heads=heads, dk=dk, dv=dv, has_init=has_init),
        grid=(batch, nc),
        in_specs=in_specs,
        out_specs=[pl.BlockSpec((L, heads * dv), row)] + state_specs,
        out_shape=[jax.ShapeDtypeStruct((batch * seq_len, heads * dv), BF16),
                   jax.ShapeDtypeStruct((batch, heads, dk, dv), F32),
                   jax.ShapeDtypeStruct((batch, heads, dk), F32),
                   jax.ShapeDtypeStruct((batch, 1, heads), F32)],
        compiler_params=pltpu.CompilerParams(
            dimension_semantics=("arbitrary", "arbitrary"), vmem_limit_bytes=VMEM_LIMIT_BYTES),
        name="mlstm",
    )(*args)
    return o, (c_new, n_new, m_new.reshape(batch, heads))


def _ret_kernel(*refs, heads, dk, dv, has_init):
    it = iter(refs)
    cos_ref, sin_ref, z_ref, norm_ref = next(it), next(it), next(it), next(it)
    s0_ref = next(it) if has_init else None
    o_ref, s_ref = next(it), next(it)
    ch = pl.program_id(1)
    L = z_ref.shape[0]
    half = dk // 2

    @pl.when(ch == 0)
    def _():
        s_ref[...] = s0_ref[...] if has_init else jnp.zeros_like(s_ref)

    cos, sin = cos_ref[...], sin_ref[...]
    t_col = lax.broadcasted_iota(jnp.int32, (L, 1), 0).astype(F32)
    rel = (lax.broadcasted_iota(jnp.int32, (L, L), 0) - lax.broadcasted_iota(jnp.int32, (L, L), 1)).astype(F32)
    koff, voff, goff = heads * dk, 2 * heads * dk, 2 * heads * dk + heads * dv

    def rope(x):
        x1, x2 = x[:, :half], x[:, half:]
        return jnp.concatenate([x1 * cos - x2 * sin, x1 * sin + x2 * cos], axis=-1)

    for h in range(heads):
        lg = math.log1p(-2.0 ** (-5.0 - h))
        q = rope(z_ref[:, h * dk:(h + 1) * dk].astype(F32)) * dk ** -0.5
        k = rope(z_ref[:, koff + h * dk:koff + (h + 1) * dk].astype(F32))
        v = z_ref[:, voff + h * dv:voff + (h + 1) * dv]
        gate = z_ref[:, goff + h * dv:goff + (h + 1) * dv].astype(F32)
        qb = q.astype(BF16)
        dec = jnp.where(rel >= 0, jnp.exp(jnp.maximum(rel, 0.0) * lg), 0.0)
        a = _dot_nt(qb, k.astype(BF16)) * dec
        s_old = s_ref[0, h]
        o = _dot(a.astype(BF16), v) + jnp.exp((t_col + 1.0) * lg) * _dot(qb, s_old.astype(BF16))
        kd = k * jnp.exp((L - 1.0 - t_col) * lg)
        s_ref[0, h] = math.exp(L * lg) * s_old + _dot_tn(kd.astype(BF16), v)
        y = _silu(gate) * _rms(o, norm_ref[...])
        o_ref[:, h * dv:(h + 1) * dv] = y.astype(o_ref.dtype)


def _retention(z, norm, init, pos0, *, batch, seq_len, heads, dk, dv):
    L = min(RET_CHUNK, seq_len)
    nc = seq_len // L
    has_init = init is not None
    half = dk // 2
    inv = ROPE_BASE ** (-jnp.arange(half, dtype=F32) / half)
    ang = (pos0 + jnp.arange(seq_len)).astype(F32)[:, None] * inv[None]
    row = lambda b, c: (b * nc + c, 0)
    in_specs = [pl.BlockSpec((L, half), lambda b, c: (c, 0)),
                pl.BlockSpec((L, half), lambda b, c: (c, 0)),
                pl.BlockSpec((L, z.shape[1]), row),
                pl.BlockSpec((1, dv), lambda b, c: (0, 0))]
    args = [jnp.cos(ang), jnp.sin(ang), z, norm.reshape(1, dv).astype(F32)]
    state_spec = pl.BlockSpec((1, heads, dk, dv), lambda b, c: (b, 0, 0, 0))
    if has_init:
        in_specs.append(state_spec)
        args.append(init)
    o, s_new = pl.pallas_call(
        functools.partial(_ret_kernel, heads=heads, dk=dk, dv=dv, has_init=has_init),
        grid=(batch, nc),
        in_specs=in_specs,
        out_specs=[pl.BlockSpec((L, heads * dv), row), state_spec],
        out_shape=[jax.ShapeDtypeStruct((batch * seq_len, heads * dv), BF16),
                   jax.ShapeDtypeStruct((batch, heads, dk, dv), F32)],
        compiler_params=pltpu.CompilerParams(
            dimension_semantics=("arbitrary", "arbitrary"), vmem_limit_bytes=VMEM_LIMIT_BYTES),
        name="retention",
    )(*args)
    return o, s_new


def _gla_kernel(*refs, heads, dk, dv, has_init):
    it = iter(refs)
    z_ref, zlr_ref, w2_ref, bg_ref, norm_ref = next(it), next(it), next(it), next(it), next(it)
    s0_ref = next(it) if has_init else None
    o_ref, s_ref, st_scr = next(it), next(it), next(it)
    ch = pl.program_id(1)
    nch = pl.num_programs(1)
    L = z_ref.shape[0]

    @pl.when(ch == 0)
    def _():
        for h in range(heads):
            st_scr[h] = s0_ref[0, h].T if has_init else jnp.zeros((dv, dk), F32)

    la = _log_sigmoid(_dot(zlr_ref[...].astype(BF16), w2_ref[...]) + bg_ref[...]) / GLA_TAU
    b = _cumsum_rows(la)
    sub = min(GLA_SUB, L)
    tri, _ = _tri_masks(sub)
    koff, voff, roff = heads * dk, 2 * heads * dk, 2 * heads * dk + heads * dv
    for h in range(heads):
        bh = b[:, h * dk:(h + 1) * dk]
        bl = bh[L - 1:L, :]
        q = z_ref[:, h * dk:(h + 1) * dk].astype(F32) * dk ** -0.5
        k = z_ref[:, koff + h * dk:koff + (h + 1) * dk].astype(F32)
        v = z_ref[:, voff + h * dv:voff + (h + 1) * dv]
        r = z_ref[:, roff + h * dv:roff + (h + 1) * dv].astype(F32)
        st_old = st_scr[h]
        sb = st_old.astype(BF16)
        parts = []
        for c in range(L // sub):
            rows = slice(c * sub, (c + 1) * sub)
            ref = bh[c * sub - 1:c * sub, :] if c else jnp.zeros_like(bl)
            qd = (q[rows] * jnp.exp(bh[rows] - ref)).astype(BF16)
            kd = (k[rows] * jnp.exp(ref - bh[rows])).astype(BF16)
            o = _dot(jnp.where(tri, _dot_nt(qd, kd), 0.0).astype(BF16), v[rows])
            for c2 in range(c):
                rows2 = slice(c2 * sub, (c2 + 1) * sub)
                ke = (k[rows2] * jnp.exp(ref - bh[rows2])).astype(BF16)
                o = o + _dot(_dot_nt(qd, ke).astype(BF16), v[rows2])
            o = o + _dot_nt((q[rows] * jnp.exp(bh[rows])).astype(BF16), sb)
            parts.append(o)
        o = parts[0] if len(parts) == 1 else jnp.concatenate(parts, axis=0)
        ke = (k * jnp.exp(bl - bh)).astype(BF16)
        st_scr[h] = jnp.exp(bl) * st_old + _dot_tn(v, ke)
        y = _silu(r) * _rms(o, norm_ref[...])
        o_ref[:, h * dv:(h + 1) * dv] = y.astype(o_ref.dtype)

    @pl.when(ch == nch - 1)
    def _():
        for h in range(heads):
            s_ref[0, h] = st_scr[h].T


def _gla(z, zlr, w_gate2, b_gate, norm, init, *, batch, seq_len, heads, dk, dv):
    L = min(GLA_CHUNK, seq_len)
    nc = seq_len // L
    has_init = init is not None
    hk = heads * dk
    w2 = jnp.zeros((LANES, hk), BF16).at[:GLA_RANK].set(w_gate2.astype(BF16))
    row = lambda b, c: (b * nc + c, 0)
    in_specs = [pl.BlockSpec((L, z.shape[1]), row),
                pl.BlockSpec((L, LANES), row),
                pl.BlockSpec((LANES, hk), lambda b, c: (0, 0)),
                pl.BlockSpec((1, hk), lambda b, c: (0, 0)),
                pl.BlockSpec((1, dv), lambda b, c: (0, 0))]
    args = [z, zlr, w2, b_gate.reshape(1, hk).astype(F32), norm.reshape(1, dv).astype(F32)]
    state_spec = pl.BlockSpec((1, heads, dk, dv), lambda b, c: (b, 0, 0, 0))
    if has_init:
        in_specs.append(state_spec)
        args.append(init)
    o, s_new = pl.pallas_call(
        functools.partial(_gla_kernel, heads=heads, dk=dk, dv=dv, has_init=has_init),
        grid=(batch, nc),
        in_specs=in_specs,
        out_specs=[pl.BlockSpec((L, heads * dv), row), state_spec],
        out_shape=[jax.ShapeDtypeStruct((batch * seq_len, heads * dv), BF16),
                   jax.ShapeDtypeStruct((batch, heads, dk, dv), F32)],
        scratch_shapes=[pltpu.VMEM((heads, dv, dk), F32)],
        compiler_params=pltpu.CompilerParams(
            dimension_semantics=("arbitrary", "arbitrary"), vmem_limit_bytes=VMEM_LIMIT_BYTES),
        name="gla",
    )(*args)
    return o, s_new


def _unit_lower_inverses(mats):
    n = mats[0].shape[0]
    r = lax.broadcasted_iota(jnp.int32, (n, n), 0)
    c = lax.broadcasted_iota(jnp.int32, (n, n), 1)
    eye = jnp.where(r == c, 1.0, 0.0)
    ps = [eye - a for a in mats]
    pws = [_dot(a.astype(BF16), a.astype(BF16)) for a in mats]
    covered = 2
    while covered < n:
        if 2 * covered < n:
            both = [_dot(jnp.concatenate([p, pw], axis=0).astype(BF16), pw.astype(BF16)) for p, pw in zip(ps, pws)]
            ps = [p + b[:n] for p, b in zip(ps, both)]
            pws = [b[n:] for b in both]
        else:
            ps = [p + _dot(p.astype(BF16), pw.astype(BF16)) for p, pw in zip(ps, pws)]
        covered *= 2
    return ps


def _gdn_kernel(*refs, qk_heads, v_heads, d, has_init):
    it = iter(refs)
    z_ref, zba_ref, cw_ref, aneg_ref, dtb_ref, norm_ref = (next(it) for _ in range(6))
    if has_init:
        s0_ref, buf0_ref = next(it), next(it)
    o_ref, s_ref = next(it), next(it)
    xs_scr, qkv_scr, gc_scr, gr_scr, bc_scr = (next(it) for _ in range(5))
    ch = pl.program_id(1)
    L = z_ref.shape[0]
    rep = v_heads // qk_heads
    conv_ch = (2 * qk_heads + v_heads) * d
    width = cw_ref.shape[0]
    pad = SUBLANES

    @pl.when(ch == 0)
    def _():
        s_ref[...] = s0_ref[...] if has_init else jnp.zeros_like(s_ref)
        xs_scr[0:pad, :] = jnp.zeros((pad, conv_ch), F32)
        if has_init:
            xs_scr[pad - width + 1:pad, :] = buf0_ref[0]

    ba = zba_ref[...]
    lane = lax.broadcasted_iota(jnp.int32, ba.shape, 1)
    beta = _sigmoid(ba)
    la = aneg_ref[...] * _softplus(ba + dtb_ref[...])
    g = _cumsum_rows(jnp.where(lane >= v_heads, la, 0.0))
    gt = _transpose_rows(g)
    for j in range(v_heads):
        gc_scr[j] = jnp.broadcast_to(g[:, v_heads + j:v_heads + j + 1], (L, LANES))
        bc_scr[j] = jnp.broadcast_to(beta[:, j:j + 1], (L, LANES))
        gr_scr[j] = jnp.broadcast_to(gt[v_heads + j:v_heads + j + 1, :], (SUBLANES, L))

    for blk in range(conv_ch // d):
        sl = slice(blk * d, (blk + 1) * d)
        xs_scr[pad:pad + L, sl] = z_ref[:, sl].astype(F32)
        w = cw_ref[:, sl]
        acc = xs_scr[pad:pad + L, sl] * w[width - 1:width]
        for t in range(1, width):
            acc = acc + xs_scr[pad - t:pad - t + L, sl] * w[width - 1 - t:width - t]
        xs_scr[0:pad, sl] = xs_scr[L:L + pad, sl]
        x = _silu(acc)
        if blk < 2 * qk_heads:
            x = x * lax.rsqrt(jnp.sum(x * x, axis=-1, keepdims=True) + EPS)
            if blk < qk_heads:
                x = x * d ** -0.5
        qkv_scr[:, sl] = x

    causal, strict = _tri_masks(L)
    gv = GDN_V_HEADS_PER_STEP
    gk = gv // rep

    def head_group(grp):
        def cols(base):
            return slice(base * d, (base + 1) * d)
        q = [qkv_scr[:, cols(grp * gk + a)] for a in range(gk)]
        k = [qkv_scr[:, cols(qk_heads + grp * gk + a)] for a in range(gk)]
        qb = [x.astype(BF16) for x in q]
        kb = [x.astype(BF16) for x in k]
        kk = [_dot_nt(x, x) for x in kb]
        qk = [_dot_nt(x, y) for x, y in zip(qb, kb)]
        heads = range(gv)
        js = [grp * gv + r for r in heads]
        v = [qkv_scr[:, cols(2 * qk_heads + j)] for j in js]
        g_col = [gc_scr[j] for j in js]
        g_row = [gr_scr[j][0:1, :] for j in js]
        b_col = [bc_scr[j] for j in js]
        s_old = [s_ref[0, j] for j in js]
        dmat = [jnp.exp(jnp.where(causal, g_col[r][:, :L] - g_row[r], -jnp.inf)) for r in heads]
        tinv = _unit_lower_inverses(
            [jnp.where(strict, b_col[r][:, :L] * dmat[r] * kk[r // rep], 0.0) for r in heads])
        eg = [jnp.exp(x) for x in g_col]
        rhs = [jnp.concatenate([b_col[r] * v[r], b_col[r] * eg[r] * k[r // rep]], axis=-1).astype(BF16)
               for r in heads]
        sol = [_dot(tinv[r].astype(BF16), rhs[r]) for r in heads]
        sb = [x.astype(BF16) for x in s_old]
        ub = [(sol[r][:, :d] - _dot(sol[r][:, d:].astype(BF16), sb[r])).astype(BF16) for r in heads]
        o = [eg[r] * _dot(qb[r // rep], sb[r]) + _dot((qk[r // rep] * dmat[r]).astype(BF16), ub[r])
             for r in heads]
        for r in heads:
            g_last = g_col[r][L - 1:L, :]
            kd = (jnp.exp(g_last - g_col[r]) * k[r // rep]).astype(BF16)
            s_ref[0, js[r]] = jnp.exp(g_last) * s_old[r] + _dot_tn(kd, ub[r])
        for r in heads:
            zg = z_ref[:, cols(conv_ch // d + js[r])].astype(F32)
            y = _rms(o[r], norm_ref[...]) * _silu(zg)
            o_ref[:, cols(js[r])] = y.astype(o_ref.dtype)

    assert v_heads % gv == 0 and gv % rep == 0
    for grp in range(v_heads // gv):
        head_group(grp)


def _gdn(z, zba, conv_w, a_log, dt_bias, norm, init, *, batch, seq_len, qk_heads, v_heads, d):
    L = min(CHUNK, seq_len)
    nc = seq_len // L
    has_init = init is not None
    conv_ch = (2 * qk_heads + v_heads) * d
    width = conv_w.shape[0]
    lane_pad = lambda x: jnp.zeros((1, LANES), F32).at[0, v_heads:2 * v_heads].set(x.astype(F32))
    row = lambda b, c: (b * nc + c, 0)
    const = lambda b, c: (0, 0)
    in_specs = [pl.BlockSpec((L, z.shape[1]), row),
                pl.BlockSpec((L, LANES), row),
                pl.BlockSpec((width, conv_ch), const),
                pl.BlockSpec((1, LANES), const),
                pl.BlockSpec((1, LANES), const),
                pl.BlockSpec((1, d), const)]
    args = [z, zba, conv_w.astype(F32), lane_pad(-jnp.exp(a_log)), lane_pad(dt_bias),
            norm.reshape(1, d).astype(F32)]
    state_spec = pl.BlockSpec((1, v_heads, d, d), lambda b, c: (b, 0, 0, 0))
    if has_init:
        in_specs += [state_spec, pl.BlockSpec((1, width - 1, conv_ch), lambda b, c: (b, 0, 0))]
        args += [init[0], init[1]]
    o, s_new = pl.pallas_call(
        functools.partial(_gdn_kernel, qk_heads=qk_heads, v_heads=v_heads, d=d, has_init=has_init),
        grid=(batch, nc),
        in_specs=in_specs,
        out_specs=[pl.BlockSpec((L, v_heads * d), row), state_spec],
        out_shape=[jax.ShapeDtypeStruct((batch * seq_len, v_heads * d), BF16),
                   jax.ShapeDtypeStruct((batch, v_heads, d, d), F32)],
        scratch_shapes=[pltpu.VMEM((L + SUBLANES, conv_ch), F32),
                        pltpu.VMEM((L, conv_ch), F32),
                        pltpu.VMEM((v_heads, L, LANES), F32),
                        pltpu.VMEM((v_heads, SUBLANES, L), F32),
                        pltpu.VMEM((v_heads, L, LANES), F32)],
        compiler_params=pltpu.CompilerParams(
            dimension_semantics=("arbitrary", "arbitrary"), vmem_limit_bytes=VMEM_LIMIT_BYTES),
        name="gdn",
    )(*args)
    return o, s_new


def _pad_cols(w, n):
    return jnp.zeros((w.shape[0], n), F32).at[:, :w.shape[1]].set(w.astype(F32))


def _trunk(x, states, pos0, p, *, batch, seq_len):
    m, dm = x.shape
    has_init = states is not None
    if has_init:
        mC, mn, mm, rS, gS, dS, dconv, fconv = states
    depth = p['norm_mix'].shape[0]
    outs = [[] for _ in range(8)]
    for i in range(depth):
        kind, j = i % 4, i // 4
        g_mix = p['norm_mix'][i]
        if kind == 0:
            heads, dk, dv = p['mlstm_n_shape']
            w_in = p['mlstm_w_in'][j]
            nm = 2 * heads * dk + 2 * heads * dv
            z, zg = _matmul(x, w_in[:, :nm].astype(BF16), norm_g=g_mix, gate_w=_pad_cols(w_in[:, nm:], LANES),
                            name="mlstm_in")
            bias = _pad_cols(p['mlstm_b_gates'][j][None], LANES)
            init = (mC[j], mn[j], mm[j]) if has_init else None
            a, (c_, n_, m_) = _mlstm(z, zg, bias, p['mlstm_norm'][j], init,
                                     batch=batch, seq_len=seq_len, heads=heads, dk=dk, dv=dv)
            outs[0].append(c_); outs[1].append(n_); outs[2].append(m_)
            w_out = p['mlstm_w_out'][j]
        elif kind == 1:
            heads, dk, dv = p['ret_shape']
            z = _matmul(x, p['ret_w_in'][j].astype(BF16), norm_g=g_mix, name="ret_in")
            a, s_ = _retention(z, p['ret_norm'][j], rS[j] if has_init else None, pos0,
                               batch=batch, seq_len=seq_len, heads=heads, dk=dk, dv=dv)
            outs[3].append(s_)
            w_out = p['ret_w_out'][j]
        elif kind == 2:
            heads, dk, dv = p['gla_shape']
            w_in = p['gla_w_in'][j]
            nm = 2 * heads * dk + 2 * heads * dv
            z, zlr = _matmul(x, w_in[:, :nm].astype(BF16), norm_g=g_mix, gate_w=_pad_cols(w_in[:, nm:], LANES),
                             name="gla_in")
            a, s_ = _gla(z, zlr, p['gla_w_gate2'][j], p['gla_b_gate'][j], p['gla_norm'][j],
                         gS[j] if has_init else None,
                         batch=batch, seq_len=seq_len, heads=heads, dk=dk, dv=dv)
            outs[4].append(s_)
            w_out = p['gla_w_out'][j]
        else:
            qk_heads, v_heads, d = p['gdn_shape']
            w_in = p['gdn_w_in'][j]
            conv_ch = (2 * qk_heads + v_heads) * d
            nm = conv_ch + v_heads * d
            width = p['gdn_conv_w'].shape[1]
            z, zba = _matmul(x, w_in[:, :nm].astype(BF16), norm_g=g_mix, gate_w=_pad_cols(w_in[:, nm:], LANES),
                             name="gdn_in")
            init = (dS[j], dconv[j]) if has_init else None
            a, s_ = _gdn(z, zba, p['gdn_conv_w'][j], p['gdn_A_log'][j], p['gdn_dt_bias'][j], p['gdn_norm'][j],
                         init, batch=batch, seq_len=seq_len, qk_heads=qk_heads, v_heads=v_heads, d=d)
            outs[5].append(s_)
            outs[6].append(z.reshape(batch, seq_len, nm)[:, seq_len - (width - 1):, :conv_ch].astype(F32))
            w_out = p['gdn_w_out'][j]
        x = _matmul(a, w_out.astype(BF16), residual=x, out_dtype=F32, name="mix_out")

        f = p['ffn_w_gate'].shape[2]
        fw = p['ffn_conv_w'].shape[1]
        buf8 = jnp.zeros((batch, SUBLANES, f), F32)
        if has_init:
            buf8 = buf8.at[:, SUBLANES - (fw - 1):].set(fconv[i].astype(F32))
        act, tail = _ffn_in(x, p['norm_ffn'][i], p['ffn_w_gate'][i].astype(BF16), p['ffn_w_up'][i].astype(BF16),
                            buf8, p['ffn_conv_w'][i], p['ffn_conv_b'][i], seq_len=seq_len)
        outs[7].append(tail[:, SUBLANES - (fw - 1):])
        x = _matmul(act, p['ffn_w_down'][i].astype(BF16), residual=x, out_dtype=F32, name="ffn_out")
    y = _final_norm(x, p['norm_final'])
    return y.reshape(batch, seq_len, dm), tuple(jnp.stack(o) for o in outs)


def kernel(x_prompt, x_sample, state_mlstm_C, state_mlstm_n, state_mlstm_m, state_ret_S, state_gla_S, state_gdn_S, state_gdn_conv, state_ffn_conv, norm_mix, norm_ffn, norm_final, mlstm_w_in, mlstm_b_gates, mlstm_norm, mlstm_w_out, ret_w_in, ret_norm, ret_w_out, gla_w_in, gla_w_gate2, gla_b_gate, gla_norm, gla_w_out, gdn_w_in, gdn_conv_w, gdn_A_log, gdn_dt_bias, gdn_norm, gdn_w_out, ffn_w_gate, ffn_w_up, ffn_conv_w, ffn_conv_b, ffn_w_down):
    p = dict(norm_mix=norm_mix, norm_ffn=norm_ffn, norm_final=norm_final,
             mlstm_w_in=mlstm_w_in, mlstm_b_gates=mlstm_b_gates, mlstm_norm=mlstm_norm, mlstm_w_out=mlstm_w_out,
             ret_w_in=ret_w_in, ret_norm=ret_norm, ret_w_out=ret_w_out,
             gla_w_in=gla_w_in, gla_w_gate2=gla_w_gate2, gla_b_gate=gla_b_gate, gla_norm=gla_norm, gla_w_out=gla_w_out,
             gdn_w_in=gdn_w_in, gdn_conv_w=gdn_conv_w, gdn_A_log=gdn_A_log, gdn_dt_bias=gdn_dt_bias,
             gdn_norm=gdn_norm, gdn_w_out=gdn_w_out,
             ffn_w_gate=ffn_w_gate, ffn_w_up=ffn_w_up, ffn_conv_w=ffn_conv_w, ffn_conv_b=ffn_conv_b,
             ffn_w_down=ffn_w_down)
    p['mlstm_n_shape'] = state_mlstm_C.shape[2:]
    p['ret_shape'] = state_ret_S.shape[2:]
    p['gla_shape'] = state_gla_S.shape[2:]
    d = state_gdn_S.shape[-1]
    v_heads = state_gdn_S.shape[2]
    p['gdn_shape'] = ((state_gdn_conv.shape[-1] // d - v_heads) // 2, v_heads, d)

    bp, tp, dm = x_prompt.shape
    bs, ts, _ = x_sample.shape
    y_prompt, p_states = _trunk(x_prompt.reshape(bp * tp, dm), None, 0, p, batch=bp, seq_len=tp)
    cache = (state_mlstm_C, state_mlstm_n, state_mlstm_m, state_ret_S, state_gla_S, state_gdn_S,
             state_gdn_conv, state_ffn_conv)
    y_sample, s_states = _trunk(x_sample.reshape(bs * ts, dm), cache, PAST_LEN, p, batch=bs, seq_len=ts)
    return (y_prompt, y_sample) + p_states + s_states
```

```python
import functools
import math

import jax
import jax.numpy as jnp
from jax import lax
from jax.experimental import pallas as pl
from jax.experimental.pallas import tpu as pltpu

F32 = jnp.float32
BF16 = jnp.bfloat16
EPS = 1e-6
PAST_LEN = 2048
ROPE_BASE = 10000.0
GLA_TAU = 16.0
GLA_RANK = 16
CHUNK = 64
MLSTM_CHUNK = 256
RET_CHUNK = 256
GLA_CHUNK = 128
GLA_SUB = 64
GDN_V_HEADS_PER_STEP = 32
GDN_CONV_HISTORY = 16
GDN_CONV_BLOCK = 256
FFN_ROW_CHUNK = 256
LANES = 128
SUBLANES = 8
VMEM_LIMIT_BYTES = 56 * 1024 * 1024


def _sigmoid(x):
    return 1.0 / (1.0 + jnp.exp(-x))


def _silu(x):
    return x * _sigmoid(x)


def _log1p_exp_neg_abs(x):
    return jnp.log(1.0 + jnp.exp(-jnp.abs(x)))


def _log_sigmoid(x):
    return jnp.minimum(x, 0.0) - _log1p_exp_neg_abs(x)


def _softplus(x):
    return jnp.maximum(x, 0.0) + _log1p_exp_neg_abs(x)


def _rms(x, g):
    return x * lax.rsqrt(jnp.mean(x * x, axis=-1, keepdims=True) + EPS) * g


def _dot(a, b):
    return jnp.dot(a, b, preferred_element_type=F32)


def _dot_nt(a, b):
    return lax.dot_general(a, b, (((1,), (1,)), ((), ())), preferred_element_type=F32)


def _dot_tn(a, b):
    return lax.dot_general(a, b, (((0,), (0,)), ((), ())), preferred_element_type=F32)


def _split3(x):
    hi = x.astype(BF16)
    r = x - hi.astype(F32)
    mid = r.astype(BF16)
    lo = (r - mid.astype(F32)).astype(BF16)
    return hi, mid, lo


def _cumsum_rows(x):
    n = x.shape[0]
    r = lax.broadcasted_iota(jnp.int32, (n, n), 0)
    c = lax.broadcasted_iota(jnp.int32, (n, n), 1)
    tril = jnp.where(r >= c, 1.0, 0.0).astype(BF16)
    hi, mid, lo = _split3(x)
    return _dot(tril, hi) + _dot(tril, mid) + _dot(tril, lo)


def _transpose_rows(x):
    n = x.shape[0]
    if n < LANES:
        x = jnp.concatenate([x, jnp.zeros((LANES - n, x.shape[1]), x.dtype)], axis=0)
    return x.T[:, :n]


def _causal_taps(x3, prev, cw):
    width = cw.shape[0]
    t8 = lax.broadcasted_iota(jnp.int32, (x3.shape[0], SUBLANES, x3.shape[2]), 1)
    acc = x3 * cw[width - 1:width]
    for s in range(1, width):
        sh = pltpu.roll(x3, s, 1)
        head = sh[:, :SUBLANES]
        for e in range(s):
            head = jnp.where(t8 == e, prev[s - 1 - e], head)
        sh = jnp.concatenate([head, sh[:, SUBLANES:]], axis=1)
        acc = acc + sh * cw[width - 1 - s:width - s]
    return acc


def _tri_masks(n):
    r = lax.broadcasted_iota(jnp.int32, (n, n), 0)
    c = lax.broadcasted_iota(jnp.int32, (n, n), 1)
    return r >= c, r > c


def _mm_kernel(*refs, has_norm, has_res, has_gate):
    it = iter(refs)
    x_ref = next(it)
    g_ref = next(it) if has_norm else None
    w_ref = next(it)
    wgate_ref = next(it) if has_gate else None
    r_ref = next(it) if has_res else None
    o_ref = next(it)
    ogate_ref = next(it) if has_gate else None
    h_scr = next(it) if has_norm else None
    if has_norm:
        @pl.when(pl.program_id(1) == 0)
        def _():
            h = _rms(x_ref[...].astype(F32), g_ref[...])
            h_scr[...] = h.astype(BF16)
            if has_gate:
                both = _dot(h.astype(BF16), wgate_ref[...])
                ogate_ref[...] = both[:, :LANES] + both[:, LANES:]
        h = h_scr[...]
    else:
        h = x_ref[...]
    acc = _dot(h, w_ref[...])
    if has_res:
        acc = acc + r_ref[...]
    o_ref[...] = acc.astype(o_ref.dtype)


def _pick(n, cands):
    for c in cands:
        if n % c == 0:
            return c
    raise ValueError(f"no tile for {n}")


def _matmul(x, w, *, norm_g=None, gate_w=None, residual=None, out_dtype=BF16, bm=1024, bn=None, name="mm"):
    m, k = x.shape
    n = w.shape[1]
    bm = min(bm, m)
    if bn is None:
        bn = _pick(n, (1024, 512, 256, 128)) if k <= 2048 else _pick(n, (512, 256, 128))
    assert m % bm == 0 and n % bn == 0
    has_norm, has_res, has_gate = norm_g is not None, residual is not None, gate_w is not None
    assert has_norm or not has_gate
    in_specs = [pl.BlockSpec((bm, k), lambda i, j: (i, 0))]
    args = [x]
    if has_norm:
        in_specs.append(pl.BlockSpec((1, k), lambda i, j: (0, 0)))
        args.append(norm_g.reshape(1, k).astype(F32))
    in_specs.append(pl.BlockSpec((k, bn), lambda i, j: (0, j)))
    args.append(w)
    if has_gate:
        g_hi = gate_w.astype(BF16)
        g_lo = (gate_w - g_hi.astype(F32)).astype(BF16)
        in_specs.append(pl.BlockSpec((k, 2 * LANES), lambda i, j: (0, 0)))
        args.append(jnp.concatenate([g_hi, g_lo], axis=1))
    if has_res:
        in_specs.append(pl.BlockSpec((bm, bn), lambda i, j: (i, j)))
        args.append(residual)
    out_specs = [pl.BlockSpec((bm, bn), lambda i, j: (i, j))]
    out_shape = [jax.ShapeDtypeStruct((m, n), out_dtype)]
    if has_gate:
        out_specs.append(pl.BlockSpec((bm, LANES), lambda i, j: (i, 0)))
        out_shape.append(jax.ShapeDtypeStruct((m, LANES), F32))
    outs = pl.pallas_call(
        functools.partial(_mm_kernel, has_norm=has_norm, has_res=has_res, has_gate=has_gate),
        grid=(m // bm, n // bn),
        in_specs=in_specs,
        out_specs=out_specs,
        out_shape=out_shape,
        scratch_shapes=[pltpu.VMEM((bm, k), BF16)] if has_norm else [],
        compiler_params=pltpu.CompilerParams(
            dimension_semantics=("arbitrary", "arbitrary"), vmem_limit_bytes=VMEM_LIMIT_BYTES),
        name=name,
    )(*args)
    return outs if has_gate else outs[0]


def _final_norm_kernel(x_ref, g_ref, o_ref):
    o_ref[...] = _rms(x_ref[...], g_ref[...])


def _final_norm(x, g, *, bm=512):
    m, k = x.shape
    bm = min(bm, m)
    return pl.pallas_call(
        _final_norm_kernel,
        grid=(m // bm,),
        in_specs=[pl.BlockSpec((bm, k), lambda i: (i, 0)), pl.BlockSpec((1, k), lambda i: (0, 0))],
        out_specs=pl.BlockSpec((bm, k), lambda i: (i, 0)),
        out_shape=jax.ShapeDtypeStruct((m, k), F32),
        compiler_params=pltpu.CompilerParams(
            dimension_semantics=("arbitrary",), vmem_limit_bytes=VMEM_LIMIT_BYTES),
        name="final_norm",
    )(x, g.reshape(1, k).astype(F32))


def _ffn_in_kernel(x_ref, g_ref, wg_ref, wu_ref, buf_ref, cw_ref, cb_ref, act_ref, tail_ref, h_scr, halo_scr,
                   *, tiles_per_seq, seq_rows):
    i, j = pl.program_id(0), pl.program_id(1)
    bm, bn = act_ref.shape

    @pl.when(j == 0)
    def _():
        h_scr[...] = _rms(x_ref[...], g_ref[...]).astype(BF16)

    nseq = bm // seq_rows
    first = (i % tiles_per_seq) == 0
    halo = jnp.where(first, buf_ref[...], halo_scr[j])
    prev2, prev1 = halo[:, 6:7], halo[:, 7:8]
    cw, cb = cw_ref[...], cb_ref[...]
    wg, wu = wg_ref[...], wu_ref[...]
    rc = min(FFN_ROW_CHUNK, bm)
    spc = max(rc // seq_rows, 1)
    tt = rc // spc
    tails = []
    for r in range(bm // rc):
        rows = slice(r * rc, (r + 1) * rc)
        h = h_scr[rows, :]
        g3 = _dot(h, wg).reshape(spc, tt, bn)
        u3 = _dot(h, wu).reshape(spc, tt, bn)
        pv = [prev1, prev2] if nseq == 1 else [prev1[r * spc:(r + 1) * spc], prev2[r * spc:(r + 1) * spc]]
        gc = _causal_taps(g3, pv, cw) + cb
        act_ref[rows, :] = (_silu(gc) * u3).reshape(rc, bn).astype(act_ref.dtype)
        if nseq == 1:
            prev2, prev1 = g3[:, tt - 2:tt - 1], g3[:, tt - 1:tt]
            tails = [g3[:, tt - SUBLANES:]]
        else:
            tails.append(g3[:, tt - SUBLANES:])
    tail = tails[0] if len(tails) == 1 else jnp.concatenate(tails, axis=0)
    halo_scr[j] = tail
    tail_ref[...] = tail


def _ffn_in(x, norm_g, w_gate, w_up, buf8, conv_w, conv_b, *, seq_len, bm=1024, bn=512):
    m, k = x.shape
    f = w_gate.shape[1]
    batch = m // seq_len
    bm = min(bm, m)
    assert f % bn == 0 and m % bm == 0 and (seq_len % bm == 0 or bm % seq_len == 0)
    nf = f // bn
    tiles_per_seq = max(seq_len // bm, 1)
    seq_rows = min(seq_len, bm)
    nseq = bm // seq_rows
    seq_blk = lambda i, j: (i // tiles_per_seq, 0, j)
    act, tails = pl.pallas_call(
        functools.partial(_ffn_in_kernel, tiles_per_seq=tiles_per_seq, seq_rows=seq_rows),
        grid=(m // bm, nf),
        in_specs=[
            pl.BlockSpec((bm, k), lambda i, j: (i, 0)),
            pl.BlockSpec((1, k), lambda i, j: (0, 0)),
            pl.BlockSpec((k, bn), lambda i, j: (0, j)),
            pl.BlockSpec((k, bn), lambda i, j: (0, j)),
            pl.BlockSpec((nseq, SUBLANES, bn), seq_blk),
            pl.BlockSpec((3, bn), lambda i, j: (0, j)),
            pl.BlockSpec((1, bn), lambda i, j: (0, j)),
        ],
        out_specs=[pl.BlockSpec((bm, bn), lambda i, j: (i, j)),
                   pl.BlockSpec((nseq, SUBLANES, bn), lambda i, j: (i, 0, j))],
        out_shape=[jax.ShapeDtypeStruct((m, f), BF16),
                   jax.ShapeDtypeStruct((m // bm * nseq, SUBLANES, f), F32)],
        scratch_shapes=[pltpu.VMEM((bm, k), BF16), pltpu.VMEM((nf, nseq, SUBLANES, bn), F32)],
        compiler_params=pltpu.CompilerParams(
            dimension_semantics=("arbitrary", "arbitrary"), vmem_limit_bytes=VMEM_LIMIT_BYTES),
        name="ffn_in",
    )(x, norm_g.reshape(1, k).astype(F32), w_gate, w_up, buf8, conv_w.astype(F32),
      conv_b.reshape(1, f).astype(F32))
    return act, tails.reshape(batch, tiles_per_seq, SUBLANES, f)[:, tiles_per_seq - 1]


def _mlstm_kernel(*refs, heads, dk, dv, has_init):
    it = iter(refs)
    z_ref, zg_ref, bias_ref, norm_ref = next(it), next(it), next(it), next(it)
    if has_init:
        c0_ref, n0_ref, m0_ref = next(it), next(it), next(it)
    o_ref, c_ref, n_ref, m_ref = next(it), next(it), next(it), next(it)
    ch = pl.program_id(1)
    L = z_ref.shape[0]

    @pl.when(ch == 0)
    def _():
        if has_init:
            c_ref[...] = c0_ref[...]
            n_ref[...] = n0_ref[...]
            m_ref[...] = m0_ref[...]
        else:
            c_ref[...] = jnp.zeros_like(c_ref)
            n_ref[...] = jnp.zeros_like(n_ref)
            m_ref[...] = jnp.zeros_like(m_ref)

    g = zg_ref[...] + bias_ref[...]
    lane = lax.broadcasted_iota(jnp.int32, g.shape, 1)
    u = jnp.where(lane < heads, g, _cumsum_rows(_log_sigmoid(g)))
    ut = _transpose_rows(u)
    tri, _ = _tri_masks(L)
    m_all = m_ref[0]
    m_lane = lax.broadcasted_iota(jnp.int32, m_all.shape, 1)
    qoff, koff, voff, ooff = 0, heads * dk, 2 * heads * dk, 2 * heads * dk + heads * dv
    hs = range(heads)
    ig_col = [u[:, h:h + 1] for h in hs]
    ig_row = [ut[h:h + 1, :] for h in hs]
    b_col = [u[:, heads + h:heads + h + 1] for h in hs]
    b_row = [ut[heads + h:heads + h + 1, :] for h in hs]
    m_prev = [m_all[:, h:h + 1] for h in hs]
    q = [(z_ref[:, qoff + h * dk:qoff + (h + 1) * dk].astype(F32) * dk ** -0.5).astype(BF16) for h in hs]
    k = [z_ref[:, koff + h * dk:koff + (h + 1) * dk] for h in hs]
    v = [z_ref[:, voff + h * dv:voff + (h + 1) * dv] for h in hs]
    c_old = [c_ref[0, h] for h in hs]
    n_old = [n_ref[0, h:h + 1, :] for h in hs]
    d = [jnp.where(tri, b_col[h] - b_row[h] + ig_row[h], -jnp.inf) for h in hs]
    li = [b_col[h] + m_prev[h] for h in hs]
    m_t = [jnp.maximum(li[h], jnp.max(d[h], axis=1, keepdims=True)) for h in hs]
    w_inter = [jnp.exp(li[h] - m_t[h]) for h in hs]
    a = [_dot_nt(q[h], k[h]) * jnp.exp(d[h] - m_t[h]) for h in hs]
    num = [_dot(a[h].astype(BF16), v[h]) + w_inter[h] * _dot(q[h], c_old[h].astype(BF16)) for h in hs]
    den = [jnp.sum(a[h], axis=1, keepdims=True)
           + w_inter[h] * jnp.sum(q[h].astype(F32) * n_old[h], axis=1, keepdims=True) for h in hs]
    hh = [num[h] / jnp.maximum(jnp.abs(den[h]), jnp.exp(-m_t[h])) for h in hs]
    for h in hs:
        m_new = m_t[h][L - 1:L, :]
        b_last = b_col[h][L - 1:L, :]
        w_k = jnp.exp(b_last - b_col[h] + ig_col[h] - m_new)
        dec = jnp.exp(b_last + m_prev[h] - m_new)
        kw = k[h].astype(F32) * w_k
        c_ref[0, h] = dec * c_old[h] + _dot_tn(kw.astype(BF16), v[h])
        n_ref[0, h:h + 1, :] = dec * n_old[h] + jnp.sum(kw, axis=0, keepdims=True)
        m_all = jnp.where(m_lane == h, m_new, m_all)
    for h in hs:
        og = z_ref[:, ooff + h * dv:ooff + (h + 1) * dv].astype(F32)
        y = _sigmoid(og) * _rms(hh[h], norm_ref[...])
        o_ref[:, h * dv:(h + 1) * dv] = y.astype(o_ref.dtype)
    m_ref[0] = m_all


def _mlstm(z, zg, bias, norm, init, *, batch, seq_len, heads, dk, dv):
    L = min(MLSTM_CHUNK, seq_len)
    nc = seq_len // L
    has_init = init is not None
    row = lambda b, c: (b * nc + c, 0)
    in_specs = [pl.BlockSpec((L, z.shape[1]), row),
                pl.BlockSpec((L, LANES), row),
                pl.BlockSpec((1, LANES), lambda b, c: (0, 0)),
                pl.BlockSpec((1, dv), lambda b, c: (0, 0))]
    args = [z, zg, bias, norm.reshape(1, dv).astype(F32)]
    state_specs = [pl.BlockSpec((1, heads, dk, dv), lambda b, c: (b, 0, 0, 0)),
                   pl.BlockSpec((1, heads, dk), lambda b, c: (b, 0, 0)),
                   pl.BlockSpec((1, 1, heads), lambda b, c: (b, 0, 0))]
    if has_init:
        in_specs += state_specs
        args += [init[0], init[1], init[2].reshape(batch, 1, heads)]
    o, c_new, n_new, m_new = pl.pallas_call(
        functools.partial(_mlstm_kernel, heads=heads, dk=dk, dv=dv, has_init=has_init),
        grid=(batch, nc),
        in_specs=in_specs,
        out_specs=[pl.BlockSpec((L, heads * dv), row)] + state_specs,
        out_shape=[jax.ShapeDtypeStruct((batch * seq_len, heads * dv), BF16),
                   jax.ShapeDtypeStruct((batch, heads, dk, dv), F32),
                   jax.ShapeDtypeStruct((batch, heads, dk), F32),
                   jax.ShapeDtypeStruct((batch, 1, heads), F32)],
        compiler_params=pltpu.CompilerParams(
            dimension_semantics=("arbitrary", "arbitrary"), vmem_limit_bytes=VMEM_LIMIT_BYTES),
        name="mlstm",
    )(*args)
    return o, (c_new, n_new, m_new.reshape(batch, heads))


def _ret_kernel(*refs, heads, dk, dv, has_init):
    it = iter(refs)
    cos_ref, sin_ref, z_ref, norm_ref = next(it), next(it), next(it), next(it)
    s0_ref = next(it) if has_init else None
    o_ref, s_ref = next(it), next(it)
    ch = pl.program_id(1)
    L = z_ref.shape[0]
    half = dk // 2

    @pl.when(ch == 0)
    def _():
        s_ref[...] = s0_ref[...] if has_init else jnp.zeros_like(s_ref)

    cos, sin = cos_ref[...], sin_ref[...]
    t_col = lax.broadcasted_iota(jnp.int32, (L, 1), 0).astype(F32)
    rel = (lax.broadcasted_iota(jnp.int32, (L, L), 0) - lax.broadcasted_iota(jnp.int32, (L, L), 1)).astype(F32)
    koff, voff, goff = heads * dk, 2 * heads * dk, 2 * heads * dk + heads * dv

    def rope(x):
        x1, x2 = x[:, :half], x[:, half:]
        return jnp.concatenate([x1 * cos - x2 * sin, x1 * sin + x2 * cos], axis=-1)

    for h in range(heads):
        lg = math.log1p(-2.0 ** (-5.0 - h))
        q = rope(z_ref[:, h * dk:(h + 1) * dk].astype(F32)) * dk ** -0.5
        k = rope(z_ref[:, koff + h * dk:koff + (h + 1) * dk].astype(F32))
        v = z_ref[:, voff + h * dv:voff + (h + 1) * dv]
        gate = z_ref[:, goff + h * dv:goff + (h + 1) * dv].astype(F32)
        qb = q.astype(BF16)
        dec = jnp.where(rel >= 0, jnp.exp(jnp.maximum(rel, 0.0) * lg), 0.0)
        a = _dot_nt(qb, k.astype(BF16)) * dec
        s_old = s_ref[0, h]
        o = _dot(a.astype(BF16), v) + jnp.exp((t_col + 1.0) * lg) * _dot(qb, s_old.astype(BF16))
        kd = k * jnp.exp((L - 1.0 - t_col) * lg)
        s_ref[0, h] = math.exp(L * lg) * s_old + _dot_tn(kd.astype(BF16), v)
        y = _silu(gate) * _rms(o, norm_ref[...])
        o_ref[:, h * dv:(h + 1) * dv] = y.astype(o_ref.dtype)


def _retention(z, norm, init, pos0, *, batch, seq_len, heads, dk, dv):
    L = min(RET_CHUNK, seq_len)
    nc = seq_len // L
    has_init = init is not None
    half = dk // 2
    inv = ROPE_BASE ** (-jnp.arange(half, dtype=F32) / half)
    ang = (pos0 + jnp.arange(seq_len)).astype(F32)[:, None] * inv[None]
    row = lambda b, c: (b * nc + c, 0)
    in_specs = [pl.BlockSpec((L, half), lambda b, c: (c, 0)),
                pl.BlockSpec((L, half), lambda b, c: (c, 0)),
                pl.BlockSpec((L, z.shape[1]), row),
                pl.BlockSpec((1, dv), lambda b, c: (0, 0))]
    args = [jnp.cos(ang), jnp.sin(ang), z, norm.reshape(1, dv).astype(F32)]
    state_spec = pl.BlockSpec((1, heads, dk, dv), lambda b, c: (b, 0, 0, 0))
    if has_init:
        in_specs.append(state_spec)
        args.append(init)
    o, s_new = pl.pallas_call(
        functools.partial(_ret_kernel, heads=heads, dk=dk, dv=dv, has_init=has_init),
        grid=(batch, nc),
        in_specs=in_specs,
        out_specs=[pl.BlockSpec((L, heads * dv), row), state_spec],
        out_shape=[jax.ShapeDtypeStruct((batch * seq_len, heads * dv), BF16),
                   jax.ShapeDtypeStruct((batch, heads, dk, dv), F32)],
        compiler_params=pltpu.CompilerParams(
            dimension_semantics=("arbitrary", "arbitrary"), vmem_limit_bytes=VMEM_LIMIT_BYTES),
        name="retention",
    )(*args)
    return o, s_new


def _gla_kernel(*refs, heads, dk, dv, has_init):
    it = iter(refs)
    z_ref, zlr_ref, w2_ref, bg_ref, norm_ref = next(it), next(it), next(it), next(it), next(it)
    s0_ref = next(it) if has_init else None
    o_ref, s_ref, st_scr = next(it), next(it), next(it)
    ch = pl.program_id(1)
    nch = pl.num_programs(1)
    L = z_ref.shape[0]

    @pl.when(ch == 0)
    def _():
        for h in range(heads):
            st_scr[h] = s0_ref[0, h].T if has_init else jnp.zeros((dv, dk), F32)

    la = _log_sigmoid(_dot(zlr_ref[...].astype(BF16), w2_ref[...]) + bg_ref[...]) / GLA_TAU
    b = _cumsum_rows(la)
    sub = min(GLA_SUB, L)
    tri, _ = _tri_masks(sub)
    koff, voff, roff = heads * dk, 2 * heads * dk, 2 * heads * dk + heads * dv
    for h in range(heads):
        bh = b[:, h * dk:(h + 1) * dk]
        bl = bh[L - 1:L, :]
        q = z_ref[:, h * dk:(h + 1) * dk].astype(F32) * dk ** -0.5
        k = z_ref[:, koff + h * dk:koff + (h + 1) * dk].astype(F32)
        v = z_ref[:, voff + h * dv:voff + (h + 1) * dv]
        r = z_ref[:, roff + h * dv:roff + (h + 1) * dv].astype(F32)
        st_old = st_scr[h]
        sb = st_old.astype(BF16)
        parts = []
        for c in range(L // sub):
            rows = slice(c * sub, (c + 1) * sub)
            ref = bh[c * sub - 1:c * sub, :] if c else jnp.zeros_like(bl)
            qd = (q[rows] * jnp.exp(bh[rows] - ref)).astype(BF16)
            kd = (k[rows] * jnp.exp(ref - bh[rows])).astype(BF16)
            o = _dot(jnp.where(tri, _dot_nt(qd, kd), 0.0).astype(BF16), v[rows])
            for c2 in range(c):
                rows2 = slice(c2 * sub, (c2 + 1) * sub)
                ke = (k[rows2] * jnp.exp(ref - bh[rows2])).astype(BF16)
                o = o + _dot(_dot_nt(qd, ke).astype(BF16), v[rows2])
            o = o + _dot_nt((q[rows] * jnp.exp(bh[rows])).astype(BF16), sb)
            parts.append(o)
        o = parts[0] if len(parts) == 1 else jnp.concatenate(parts, axis=0)
        ke = (k * jnp.exp(bl - bh)).astype(BF16)
        st_scr[h] = jnp.exp(bl) * st_old + _dot_tn(v, ke)
        y = _silu(r) * _rms(o, norm_ref[...])
        o_ref[:, h * dv:(h + 1) * dv] = y.astype(o_ref.dtype)

    @pl.when(ch == nch - 1)
    def _():
        for h in range(heads):
            s_ref[0, h] = st_scr[h].T


def _gla(z, zlr, w_gate2, b_gate, norm, init, *, batch, seq_len, heads, dk, dv):
    L = min(GLA_CHUNK, seq_len)
    nc = seq_len // L
    has_init = init is not None
    hk = heads * dk
    w2 = jnp.zeros((LANES, hk), BF16).at[:GLA_RANK].set(w_gate2.astype(BF16))
    row = lambda b, c: (b * nc + c, 0)
    in_specs = [pl.BlockSpec((L, z.shape[1]), row),
                pl.BlockSpec((L, LANES), row),
                pl.BlockSpec((LANES, hk), lambda b, c: (0, 0)),
                pl.BlockSpec((1, hk), lambda b, c: (0, 0)),
                pl.BlockSpec((1, dv), lambda b, c: (0, 0))]
    args = [z, zlr, w2, b_gate.reshape(1, hk).astype(F32), norm.reshape(1, dv).astype(F32)]
    state_spec = pl.BlockSpec((1, heads, dk, dv), lambda b, c: (b, 0, 0, 0))
    if has_init:
        in_specs.append(state_spec)
        args.append(init)
    o, s_new = pl.pallas_call(
        functools.partial(_gla_kernel, heads=heads, dk=dk, dv=dv, has_init=has_init),
        grid=(batch, nc),
        in_specs=in_specs,
        out_specs=[pl.BlockSpec((L, heads * dv), row), state_spec],
        out_shape=[jax.ShapeDtypeStruct((batch * seq_len, heads * dv), BF16),
                   jax.ShapeDtypeStruct((batch, heads, dk, dv), F32)],
        scratch_shapes=[pltpu.VMEM((heads, dv, dk), F32)],
        compiler_params=pltpu.CompilerParams(
            dimension_semantics=("arbitrary", "arbitrary"), vmem_limit_bytes=VMEM_LIMIT_BYTES),
        name="gla",
    )(*args)
    return o, s_new


def _unit_lower_inverses(mats):
    n = mats[0].shape[0]
    r = lax.broadcasted_iota(jnp.int32, (n, n), 0)
    c = lax.broadcasted_iota(jnp.int32, (n, n), 1)
    eye = jnp.where(r == c, 1.0, 0.0)
    ps = [eye - a for a in mats]
    pws = [_dot(a.astype(BF16), a.astype(BF16)) for a in mats]
    covered = 2
    while covered < n:
        if 2 * covered < n:
            both = [_dot(jnp.concatenate([p, pw], axis=0).astype(BF16), pw.astype(BF16)) for p, pw in zip(ps, pws)]
            ps = [p + b[:n] for p, b in zip(ps, both)]
            pws = [b[n:] for b in both]
        else:
            ps = [p + _dot(p.astype(BF16), pw.astype(BF16)) for p, pw in zip(ps, pws)]
        covered *= 2
    return ps


def _gdn_kernel(*refs, qk_heads, v_heads, d, has_init):
    it = iter(refs)
    z_ref, zba_ref, cw_ref, aneg_ref, dtb_ref, norm_ref = (next(it) for _ in range(6))
    if has_init:
        s0_ref, buf0_ref = next(it), next(it)
    o_ref, s_ref = next(it), next(it)
    xs_scr, qkv_scr, gc_scr, gr_scr, bc_scr = (next(it) for _ in range(5))
    ch = pl.program_id(1)
    L = z_ref.shape[0]
    rep = v_heads // qk_heads
    conv_ch = (2 * qk_heads + v_heads) * d
    width = cw_ref.shape[0]
    hist = GDN_CONV_HISTORY

    @pl.when(ch == 0)
    def _():
        s_ref[...] = s0_ref[...] if has_init else jnp.zeros_like(s_ref)
        prev = jnp.zeros((hist, conv_ch), F32)
        if has_init:
            prev = jnp.concatenate([prev[:hist - width + 1], buf0_ref[0]], axis=0)
        hi = prev.astype(BF16)
        xs_scr[L:L + hist, :] = hi
        xs_scr[L + hist:L + 2 * hist, :] = (prev - hi.astype(F32)).astype(BF16)

    ba = zba_ref[...]
    lane = lax.broadcasted_iota(jnp.int32, ba.shape, 1)
    beta = _sigmoid(ba)
    la = aneg_ref[...] * _softplus(ba + dtb_ref[...])
    g = _cumsum_rows(jnp.where(lane >= v_heads, la, 0.0))
    gt = _transpose_rows(g)
    for j in range(v_heads):
        gc_scr[j] = jnp.broadcast_to(g[:, v_heads + j:v_heads + j + 1], (L, LANES))
        bc_scr[j] = jnp.broadcast_to(beta[:, j:j + 1], (L, LANES))
    for a in range(qk_heads):
        pair_row = jnp.concatenate([gt[v_heads + rep * a + r:v_heads + rep * a + r + 1, :] for r in range(rep)], axis=-1)
        gr_scr[a] = jnp.broadcast_to(pair_row, (SUBLANES, rep * L))

    rows = lax.broadcasted_iota(jnp.int32, ((width - 1) * L, L + 2 * hist), 0)
    col = lax.broadcasted_iota(jnp.int32, ((width - 1) * L, L + 2 * hist), 1)
    src = rows % L - (rows // L + 1)
    hi_col = jnp.where(src >= 0, src, src + L + hist)
    lo_col = jnp.where(src >= 0, -1, src + L + 2 * hist)
    shift = jnp.where((col == hi_col) | (col == lo_col), 1.0, 0.0).astype(BF16)
    cb = GDN_CONV_BLOCK
    for blk in range(conv_ch // cb):
        sl = slice(blk * cb, (blk + 1) * cb)
        cur = z_ref[:, sl]
        xs_scr[0:L, sl] = cur
        sh = _dot(shift, xs_scr[:, sl])
        w = cw_ref[:, sl]
        acc = cur.astype(F32) * w[width - 1:width]
        for t in range(1, width):
            acc = acc + sh[(t - 1) * L:t * L] * w[width - 1 - t:width - t]
        xs_scr[L:L + hist, sl] = z_ref[L - hist:L, sl]
        xs_scr[L + hist:L + 2 * hist, sl] = jnp.zeros((hist, cb), BF16)
        x = _silu(acc)
        for c in range(cb // d):
            head = blk * (cb // d) + c
            xh = x[:, c * d:(c + 1) * d]
            if head < 2 * qk_heads:
                xh = xh * lax.rsqrt(jnp.sum(xh * xh, axis=-1, keepdims=True) + EPS)
                if head < qk_heads:
                    xh = xh * d ** -0.5
            qkv_scr[:, head * d:(head + 1) * d] = xh

    assert rep == 2
    lane2 = lax.broadcasted_iota(jnp.int32, (L, 2 * L), 1)
    row2 = lax.broadcasted_iota(jnp.int32, (L, 2 * L), 0)
    left = lane2 < L
    tcol = jnp.where(left, lane2, lane2 - L)
    causal, strict = row2 >= tcol, row2 > tcol
    eye = jnp.where(row2 == tcol, 1.0, 0.0)

    def blockdiag(x):
        return jnp.concatenate([jnp.where(left, x, 0.0), jnp.where(left, 0.0, x)], axis=0).astype(BF16)

    def pair_inverses(mats):
        ps = [eye - a for a in mats]
        pws = [_dot(a.astype(BF16), blockdiag(a)) for a in mats]
        covered = 2
        while covered < L:
            if 2 * covered < L:
                both = [_dot(jnp.concatenate([p, pw], axis=0).astype(BF16), blockdiag(pw)) for p, pw in zip(ps, pws)]
                ps = [p + b[:L] for p, b in zip(ps, both)]
                pws = [b[L:] for b in both]
            else:
                ps = [p + _dot(p.astype(BF16), blockdiag(pw)) for p, pw in zip(ps, pws)]
            covered *= 2
        return ps

    gv = GDN_V_HEADS_PER_STEP
    gk = gv // rep

    def head_group(grp):
        def cols(base):
            return slice(base * d, (base + 1) * d)
        pairs = range(gk)
        heads = range(gv)
        js = [grp * gv + r for r in heads]
        k = lambda a: qkv_scr[:, cols(qk_heads + grp * gk + a)]
        g_col = lambda r: gc_scr[js[r]]
        b_col = lambda r: bc_scr[js[r]]
        qb = [qkv_scr[:, cols(grp * gk + a)].astype(BF16) for a in pairs]
        kb = [k(a).astype(BF16) for a in pairs]
        kb2 = [jnp.concatenate([x, x], axis=0) for x in kb]
        kk = [_dot_nt(kb[a], kb2[a]) for a in pairs]
        qk = [_dot_nt(qb[a], kb2[a]) for a in pairs]
        g_pair = [jnp.where(left, g_col(2 * a)[:, :2 * L], g_col(2 * a + 1)[:, :2 * L]) for a in pairs]
        b_pair = [jnp.where(left, b_col(2 * a)[:, :2 * L], b_col(2 * a + 1)[:, :2 * L]) for a in pairs]
        g_row = [gr_scr[grp * gk + a][0:1, :] for a in pairs]
        dmat = [jnp.exp(jnp.where(causal, g_pair[a] - g_row[a], -jnp.inf)) for a in pairs]
        tinv = pair_inverses([jnp.where(strict, b_pair[a] * dmat[a] * kk[a], 0.0) for a in pairs])
        rhs = [jnp.concatenate([b_col(r) * qkv_scr[:, cols(2 * qk_heads + js[r])],
                                b_col(r) * jnp.exp(g_col(r)) * k(r // rep)], axis=-1).astype(BF16) for r in heads]
        sol = [_dot(blockdiag(tinv[a]), jnp.concatenate([rhs[2 * a], rhs[2 * a + 1]], axis=0)) for a in pairs]
        sol = [sol[r // rep][(r % rep) * L:(r % rep + 1) * L] for r in heads]
        ss = [_dot(jnp.concatenate([sol[r][:, d:].astype(BF16), qb[r // rep]], axis=0),
                   s_ref[0, js[r]].astype(BF16)) for r in heads]
        ub = [(sol[r][:, :d] - ss[r][:L]).astype(BF16) for r in heads]
        ou = [_dot(blockdiag(qk[a] * dmat[a]), jnp.concatenate([ub[2 * a], ub[2 * a + 1]], axis=0)) for a in pairs]
        for r in heads:
            o = jnp.exp(g_col(r)) * ss[r][L:] + ou[r // rep][(r % rep) * L:(r % rep + 1) * L]
            zg = z_ref[:, cols(conv_ch // d + js[r])].astype(F32)
            y = _rms(o, norm_ref[...]) * _silu(zg)
            o_ref[:, cols(js[r])] = y.astype(o_ref.dtype)
        for r in heads:
            g_last = g_col(r)[L - 1:L, :]
            kd = (jnp.exp(g_last - g_col(r)) * k(r // rep)).astype(BF16)
            s_ref[0, js[r]] = jnp.exp(g_last) * s_ref[0, js[r]] + _dot_tn(kd, ub[r])

    assert v_heads % gv == 0 and gv % rep == 0
    for grp in range(v_heads // gv):
        head_group(grp)


def _gdn(z, zba, conv_w, a_log, dt_bias, norm, init, *, batch, seq_len, qk_heads, v_heads, d):
    L = min(CHUNK, seq_len)
    nc = seq_len // L
    has_init = init is not None
    conv_ch = (2 * qk_heads + v_heads) * d
    width = conv_w.shape[0]
    lane_pad = lambda x: jnp.zeros((1, LANES), F32).at[0, v_heads:2 * v_heads].set(x.astype(F32))
    row = lambda b, c: (b * nc + c, 0)
    const = lambda b, c: (0, 0)
    in_specs = [pl.BlockSpec((L, z.shape[1]), row),
                pl.BlockSpec((L, LANES), row),
                pl.BlockSpec((width, conv_ch), const),
                pl.BlockSpec((1, LANES), const),
                pl.BlockSpec((1, LANES), const),
                pl.BlockSpec((1, d), const)]
    args = [z, zba, conv_w.astype(F32), lane_pad(-jnp.exp(a_log)), lane_pad(dt_bias),
            norm.reshape(1, d).astype(F32)]
    state_spec = pl.BlockSpec((1, v_heads, d, d), lambda b, c: (b, 0, 0, 0))
    if has_init:
        in_specs += [state_spec, pl.BlockSpec((1, width - 1, conv_ch), lambda b, c: (b, 0, 0))]
        args += [init[0], init[1]]
    o, s_new = pl.pallas_call(
        functools.partial(_gdn_kernel, qk_heads=qk_heads, v_heads=v_heads, d=d, has_init=has_init),
        grid=(batch, nc),
        in_specs=in_specs,
        out_specs=[pl.BlockSpec((L, v_heads * d), row), state_spec],
        out_shape=[jax.ShapeDtypeStruct((batch * seq_len, v_heads * d), BF16),
                   jax.ShapeDtypeStruct((batch, v_heads, d, d), F32)],
        scratch_shapes=[pltpu.VMEM((L + 2 * GDN_CONV_HISTORY, conv_ch), BF16),
                        pltpu.VMEM((L, conv_ch), F32),
                        pltpu.VMEM((v_heads, L, LANES), F32),
                        pltpu.VMEM((qk_heads, SUBLANES, v_heads // qk_heads * L), F32),
                        pltpu.VMEM((v_heads, L, LANES), F32)],
        compiler_params=pltpu.CompilerParams(
            dimension_semantics=("arbitrary", "arbitrary"), vmem_limit_bytes=VMEM_LIMIT_BYTES),
        name="gdn",
    )(*args)
    return o, s_new


def _pad_cols(w, n):
    return jnp.zeros((w.shape[0], n), F32).at[:, :w.shape[1]].set(w.astype(F32))


def _trunk(x, states, pos0, p, *, batch, seq_len):
    m, dm = x.shape
    has_init = states is not None
    if has_init:
        mC, mn, mm, rS, gS, dS, dconv, fconv = states
    depth = p['norm_mix'].shape[0]
    outs = [[] for _ in range(8)]
    for i in range(depth):
        kind, j = i % 4, i // 4
        g_mix = p['norm_mix'][i]
        if kind == 0:
            heads, dk, dv = p['mlstm_n_shape']
            w_in = p['mlstm_w_in'][j]
            nm = 2 * heads * dk + 2 * heads * dv
            z, zg = _matmul(x, w_in[:, :nm].astype(BF16), norm_g=g_mix, gate_w=_pad_cols(w_in[:, nm:], LANES),
                            name="mlstm_in")
            bias = _pad_cols(p['mlstm_b_gates'][j][None], LANES)
            init = (mC[j], mn[j], mm[j]) if has_init else None
            a, (c_, n_, m_) = _mlstm(z, zg, bias, p['mlstm_norm'][j], init,
                                     batch=batch, seq_len=seq_len, heads=heads, dk=dk, dv=dv)
            outs[0].append(c_); outs[1].append(n_); outs[2].append(m_)
            w_out = p['mlstm_w_out'][j]
        elif kind == 1:
            heads, dk, dv = p['ret_shape']
            z = _matmul(x, p['ret_w_in'][j].astype(BF16), norm_g=g_mix, name="ret_in")
            a, s_ = _retention(z, p['ret_norm'][j], rS[j] if has_init else None, pos0,
                               batch=batch, seq_len=seq_len, heads=heads, dk=dk, dv=dv)
            outs[3].append(s_)
            w_out = p['ret_w_out'][j]
        elif kind == 2:
            heads, dk, dv = p['gla_shape']
            w_in = p['gla_w_in'][j]
            nm = 2 * heads * dk + 2 * heads * dv
            z, zlr = _matmul(x, w_in[:, :nm].astype(BF16), norm_g=g_mix, gate_w=_pad_cols(w_in[:, nm:], LANES),
                             name="gla_in")
            a, s_ = _gla(z, zlr, p['gla_w_gate2'][j], p['gla_b_gate'][j], p['gla_norm'][j],
                         gS[j] if has_init else None,
                         batch=batch, seq_len=seq_len, heads=heads, dk=dk, dv=dv)
            outs[4].append(s_)
            w_out = p['gla_w_out'][j]
        else:
            qk_heads, v_heads, d = p['gdn_shape']
            w_in = p['gdn_w_in'][j]
            conv_ch = (2 * qk_heads + v_heads) * d
            nm = conv_ch + v_heads * d
            width = p['gdn_conv_w'].shape[1]
            z, zba = _matmul(x, w_in[:, :nm].astype(BF16), norm_g=g_mix, gate_w=_pad_cols(w_in[:, nm:], LANES),
                             name="gdn_in")
            init = (dS[j], dconv[j]) if has_init else None
            a, s_ = _gdn(z, zba, p['gdn_conv_w'][j], p['gdn_A_log'][j], p['gdn_dt_bias'][j], p['gdn_norm'][j],
                         init, batch=batch, seq_len=seq_len, qk_heads=qk_heads, v_heads=v_heads, d=d)
            outs[5].append(s_)
            outs[6].append(z.reshape(batch, seq_len, nm)[:, seq_len - (width - 1):, :conv_ch].astype(F32))
            w_out = p['gdn_w_out'][j]
        x = _matmul(a, w_out.astype(BF16), residual=x, out_dtype=F32, name="mix_out")

        f = p['ffn_w_gate'].shape[2]
        fw = p['ffn_conv_w'].shape[1]
        buf8 = jnp.zeros((batch, SUBLANES, f), F32)
        if has_init:
            buf8 = buf8.at[:, SUBLANES - (fw - 1):].set(fconv[i].astype(F32))
        act, tail = _ffn_in(x, p['norm_ffn'][i], p['ffn_w_gate'][i].astype(BF16), p['ffn_w_up'][i].astype(BF16),
                            buf8, p['ffn_conv_w'][i], p['ffn_conv_b'][i], seq_len=seq_len)
        outs[7].append(tail[:, SUBLANES - (fw - 1):])
        x = _matmul(act, p['ffn_w_down'][i].astype(BF16), residual=x, out_dtype=F32, name="ffn_out")
    y = _final_norm(x, p['norm_final'])
    return y.reshape(batch, seq_len, dm), tuple(jnp.stack(o) for o in outs)


def kernel(x_prompt, x_sample, state_mlstm_C, state_mlstm_n, state_mlstm_m, state_ret_S, state_gla_S, state_gdn_S, state_gdn_conv, state_ffn_conv, norm_mix, norm_ffn, norm_final, mlstm_w_in, mlstm_b_gates, mlstm_norm, mlstm_w_out, ret_w_in, ret_norm, ret_w_out, gla_w_in, gla_w_gate2, gla_b_gate, gla_norm, gla_w_out, gdn_w_in, gdn_conv_w, gdn_A_log, gdn_dt_bias, gdn_norm, gdn_w_out, ffn_w_gate, ffn_w_up, ffn_conv_w, ffn_conv_b, ffn_w_down):
    p = dict(norm_mix=norm_mix, norm_ffn=norm_ffn, norm_final=norm_final,
             mlstm_w_in=mlstm_w_in, mlstm_b_gates=mlstm_b_gates, mlstm_norm=mlstm_norm, mlstm_w_out=mlstm_w_out,
             ret_w_in=ret_w_in, ret_norm=ret_norm, ret_w_out=ret_w_out,
             gla_w_in=gla_w_in, gla_w_gate2=gla_w_gate2, gla_b_gate=gla_b_gate, gla_norm=gla_norm, gla_w_out=gla_w_out,
             gdn_w_in=gdn_w_in, gdn_conv_w=gdn_conv_w, gdn_A_log=gdn_A_log, gdn_dt_bias=gdn_dt_bias,
             gdn_norm=gdn_norm, gdn_w_out=gdn_w_out,
             ffn_w_gate=ffn_w_gate, ffn_w_up=ffn_w_up, ffn_conv_w=ffn_conv_w, ffn_conv_b=ffn_conv_b,
             ffn_w_down=ffn_w_down)
    p['mlstm_n_shape'] = state_mlstm_C.shape[2:]
    p['ret_shape'] = state_ret_S.shape[2:]
    p['gla_shape'] = state_gla_S.shape[2:]
    d = state_gdn_S.shape[-1]
    v_heads = state_gdn_S.shape[2]
    p['gdn_shape'] = ((state_gdn_conv.shape[-1] // d - v_heads) // 2, v_heads, d)

    bp, tp, dm = x_prompt.shape
    bs, ts, _ = x_sample.shape
    y_prompt, p_states = _trunk(x_prompt.reshape(bp * tp, dm), None, 0, p, batch=bp, seq_len=tp)
    cache = (state_mlstm_C, state_mlstm_n, state_mlstm_m, state_ret_S, state_gla_S, state_gdn_S,
             state_gdn_conv, state_ffn_conv)
    y_sample, s_states = _trunk(x_sample.reshape(bs * ts, dm), cache, PAST_LEN, p, batch=bs, seq_len=ts)
    return (y_prompt, y_sample) + p_states + s_states
```

```python
import functools
import math

import jax
import jax.numpy as jnp
from jax import lax
from jax.experimental import pallas as pl
from jax.experimental.pallas import tpu as pltpu

F32 = jnp.float32
BF16 = jnp.bfloat16
EPS = 1e-6
PAST_LEN = 2048
ROPE_BASE = 10000.0
GLA_TAU = 16.0
GLA_RANK = 16
CHUNK = 64
MLSTM_CHUNK = 256
RET_CHUNK = 256
GLA_CHUNK = 128
GLA_SUB = 64
GDN_V_HEADS_PER_STEP = 32
GDN_CONV_HISTORY = 16
GDN_CONV_BLOCK = 256
FFN_ROW_CHUNK = 256
LANES = 128
SUBLANES = 8
VMEM_LIMIT_BYTES = 56 * 1024 * 1024


def _sigmoid(x):
    return 1.0 / (1.0 + jnp.exp(-x))


def _silu(x):
    return x * _sigmoid(x)


def _log1p_exp_neg_abs(x):
    return jnp.log(1.0 + jnp.exp(-jnp.abs(x)))


def _log_sigmoid(x):
    return jnp.minimum(x, 0.0) - _log1p_exp_neg_abs(x)


def _softplus(x):
    return jnp.maximum(x, 0.0) + _log1p_exp_neg_abs(x)


def _rms(x, g):
    return x * lax.rsqrt(jnp.mean(x * x, axis=-1, keepdims=True) + EPS) * g


def _dot(a, b):
    return jnp.dot(a, b, preferred_element_type=F32)


def _dot_nt(a, b):
    return lax.dot_general(a, b, (((1,), (1,)), ((), ())), preferred_element_type=F32)


def _dot_tn(a, b):
    return lax.dot_general(a, b, (((0,), (0,)), ((), ())), preferred_element_type=F32)


def _split3(x):
    hi = x.astype(BF16)
    r = x - hi.astype(F32)
    mid = r.astype(BF16)
    lo = (r - mid.astype(F32)).astype(BF16)
    return hi, mid, lo


def _cumsum_rows(x):
    n = x.shape[0]
    r = lax.broadcasted_iota(jnp.int32, (n, n), 0)
    c = lax.broadcasted_iota(jnp.int32, (n, n), 1)
    tril = jnp.where(r >= c, 1.0, 0.0).astype(BF16)
    hi, mid, lo = _split3(x)
    return _dot(tril, hi) + _dot(tril, mid) + _dot(tril, lo)


def _transpose_rows(x):
    n = x.shape[0]
    if n < LANES:
        x = jnp.concatenate([x, jnp.zeros((LANES - n, x.shape[1]), x.dtype)], axis=0)
    return x.T[:, :n]


def _causal_taps(x3, prev, cw):
    width = cw.shape[0]
    t8 = lax.broadcasted_iota(jnp.int32, (x3.shape[0], SUBLANES, x3.shape[2]), 1)
    acc = x3 * cw[width - 1:width]
    for s in range(1, width):
        sh = pltpu.roll(x3, s, 1)
        head = sh[:, :SUBLANES]
        for e in range(s):
            head = jnp.where(t8 == e, prev[s - 1 - e], head)
        sh = jnp.concatenate([head, sh[:, SUBLANES:]], axis=1)
        acc = acc + sh * cw[width - 1 - s:width - s]
    return acc


def _tri_masks(n):
    r = lax.broadcasted_iota(jnp.int32, (n, n), 0)
    c = lax.broadcasted_iota(jnp.int32, (n, n), 1)
    return r >= c, r > c


def _mm_kernel(*refs, has_norm, has_res, has_gate):
    it = iter(refs)
    x_ref = next(it)
    g_ref = next(it) if has_norm else None
    w_ref = next(it)
    wgate_ref = next(it) if has_gate else None
    r_ref = next(it) if has_res else None
    o_ref = next(it)
    ogate_ref = next(it) if has_gate else None
    h_scr = next(it) if has_norm else None
    if has_norm:
        @pl.when(pl.program_id(1) == 0)
        def _():
            h = _rms(x_ref[...].astype(F32), g_ref[...])
            h_scr[...] = h.astype(BF16)
            if has_gate:
                both = _dot(h.astype(BF16), wgate_ref[...])
                ogate_ref[...] = both[:, :LANES] + both[:, LANES:]
        h = h_scr[...]
    else:
        h = x_ref[...]
    acc = _dot(h, w_ref[...])
    if has_res:
        acc = acc + r_ref[...]
    o_ref[...] = acc.astype(o_ref.dtype)


def _pick(n, cands):
    for c in cands:
        if n % c == 0:
            return c
    raise ValueError(f"no tile for {n}")


def _matmul(x, w, *, norm_g=None, gate_w=None, residual=None, out_dtype=BF16, bm=1024, bn=None, name="mm"):
    m, k = x.shape
    n = w.shape[1]
    bm = min(bm, m)
    if bn is None:
        bn = _pick(n, (1024, 512, 256, 128)) if k <= 2048 else _pick(n, (512, 256, 128))
    assert m % bm == 0 and n % bn == 0
    has_norm, has_res, has_gate = norm_g is not None, residual is not None, gate_w is not None
    assert has_norm or not has_gate
    in_specs = [pl.BlockSpec((bm, k), lambda i, j: (i, 0))]
    args = [x]
    if has_norm:
        in_specs.append(pl.BlockSpec((1, k), lambda i, j: (0, 0)))
        args.append(norm_g.reshape(1, k).astype(F32))
    in_specs.append(pl.BlockSpec((k, bn), lambda i, j: (0, j)))
    args.append(w)
    if has_gate:
        g_hi = gate_w.astype(BF16)
        g_lo = (gate_w - g_hi.astype(F32)).astype(BF16)
        in_specs.append(pl.BlockSpec((k, 2 * LANES), lambda i, j: (0, 0)))
        args.append(jnp.concatenate([g_hi, g_lo], axis=1))
    if has_res:
        in_specs.append(pl.BlockSpec((bm, bn), lambda i, j: (i, j)))
        args.append(residual)
    out_specs = [pl.BlockSpec((bm, bn), lambda i, j: (i, j))]
    out_shape = [jax.ShapeDtypeStruct((m, n), out_dtype)]
    if has_gate:
        out_specs.append(pl.BlockSpec((bm, LANES), lambda i, j: (i, 0)))
        out_shape.append(jax.ShapeDtypeStruct((m, LANES), F32))
    outs = pl.pallas_call(
        functools.partial(_mm_kernel, has_norm=has_norm, has_res=has_res, has_gate=has_gate),
        grid=(m // bm, n // bn),
        in_specs=in_specs,
        out_specs=out_specs,
        out_shape=out_shape,
        scratch_shapes=[pltpu.VMEM((bm, k), BF16)] if has_norm else [],
        compiler_params=pltpu.CompilerParams(
            dimension_semantics=("arbitrary", "arbitrary"), vmem_limit_bytes=VMEM_LIMIT_BYTES),
        name=name,
    )(*args)
    return outs if has_gate else outs[0]


def _final_norm_kernel(x_ref, g_ref, o_ref):
    o_ref[...] = _rms(x_ref[...], g_ref[...])


def _final_norm(x, g, *, bm=512):
    m, k = x.shape
    bm = min(bm, m)
    return pl.pallas_call(
        _final_norm_kernel,
        grid=(m // bm,),
        in_specs=[pl.BlockSpec((bm, k), lambda i: (i, 0)), pl.BlockSpec((1, k), lambda i: (0, 0))],
        out_specs=pl.BlockSpec((bm, k), lambda i: (i, 0)),
        out_shape=jax.ShapeDtypeStruct((m, k), F32),
        compiler_params=pltpu.CompilerParams(
            dimension_semantics=("arbitrary",), vmem_limit_bytes=VMEM_LIMIT_BYTES),
        name="final_norm",
    )(x, g.reshape(1, k).astype(F32))


def _ffn_in_kernel(x_ref, g_ref, wg_ref, wu_ref, buf_ref, cw_ref, cb_ref, act_ref, tail_ref, h_scr, halo_scr,
                   *, tiles_per_seq, seq_rows):
    i, j = pl.program_id(0), pl.program_id(1)
    bm, bn = act_ref.shape

    @pl.when(j == 0)
    def _():
        h_scr[...] = _rms(x_ref[...], g_ref[...]).astype(BF16)

    nseq = bm // seq_rows
    first = (i % tiles_per_seq) == 0
    halo = jnp.where(first, buf_ref[...], halo_scr[j])
    prev2, prev1 = halo[:, 6:7], halo[:, 7:8]
    cw, cb = cw_ref[...], cb_ref[...]
    wg, wu = wg_ref[...], wu_ref[...]
    rc = min(FFN_ROW_CHUNK, bm)
    spc = max(rc // seq_rows, 1)
    tt = rc // spc
    tails = []
    for r in range(bm // rc):
        rows = slice(r * rc, (r + 1) * rc)
        h = h_scr[rows, :]
        g3 = _dot(h, wg).reshape(spc, tt, bn)
        u3 = _dot(h, wu).reshape(spc, tt, bn)
        pv = [prev1, prev2] if nseq == 1 else [prev1[r * spc:(r + 1) * spc], prev2[r * spc:(r + 1) * spc]]
        gc = _causal_taps(g3, pv, cw) + cb
        act_ref[rows, :] = (_silu(gc) * u3).reshape(rc, bn).astype(act_ref.dtype)
        if nseq == 1:
            prev2, prev1 = g3[:, tt - 2:tt - 1], g3[:, tt - 1:tt]
            tails = [g3[:, tt - SUBLANES:]]
        else:
            tails.append(g3[:, tt - SUBLANES:])
    tail = tails[0] if len(tails) == 1 else jnp.concatenate(tails, axis=0)
    halo_scr[j] = tail
    tail_ref[...] = tail


def _ffn_in(x, norm_g, w_gate, w_up, buf8, conv_w, conv_b, *, seq_len, bm=2048, bn=512):
    m, k = x.shape
    f = w_gate.shape[1]
    batch = m // seq_len
    bm = min(bm, m)
    assert f % bn == 0 and m % bm == 0 and (seq_len % bm == 0 or bm % seq_len == 0)
    nf = f // bn
    tiles_per_seq = max(seq_len // bm, 1)
    seq_rows = min(seq_len, bm)
    nseq = bm // seq_rows
    seq_blk = lambda i, j: (i // tiles_per_seq, 0, j)
    act, tails = pl.pallas_call(
        functools.partial(_ffn_in_kernel, tiles_per_seq=tiles_per_seq, seq_rows=seq_rows),
        grid=(m // bm, nf),
        in_specs=[
            pl.BlockSpec((bm, k), lambda i, j: (i, 0)),
            pl.BlockSpec((1, k), lambda i, j: (0, 0)),
            pl.BlockSpec((k, bn), lambda i, j: (0, j)),
            pl.BlockSpec((k, bn), lambda i, j: (0, j)),
            pl.BlockSpec((nseq, SUBLANES, bn), seq_blk),
            pl.BlockSpec((3, bn), lambda i, j: (0, j)),
            pl.BlockSpec((1, bn), lambda i, j: (0, j)),
        ],
        out_specs=[pl.BlockSpec((bm, bn), lambda i, j: (i, j)),
                   pl.BlockSpec((nseq, SUBLANES, bn), lambda i, j: (i, 0, j))],
        out_shape=[jax.ShapeDtypeStruct((m, f), BF16),
                   jax.ShapeDtypeStruct((m // bm * nseq, SUBLANES, f), F32)],
        scratch_shapes=[pltpu.VMEM((bm, k), BF16), pltpu.VMEM((nf, nseq, SUBLANES, bn), F32)],
        compiler_params=pltpu.CompilerParams(
            dimension_semantics=("arbitrary", "arbitrary"), vmem_limit_bytes=VMEM_LIMIT_BYTES),
        name="ffn_in",
    )(x, norm_g.reshape(1, k).astype(F32), w_gate, w_up, buf8, conv_w.astype(F32),
      conv_b.reshape(1, f).astype(F32))
    return act, tails.reshape(batch, tiles_per_seq, SUBLANES, f)[:, tiles_per_seq - 1]


def _mlstm_kernel(*refs, heads, dk, dv, has_init):
    it = iter(refs)
    z_ref, zg_ref, bias_ref, norm_ref = next(it), next(it), next(it), next(it)
    if has_init:
        c0_ref, n0_ref, m0_ref = next(it), next(it), next(it)
    o_ref, c_ref, n_ref, m_ref = next(it), next(it), next(it), next(it)
    ch = pl.program_id(1)
    L = z_ref.shape[0]

    @pl.when(ch == 0)
    def _():
        if has_init:
            c_ref[...] = c0_ref[...]
            n_ref[...] = n0_ref[...]
            m_ref[...] = m0_ref[...]
        else:
            c_ref[...] = jnp.zeros_like(c_ref)
            n_ref[...] = jnp.zeros_like(n_ref)
            m_ref[...] = jnp.zeros_like(m_ref)

    g = zg_ref[...] + bias_ref[...]
    lane = lax.broadcasted_iota(jnp.int32, g.shape, 1)
    u = jnp.where(lane < heads, g, _cumsum_rows(_log_sigmoid(g)))
    ut = _transpose_rows(u)
    tri, _ = _tri_masks(L)
    m_all = m_ref[0]
    m_lane = lax.broadcasted_iota(jnp.int32, m_all.shape, 1)
    qoff, koff, voff, ooff = 0, heads * dk, 2 * heads * dk, 2 * heads * dk + heads * dv
    hs = range(heads)
    ig_col = [u[:, h:h + 1] for h in hs]
    ig_row = [ut[h:h + 1, :] for h in hs]
    b_col = [u[:, heads + h:heads + h + 1] for h in hs]
    b_row = [ut[heads + h:heads + h + 1, :] for h in hs]
    m_prev = [m_all[:, h:h + 1] for h in hs]
    q = [(z_ref[:, qoff + h * dk:qoff + (h + 1) * dk].astype(F32) * dk ** -0.5).astype(BF16) for h in hs]
    k = [z_ref[:, koff + h * dk:koff + (h + 1) * dk] for h in hs]
    v = [z_ref[:, voff + h * dv:voff + (h + 1) * dv] for h in hs]
    c_old = [c_ref[0, h] for h in hs]
    n_old = [n_ref[0, h:h + 1, :] for h in hs]
    d = [jnp.where(tri, b_col[h] - b_row[h] + ig_row[h], -jnp.inf) for h in hs]
    li = [b_col[h] + m_prev[h] for h in hs]
    m_t = [jnp.maximum(li[h], jnp.max(d[h], axis=1, keepdims=True)) for h in hs]
    w_inter = [jnp.exp(li[h] - m_t[h]) for h in hs]
    a = [_dot_nt(q[h], k[h]) * jnp.exp(d[h] - m_t[h]) for h in hs]
    num = [_dot(a[h].astype(BF16), v[h]) + w_inter[h] * _dot(q[h], c_old[h].astype(BF16)) for h in hs]
    den = [jnp.sum(a[h], axis=1, keepdims=True)
           + w_inter[h] * jnp.sum(q[h].astype(F32) * n_old[h], axis=1, keepdims=True) for h in hs]
    hh = [num[h] / jnp.maximum(jnp.abs(den[h]), jnp.exp(-m_t[h])) for h in hs]
    for h in hs:
        m_new = m_t[h][L - 1:L, :]
        b_last = b_col[h][L - 1:L, :]
        w_k = jnp.exp(b_last - b_col[h] + ig_col[h] - m_new)
        dec = jnp.exp(b_last + m_prev[h] - m_new)
        kw = k[h].astype(F32) * w_k
        c_ref[0, h] = dec * c_old[h] + _dot_tn(kw.astype(BF16), v[h])
        n_ref[0, h:h + 1, :] = dec * n_old[h] + jnp.sum(kw, axis=0, keepdims=True)
        m_all = jnp.where(m_lane == h, m_new, m_all)
    for h in hs:
        og = z_ref[:, ooff + h * dv:ooff + (h + 1) * dv].astype(F32)
        y = _sigmoid(og) * _rms(hh[h], norm_ref[...])
        o_ref[:, h * dv:(h + 1) * dv] = y.astype(o_ref.dtype)
    m_ref[0] = m_all


def _mlstm(z, zg, bias, norm, init, layer, *, batch, seq_len, heads, dk, dv):
    L = min(MLSTM_CHUNK, seq_len)
    nc = seq_len // L
    has_init = init is not None
    row = lambda b, c: (b * nc + c, 0)
    in_specs = [pl.BlockSpec((L, z.shape[1]), row),
                pl.BlockSpec((L, LANES), row),
                pl.BlockSpec((1, LANES), lambda b, c: (0, 0)),
                pl.BlockSpec((1, dv), lambda b, c: (0, 0))]
    args = [z, zg, bias, norm.reshape(1, dv).astype(F32)]
    state_specs = [pl.BlockSpec((1, heads, dk, dv), lambda b, c: (b, 0, 0, 0)),
                   pl.BlockSpec((1, heads, dk), lambda b, c: (b, 0, 0)),
                   pl.BlockSpec((1, 1, heads), lambda b, c: (b, 0, 0))]
    if has_init:
        in_specs += [pl.BlockSpec((None, 1, heads, dk, dv), lambda b, c: (layer, b, 0, 0, 0)),
                     pl.BlockSpec((None, 1, heads, dk), lambda b, c: (layer, b, 0, 0)),
                     state_specs[2]]
        args += [init[0], init[1], init[2][layer].reshape(batch, 1, heads)]
    o, c_new, n_new, m_new = pl.pallas_call(
        functools.partial(_mlstm_kernel, heads=heads, dk=dk, dv=dv, has_init=has_init),
        grid=(batch, nc),
        in_specs=in_specs,
        out_specs=[pl.BlockSpec((L, heads * dv), row)] + state_specs,
        out_shape=[jax.ShapeDtypeStruct((batch * seq_len, heads * dv), BF16),
                   jax.ShapeDtypeStruct((batch, heads, dk, dv), F32),
                   jax.ShapeDtypeStruct((batch, heads, dk), F32),
                   jax.ShapeDtypeStruct((batch, 1, heads), F32)],
        compiler_params=pltpu.CompilerParams(
            dimension_semantics=("arbitrary", "arbitrary"), vmem_limit_bytes=VMEM_LIMIT_BYTES),
        name="mlstm",
    )(*args)
    return o, (c_new, n_new, m_new.reshape(batch, heads))


def _ret_kernel(*refs, heads, dk, dv, has_init):
    it = iter(refs)
    cos_ref, sin_ref, z_ref, norm_ref = next(it), next(it), next(it), next(it)
    s0_ref = next(it) if has_init else None
    o_ref, s_ref = next(it), next(it)
    ch = pl.program_id(1)
    L = z_ref.shape[0]
    half = dk // 2

    @pl.when(ch == 0)
    def _():
        s_ref[...] = s0_ref[...] if has_init else jnp.zeros_like(s_ref)

    cos, sin = cos_ref[...], sin_ref[...]
    t_col = lax.broadcasted_iota(jnp.int32, (L, 1), 0).astype(F32)
    rel = (lax.broadcasted_iota(jnp.int32, (L, L), 0) - lax.broadcasted_iota(jnp.int32, (L, L), 1)).astype(F32)
    koff, voff, goff = heads * dk, 2 * heads * dk, 2 * heads * dk + heads * dv

    def rope(x):
        x1, x2 = x[:, :half], x[:, half:]
        return jnp.concatenate([x1 * cos - x2 * sin, x1 * sin + x2 * cos], axis=-1)

    for h in range(heads):
        lg = math.log1p(-2.0 ** (-5.0 - h))
        q = rope(z_ref[:, h * dk:(h + 1) * dk].astype(F32)) * dk ** -0.5
        k = rope(z_ref[:, koff + h * dk:koff + (h + 1) * dk].astype(F32))
        v = z_ref[:, voff + h * dv:voff + (h + 1) * dv]
        gate = z_ref[:, goff + h * dv:goff + (h + 1) * dv].astype(F32)
        qb = q.astype(BF16)
        dec = jnp.where(rel >= 0, jnp.exp(jnp.maximum(rel, 0.0) * lg), 0.0)
        a = _dot_nt(qb, k.astype(BF16)) * dec
        s_old = s_ref[0, h]
        o = _dot(a.astype(BF16), v) + jnp.exp((t_col + 1.0) * lg) * _dot(qb, s_old.astype(BF16))
        kd = k * jnp.exp((L - 1.0 - t_col) * lg)
        s_ref[0, h] = math.exp(L * lg) * s_old + _dot_tn(kd.astype(BF16), v)
        y = _silu(gate) * _rms(o, norm_ref[...])
        o_ref[:, h * dv:(h + 1) * dv] = y.astype(o_ref.dtype)


def _retention(z, norm, init, layer, pos0, *, batch, seq_len, heads, dk, dv):
    L = min(RET_CHUNK, seq_len)
    nc = seq_len // L
    has_init = init is not None
    half = dk // 2
    inv = ROPE_BASE ** (-jnp.arange(half, dtype=F32) / half)
    ang = (pos0 + jnp.arange(seq_len)).astype(F32)[:, None] * inv[None]
    row = lambda b, c: (b * nc + c, 0)
    in_specs = [pl.BlockSpec((L, half), lambda b, c: (c, 0)),
                pl.BlockSpec((L, half), lambda b, c: (c, 0)),
                pl.BlockSpec((L, z.shape[1]), row),
                pl.BlockSpec((1, dv), lambda b, c: (0, 0))]
    args = [jnp.cos(ang), jnp.sin(ang), z, norm.reshape(1, dv).astype(F32)]
    state_spec = pl.BlockSpec((1, heads, dk, dv), lambda b, c: (b, 0, 0, 0))
    if has_init:
        in_specs.append(pl.BlockSpec((None, 1, heads, dk, dv), lambda b, c: (layer, b, 0, 0, 0)))
        args.append(init)
    o, s_new = pl.pallas_call(
        functools.partial(_ret_kernel, heads=heads, dk=dk, dv=dv, has_init=has_init),
        grid=(batch, nc),
        in_specs=in_specs,
        out_specs=[pl.BlockSpec((L, heads * dv), row), state_spec],
        out_shape=[jax.ShapeDtypeStruct((batch * seq_len, heads * dv), BF16),
                   jax.ShapeDtypeStruct((batch, heads, dk, dv), F32)],
        compiler_params=pltpu.CompilerParams(
            dimension_semantics=("arbitrary", "arbitrary"), vmem_limit_bytes=VMEM_LIMIT_BYTES),
        name="retention",
    )(*args)
    return o, s_new


def _gla_kernel(*refs, heads, dk, dv, has_init):
    it = iter(refs)
    z_ref, zlr_ref, w2_ref, bg_ref, norm_ref = next(it), next(it), next(it), next(it), next(it)
    s0_ref = next(it) if has_init else None
    o_ref, s_ref, st_scr = next(it), next(it), next(it)
    ch = pl.program_id(1)
    nch = pl.num_programs(1)
    L = z_ref.shape[0]

    @pl.when(ch == 0)
    def _():
        for h in range(heads):
            st_scr[h] = s0_ref[0, h].T if has_init else jnp.zeros((dv, dk), F32)

    la = _log_sigmoid(_dot(zlr_ref[...].astype(BF16), w2_ref[...]) + bg_ref[...]) / GLA_TAU
    b = _cumsum_rows(la)
    sub = min(GLA_SUB, L)
    tri, _ = _tri_masks(sub)
    koff, voff, roff = heads * dk, 2 * heads * dk, 2 * heads * dk + heads * dv
    hs = range(heads)
    bh = [b[:, h * dk:(h + 1) * dk] for h in hs]
    bl = [x[L - 1:L, :] for x in bh]
    q = [z_ref[:, h * dk:(h + 1) * dk].astype(F32) * dk ** -0.5 for h in hs]
    k = [z_ref[:, koff + h * dk:koff + (h + 1) * dk].astype(F32) for h in hs]
    v = [z_ref[:, voff + h * dv:voff + (h + 1) * dv] for h in hs]
    st_old = [st_scr[h] for h in hs]
    sb = [x.astype(BF16) for x in st_old]
    parts = [[] for _ in hs]
    for c in range(L // sub):
        rows = slice(c * sub, (c + 1) * sub)
        ref = [x[c * sub - 1:c * sub, :] if c else jnp.zeros_like(x[0:1, :]) for x in bh]
        qd = [(q[h][rows] * jnp.exp(bh[h][rows] - ref[h])).astype(BF16) for h in hs]
        kd = [(k[h][rows] * jnp.exp(ref[h] - bh[h][rows])).astype(BF16) for h in hs]
        o = [_dot(jnp.where(tri, _dot_nt(qd[h], kd[h]), 0.0).astype(BF16), v[h][rows]) for h in hs]
        for c2 in range(c):
            rows2 = slice(c2 * sub, (c2 + 1) * sub)
            ke = [(k[h][rows2] * jnp.exp(ref[h] - bh[h][rows2])).astype(BF16) for h in hs]
            o = [o[h] + _dot(_dot_nt(qd[h], ke[h]).astype(BF16), v[h][rows2]) for h in hs]
        o = [o[h] + _dot_nt((q[h][rows] * jnp.exp(bh[h][rows])).astype(BF16), sb[h]) for h in hs]
        for h in hs:
            parts[h].append(o[h])
    for h in hs:
        ke = (k[h] * jnp.exp(bl[h] - bh[h])).astype(BF16)
        st_scr[h] = jnp.exp(bl[h]) * st_old[h] + _dot_tn(v[h], ke)
    for h in hs:
        o = parts[h][0] if len(parts[h]) == 1 else jnp.concatenate(parts[h], axis=0)
        r = z_ref[:, roff + h * dv:roff + (h + 1) * dv].astype(F32)
        y = _silu(r) * _rms(o, norm_ref[...])
        o_ref[:, h * dv:(h + 1) * dv] = y.astype(o_ref.dtype)

    @pl.when(ch == nch - 1)
    def _():
        for h in range(heads):
            s_ref[0, h] = st_scr[h].T


def _gla(z, zlr, w_gate2, b_gate, norm, init, layer, *, batch, seq_len, heads, dk, dv):
    L = min(GLA_CHUNK, seq_len)
    nc = seq_len // L
    has_init = init is not None
    hk = heads * dk
    w2 = jnp.zeros((LANES, hk), BF16).at[:GLA_RANK].set(w_gate2.astype(BF16))
    row = lambda b, c: (b * nc + c, 0)
    in_specs = [pl.BlockSpec((L, z.shape[1]), row),
                pl.BlockSpec((L, LANES), row),
                pl.BlockSpec((LANES, hk), lambda b, c: (0, 0)),
                pl.BlockSpec((1, hk), lambda b, c: (0, 0)),
                pl.BlockSpec((1, dv), lambda b, c: (0, 0))]
    args = [z, zlr, w2, b_gate.reshape(1, hk).astype(F32), norm.reshape(1, dv).astype(F32)]
    state_spec = pl.BlockSpec((1, heads, dk, dv), lambda b, c: (b, 0, 0, 0))
    if has_init:
        in_specs.append(pl.BlockSpec((None, 1, heads, dk, dv), lambda b, c: (layer, b, 0, 0, 0)))
        args.append(init)
    o, s_new = pl.pallas_call(
        functools.partial(_gla_kernel, heads=heads, dk=dk, dv=dv, has_init=has_init),
        grid=(batch, nc),
        in_specs=in_specs,
        out_specs=[pl.BlockSpec((L, heads * dv), row), state_spec],
        out_shape=[jax.ShapeDtypeStruct((batch * seq_len, heads * dv), BF16),
                   jax.ShapeDtypeStruct((batch, heads, dk, dv), F32)],
        scratch_shapes=[pltpu.VMEM((heads, dv, dk), F32)],
        compiler_params=pltpu.CompilerParams(
            dimension_semantics=("arbitrary", "arbitrary"), vmem_limit_bytes=VMEM_LIMIT_BYTES),
        name="gla",
    )(*args)
    return o, s_new


def _unit_lower_inverses(mats):
    n = mats[0].shape[0]
    r = lax.broadcasted_iota(jnp.int32, (n, n), 0)
    c = lax.broadcasted_iota(jnp.int32, (n, n), 1)
    eye = jnp.where(r == c, 1.0, 0.0)
    ps = [eye - a for a in mats]
    pws = [_dot(a.astype(BF16), a.astype(BF16)) for a in mats]
    covered = 2
    while covered < n:
        if 2 * covered < n:
            both = [_dot(jnp.concatenate([p, pw], axis=0).astype(BF16), pw.astype(BF16)) for p, pw in zip(ps, pws)]
            ps = [p + b[:n] for p, b in zip(ps, both)]
            pws = [b[n:] for b in both]
        else:
            ps = [p + _dot(p.astype(BF16), pw.astype(BF16)) for p, pw in zip(ps, pws)]
        covered *= 2
    return ps


def _gdn_kernel(*refs, qk_heads, v_heads, d, has_init, chunk, lookahead):
    it = iter(refs)
    z_ref, zba_ref = next(it), next(it)
    zn_ref, zban_ref = (next(it), next(it)) if lookahead else (None, None)
    cw_ref, aneg_ref, dtb_ref, norm_ref = (next(it) for _ in range(4))
    if has_init:
        s0_ref, buf0_ref = next(it), next(it)
    o_ref, s_ref = next(it), next(it)
    xs_scr = next(it)
    sets = [tuple(next(it) for _ in range(4)) for _ in range(2 if lookahead else 1)]
    blk_i = pl.program_id(1)
    L = chunk
    rep = v_heads // qk_heads
    conv_ch = (2 * qk_heads + v_heads) * d
    width = cw_ref.shape[0]
    hist = GDN_CONV_HISTORY

    rows = lax.broadcasted_iota(jnp.int32, ((width - 1) * L, L + 2 * hist), 0)
    col = lax.broadcasted_iota(jnp.int32, ((width - 1) * L, L + 2 * hist), 1)
    src = rows % L - (rows // L + 1)
    hi_col = jnp.where(src >= 0, src, src + L + hist)
    lo_col = jnp.where(src >= 0, -1, src + L + 2 * hist)
    shift = jnp.where((col == hi_col) | (col == lo_col), 1.0, 0.0).astype(BF16)

    def prep(zsrc, basrc, r0, scr):
        qkv_scr, gc_scr, gr_scr, bc_scr = scr
        ba = basrc[r0:r0 + L, :]
        lane = lax.broadcasted_iota(jnp.int32, ba.shape, 1)
        beta = _sigmoid(ba)
        la = aneg_ref[...] * _softplus(ba + dtb_ref[...])
        g = _cumsum_rows(jnp.where(lane >= v_heads, la, 0.0))
        gt = _transpose_rows(g)
        for j in range(v_heads):
            gc_scr[j] = jnp.broadcast_to(g[:, v_heads + j:v_heads + j + 1], (L, LANES))
            bc_scr[j] = jnp.broadcast_to(beta[:, j:j + 1], (L, LANES))
        for a in range(qk_heads):
            pair_row = jnp.concatenate(
                [gt[v_heads + rep * a + r:v_heads + rep * a + r + 1, :] for r in range(rep)], axis=-1)
            gr_scr[a] = jnp.broadcast_to(pair_row, (SUBLANES, rep * L))
        cb = GDN_CONV_BLOCK
        for blk in range(conv_ch // cb):
            sl = slice(blk * cb, (blk + 1) * cb)
            cur = zsrc[r0:r0 + L, sl]
            xs_scr[0:L, sl] = cur
            sh = _dot(shift, xs_scr[:, sl])
            w = cw_ref[:, sl]
            acc = cur.astype(F32) * w[width - 1:width]
            for t in range(1, width):
                acc = acc + sh[(t - 1) * L:t * L] * w[width - 1 - t:width - t]
            xs_scr[L:L + hist, sl] = zsrc[r0 + L - hist:r0 + L, sl]
            xs_scr[L + hist:L + 2 * hist, sl] = jnp.zeros((hist, cb), BF16)
            x = _silu(acc)
            for c in range(cb // d):
                head = blk * (cb // d) + c
                xh = x[:, c * d:(c + 1) * d]
                if head < 2 * qk_heads:
                    xh = xh * lax.rsqrt(jnp.sum(xh * xh, axis=-1, keepdims=True) + EPS)
                    if head < qk_heads:
                        xh = xh * d ** -0.5
                qkv_scr[:, head * d:(head + 1) * d] = xh

    assert rep == 2
    lane2 = lax.broadcasted_iota(jnp.int32, (L, 2 * L), 1)
    row2 = lax.broadcasted_iota(jnp.int32, (L, 2 * L), 0)
    left = lane2 < L
    tcol = jnp.where(left, lane2, lane2 - L)
    causal, strict = row2 >= tcol, row2 > tcol
    eye = jnp.where(row2 == tcol, 1.0, 0.0)

    def blockdiag(x):
        return jnp.concatenate([jnp.where(left, x, 0.0), jnp.where(left, 0.0, x)], axis=0).astype(BF16)

    def pair_inverses(mats):
        ps = [eye - a for a in mats]
        pws = [_dot(a.astype(BF16), blockdiag(a)) for a in mats]
        covered = 2
        while covered < L:
            if 2 * covered < L:
                both = [_dot(jnp.concatenate([p, pw], axis=0).astype(BF16), blockdiag(pw)) for p, pw in zip(ps, pws)]
                ps = [p + b[:L] for p, b in zip(ps, both)]
                pws = [b[L:] for b in both]
            else:
                ps = [p + _dot(p.astype(BF16), blockdiag(pw)) for p, pw in zip(ps, pws)]
            covered *= 2
        return ps

    gv = GDN_V_HEADS_PER_STEP
    gk = gv // rep

    def head_group(grp, r0, scr):
        qkv_scr, gc_scr, gr_scr, bc_scr = scr
        out_rows = slice(r0, r0 + L)

        def cols(base):
            return slice(base * d, (base + 1) * d)
        pairs = range(gk)
        heads = range(gv)
        js = [grp * gv + r for r in heads]
        k = lambda a: qkv_scr[:, cols(qk_heads + grp * gk + a)]
        g_col = lambda r: gc_scr[js[r]]
        b_col = lambda r: bc_scr[js[r]]
        qb = [qkv_scr[:, cols(grp * gk + a)].astype(BF16) for a in pairs]
        kb = [k(a).astype(BF16) for a in pairs]
        kb2 = [jnp.concatenate([x, x], axis=0) for x in kb]
        kk = [_dot_nt(kb[a], kb2[a]) for a in pairs]
        qk = [_dot_nt(qb[a], kb2[a]) for a in pairs]
        g_pair = [jnp.where(left, g_col(2 * a)[:, :2 * L], g_col(2 * a + 1)[:, :2 * L]) for a in pairs]
        b_pair = [jnp.where(left, b_col(2 * a)[:, :2 * L], b_col(2 * a + 1)[:, :2 * L]) for a in pairs]
        g_row = [gr_scr[grp * gk + a][0:1, :] for a in pairs]
        dmat = [jnp.exp(jnp.where(causal, g_pair[a] - g_row[a], -jnp.inf)) for a in pairs]
        tinv = pair_inverses([jnp.where(strict, b_pair[a] * dmat[a] * kk[a], 0.0) for a in pairs])
        rhs = [jnp.concatenate([b_col(r) * qkv_scr[:, cols(2 * qk_heads + js[r])],
                                b_col(r) * jnp.exp(g_col(r)) * k(r // rep)], axis=-1).astype(BF16) for r in heads]
        sol = [_dot(blockdiag(tinv[a]), jnp.concatenate([rhs[2 * a], rhs[2 * a + 1]], axis=0)) for a in pairs]
        sol = [sol[r // rep][(r % rep) * L:(r % rep + 1) * L] for r in heads]
        ss = [_dot(jnp.concatenate([sol[r][:, d:].astype(BF16), qb[r // rep]], axis=0),
                   s_ref[0, js[r]].astype(BF16)) for r in heads]
        ub = [(sol[r][:, :d] - ss[r][:L]).astype(BF16) for r in heads]
        ou = [_dot(blockdiag(qk[a] * dmat[a]), jnp.concatenate([ub[2 * a], ub[2 * a + 1]], axis=0)) for a in pairs]
        for r in heads:
            o = jnp.exp(g_col(r)) * ss[r][L:] + ou[r // rep][(r % rep) * L:(r % rep + 1) * L]
            zg = z_ref[out_rows, cols(conv_ch // d + js[r])].astype(F32)
            y = _rms(o, norm_ref[...]) * _silu(zg)
            o_ref[out_rows, cols(js[r])] = y.astype(o_ref.dtype)
        for r in heads:
            g_last = g_col(r)[L - 1:L, :]
            kd = (jnp.exp(g_last - g_col(r)) * k(r // rep)).astype(BF16)
            s_ref[0, js[r]] = jnp.exp(g_last) * s_ref[0, js[r]] + _dot_tn(kd, ub[r])

    assert v_heads % gv == 0 and gv % rep == 0

    def heads(r0, scr):
        for grp in range(v_heads // gv):
            head_group(grp, r0, scr)

    @pl.when(blk_i == 0)
    def _():
        s_ref[...] = s0_ref[...] if has_init else jnp.zeros_like(s_ref)
        prev = jnp.zeros((hist, conv_ch), F32)
        if has_init:
            prev = jnp.concatenate([prev[:hist - width + 1], buf0_ref[0]], axis=0)
        hi = prev.astype(BF16)
        xs_scr[L:L + hist, :] = hi
        xs_scr[L + hist:L + 2 * hist, :] = (prev - hi.astype(F32)).astype(BF16)
        if lookahead:
            prep(z_ref, zba_ref, 0, sets[0])

    if lookahead:
        prep(z_ref, zba_ref, L, sets[1])
        heads(0, sets[0])
        prep(zn_ref, zban_ref, 0, sets[0])
        heads(L, sets[1])
    else:
        prep(z_ref, zba_ref, 0, sets[0])
        heads(0, sets[0])


def _gdn(z, zba, conv_w, a_log, dt_bias, norm, init, layer, *, batch, seq_len, qk_heads, v_heads, d):
    L = min(CHUNK, seq_len)
    nc = seq_len // L
    lookahead = nc >= 2 and nc % 2 == 0
    cpb = 2 if lookahead else 1
    nb = nc // cpb
    has_init = init is not None
    conv_ch = (2 * qk_heads + v_heads) * d
    width = conv_w.shape[0]
    lane_pad = lambda x: jnp.zeros((1, LANES), F32).at[0, v_heads:2 * v_heads].set(x.astype(F32))
    row = lambda b, i: (b * nb + i, 0)
    nxt = lambda b, i: (b * nc + jnp.minimum(cpb * (i + 1), nc - 1), 0)
    const = lambda b, i: (0, 0)
    in_specs = [pl.BlockSpec((cpb * L, z.shape[1]), row), pl.BlockSpec((cpb * L, LANES), row)]
    args = [z, zba]
    if lookahead:
        in_specs += [pl.BlockSpec((L, z.shape[1]), nxt), pl.BlockSpec((L, LANES), nxt)]
        args += [z, zba]
    in_specs += [pl.BlockSpec((width, conv_ch), const),
                 pl.BlockSpec((1, LANES), const),
                 pl.BlockSpec((1, LANES), const),
                 pl.BlockSpec((1, d), const)]
    args += [conv_w.astype(F32), lane_pad(-jnp.exp(a_log)), lane_pad(dt_bias), norm.reshape(1, d).astype(F32)]
    state_spec = pl.BlockSpec((1, v_heads, d, d), lambda b, i: (b, 0, 0, 0))
    if has_init:
        in_specs += [pl.BlockSpec((None, 1, v_heads, d, d), lambda b, i: (layer, b, 0, 0, 0)),
                     pl.BlockSpec((None, 1, width - 1, conv_ch), lambda b, i: (layer, b, 0, 0))]
        args += [init[0], init[1]]
    scratch_set = [pltpu.VMEM((L, conv_ch), F32),
                   pltpu.VMEM((v_heads, L, LANES), F32),
                   pltpu.VMEM((qk_heads, SUBLANES, v_heads // qk_heads * L), F32),
                   pltpu.VMEM((v_heads, L, LANES), F32)]
    o, s_new = pl.pallas_call(
        functools.partial(_gdn_kernel, qk_heads=qk_heads, v_heads=v_heads, d=d, has_init=has_init,
                          chunk=L, lookahead=lookahead),
        grid=(batch, nb),
        in_specs=in_specs,
        out_specs=[pl.BlockSpec((cpb * L, v_heads * d), row), state_spec],
        out_shape=[jax.ShapeDtypeStruct((batch * seq_len, v_heads * d), BF16),
                   jax.ShapeDtypeStruct((batch, v_heads, d, d), F32)],
        scratch_shapes=[pltpu.VMEM((L + 2 * GDN_CONV_HISTORY, conv_ch), BF16)] + scratch_set * cpb,
        compiler_params=pltpu.CompilerParams(
            dimension_semantics=("arbitrary", "arbitrary"), vmem_limit_bytes=VMEM_LIMIT_BYTES),
        name="gdn",
    )(*args)
    return o, s_new


def _pad_cols(w, n):
    return jnp.zeros((w.shape[0], n), F32).at[:, :w.shape[1]].set(w.astype(F32))


def _trunk(x, states, pos0, p, *, batch, seq_len):
    m, dm = x.shape
    has_init = states is not None
    if has_init:
        mC, mn, mm, rS, gS, dS, dconv, fconv = states
    depth = p['norm_mix'].shape[0]
    outs = [[] for _ in range(8)]
    for i in range(depth):
        kind, j = i % 4, i // 4
        g_mix = p['norm_mix'][i]
        if kind == 0:
            heads, dk, dv = p['mlstm_n_shape']
            w_in = p['mlstm_w_in'][j]
            nm = 2 * heads * dk + 2 * heads * dv
            z, zg = _matmul(x, w_in[:, :nm].astype(BF16), norm_g=g_mix, gate_w=_pad_cols(w_in[:, nm:], LANES),
                            name="mlstm_in")
            bias = _pad_cols(p['mlstm_b_gates'][j][None], LANES)
            init = (mC, mn, mm) if has_init else None
            a, (c_, n_, m_) = _mlstm(z, zg, bias, p['mlstm_norm'][j], init, j,
                                     batch=batch, seq_len=seq_len, heads=heads, dk=dk, dv=dv)
            outs[0].append(c_); outs[1].append(n_); outs[2].append(m_)
            w_out = p['mlstm_w_out'][j]
        elif kind == 1:
            heads, dk, dv = p['ret_shape']
            z = _matmul(x, p['ret_w_in'][j].astype(BF16), norm_g=g_mix, name="ret_in")
            a, s_ = _retention(z, p['ret_norm'][j], rS if has_init else None, j, pos0,
                               batch=batch, seq_len=seq_len, heads=heads, dk=dk, dv=dv)
            outs[3].append(s_)
            w_out = p['ret_w_out'][j]
        elif kind == 2:
            heads, dk, dv = p['gla_shape']
            w_in = p['gla_w_in'][j]
            nm = 2 * heads * dk + 2 * heads * dv
            z, zlr = _matmul(x, w_in[:, :nm].astype(BF16), norm_g=g_mix, gate_w=_pad_cols(w_in[:, nm:], LANES),
                             name="gla_in")
            a, s_ = _gla(z, zlr, p['gla_w_gate2'][j], p['gla_b_gate'][j], p['gla_norm'][j],
                         gS if has_init else None, j,
                         batch=batch, seq_len=seq_len, heads=heads, dk=dk, dv=dv)
            outs[4].append(s_)
            w_out = p['gla_w_out'][j]
        else:
            qk_heads, v_heads, d = p['gdn_shape']
            w_in = p['gdn_w_in'][j]
            conv_ch = (2 * qk_heads + v_heads) * d
            nm = conv_ch + v_heads * d
            width = p['gdn_conv_w'].shape[1]
            z, zba = _matmul(x, w_in[:, :nm].astype(BF16), norm_g=g_mix, gate_w=_pad_cols(w_in[:, nm:], LANES),
                             name="gdn_in")
            init = (dS, dconv) if has_init else None
            a, s_ = _gdn(z, zba, p['gdn_conv_w'][j], p['gdn_A_log'][j], p['gdn_dt_bias'][j], p['gdn_norm'][j],
                         init, j, batch=batch, seq_len=seq_len, qk_heads=qk_heads, v_heads=v_heads, d=d)
            outs[5].append(s_)
            outs[6].append(z.reshape(batch, seq_len, nm)[:, seq_len - (width - 1):, :conv_ch].astype(F32))
            w_out = p['gdn_w_out'][j]
        x = _matmul(a, w_out.astype(BF16), residual=x, out_dtype=F32, name="mix_out")

        f = p['ffn_w_gate'].shape[2]
        fw = p['ffn_conv_w'].shape[1]
        buf8 = jnp.zeros((batch, SUBLANES, f), F32)
        if has_init:
            buf8 = buf8.at[:, SUBLANES - (fw - 1):].set(fconv[i].astype(F32))
        act, tail = _ffn_in(x, p['norm_ffn'][i], p['ffn_w_gate'][i].astype(BF16), p['ffn_w_up'][i].astype(BF16),
                            buf8, p['ffn_conv_w'][i], p['ffn_conv_b'][i], seq_len=seq_len)
        outs[7].append(tail[:, SUBLANES - (fw - 1):])
        x = _matmul(act, p['ffn_w_down'][i].astype(BF16), residual=x, out_dtype=F32, name="ffn_out")
    y = _final_norm(x, p['norm_final'])
    return y.reshape(batch, seq_len, dm), tuple(o[0][None] if len(o) == 1 else jnp.stack(o) for o in outs)


def kernel(x_prompt, x_sample, state_mlstm_C, state_mlstm_n, state_mlstm_m, state_ret_S, state_gla_S, state_gdn_S, state_gdn_conv, state_ffn_conv, norm_mix, norm_ffn, norm_final, mlstm_w_in, mlstm_b_gates, mlstm_norm, mlstm_w_out, ret_w_in, ret_norm, ret_w_out, gla_w_in, gla_w_gate2, gla_b_gate, gla_norm, gla_w_out, gdn_w_in, gdn_conv_w, gdn_A_log, gdn_dt_bias, gdn_norm, gdn_w_out, ffn_w_gate, ffn_w_up, ffn_conv_w, ffn_conv_b, ffn_w_down):
    p = dict(norm_mix=norm_mix, norm_ffn=norm_ffn, norm_final=norm_final,
             mlstm_w_in=mlstm_w_in, mlstm_b_gates=mlstm_b_gates, mlstm_norm=mlstm_norm, mlstm_w_out=mlstm_w_out,
             ret_w_in=ret_w_in, ret_norm=ret_norm, ret_w_out=ret_w_out,
             gla_w_in=gla_w_in, gla_w_gate2=gla_w_gate2, gla_b_gate=gla_b_gate, gla_norm=gla_norm, gla_w_out=gla_w_out,
             gdn_w_in=gdn_w_in, gdn_conv_w=gdn_conv_w, gdn_A_log=gdn_A_log, gdn_dt_bias=gdn_dt_bias,
             gdn_norm=gdn_norm, gdn_w_out=gdn_w_out,
             ffn_w_gate=ffn_w_gate, ffn_w_up=ffn_w_up, ffn_conv_w=ffn_conv_w, ffn_conv_b=ffn_conv_b,
             ffn_w_down=ffn_w_down)
    p['mlstm_n_shape'] = state_mlstm_C.shape[2:]
    p['ret_shape'] = state_ret_S.shape[2:]
    p['gla_shape'] = state_gla_S.shape[2:]
    d = state_gdn_S.shape[-1]
    v_heads = state_gdn_S.shape[2]
    p['gdn_shape'] = ((state_gdn_conv.shape[-1] // d - v_heads) // 2, v_heads, d)

    bp, tp, dm = x_prompt.shape
    bs, ts, _ = x_sample.shape
    y_prompt, p_states = _trunk(x_prompt.reshape(bp * tp, dm), None, 0, p, batch=bp, seq_len=tp)
    cache = (state_mlstm_C, state_mlstm_n, state_mlstm_m, state_ret_S, state_gla_S, state_gdn_S,
             state_gdn_conv, state_ffn_conv)
    y_sample, s_states = _trunk(x_sample.reshape(bs * ts, dm), cache, PAST_LEN, p, batch=bs, seq_len=ts)
    return (y_prompt, y_sample) + p_states + s_states
```

```python
import functools
import math

import jax
import jax.numpy as jnp
from jax import lax
from jax.experimental import pallas as pl
from jax.experimental.pallas import tpu as pltpu

F32 = jnp.float32
BF16 = jnp.bfloat16
EPS = 1e-6
PAST_LEN = 2048
ROPE_BASE = 10000.0
GLA_TAU = 16.0
GLA_RANK = 16
CHUNK = 64
MLSTM_CHUNK = 256
RET_CHUNK = 256
GLA_CHUNK = 128
GLA_SUB = 64
GDN_V_HEADS_PER_STEP = 32
GDN_CONV_HISTORY = 16
GDN_CONV_BLOCK = 256
FFN_ROW_CHUNK = 256
LANES = 128
SUBLANES = 8
VMEM_LIMIT_BYTES = 56 * 1024 * 1024


def _sigmoid(x):
    return 1.0 / (1.0 + jnp.exp(-x))


def _silu(x):
    return x * _sigmoid(x)


def _log1p_exp_neg_abs(x):
    return jnp.log(1.0 + jnp.exp(-jnp.abs(x)))


def _log_sigmoid(x):
    return jnp.minimum(x, 0.0) - _log1p_exp_neg_abs(x)


def _softplus(x):
    return jnp.maximum(x, 0.0) + _log1p_exp_neg_abs(x)


def _rms(x, g):
    return x * lax.rsqrt(jnp.mean(x * x, axis=-1, keepdims=True) + EPS) * g


def _dot(a, b):
    return jnp.dot(a, b, preferred_element_type=F32)


def _dot_nt(a, b):
    return lax.dot_general(a, b, (((1,), (1,)), ((), ())), preferred_element_type=F32)


def _dot_tn(a, b):
    return lax.dot_general(a, b, (((0,), (0,)), ((), ())), preferred_element_type=F32)


def _split3(x):
    hi = x.astype(BF16)
    r = x - hi.astype(F32)
    mid = r.astype(BF16)
    lo = (r - mid.astype(F32)).astype(BF16)
    return hi, mid, lo


def _cumsum_rows(x):
    n = x.shape[0]
    r = lax.broadcasted_iota(jnp.int32, (n, n), 0)
    c = lax.broadcasted_iota(jnp.int32, (n, n), 1)
    tril = jnp.where(r >= c, 1.0, 0.0).astype(BF16)
    hi, mid, lo = _split3(x)
    return _dot(tril, hi) + _dot(tril, mid) + _dot(tril, lo)


def _transpose_rows(x):
    n = x.shape[0]
    if n < LANES:
        x = jnp.concatenate([x, jnp.zeros((LANES - n, x.shape[1]), x.dtype)], axis=0)
    return x.T[:, :n]


def _causal_taps(x3, prev, cw):
    width = cw.shape[0]
    t8 = lax.broadcasted_iota(jnp.int32, (x3.shape[0], SUBLANES, x3.shape[2]), 1)
    acc = x3 * cw[width - 1:width]
    for s in range(1, width):
        sh = pltpu.roll(x3, s, 1)
        head = sh[:, :SUBLANES]
        for e in range(s):
            head = jnp.where(t8 == e, prev[s - 1 - e], head)
        sh = jnp.concatenate([head, sh[:, SUBLANES:]], axis=1)
        acc = acc + sh * cw[width - 1 - s:width - s]
    return acc


def _tri_masks(n):
    r = lax.broadcasted_iota(jnp.int32, (n, n), 0)
    c = lax.broadcasted_iota(jnp.int32, (n, n), 1)
    return r >= c, r > c


def _mm_kernel(*refs, has_norm, has_res, has_gate):
    it = iter(refs)
    x_ref = next(it)
    g_ref = next(it) if has_norm else None
    w_ref = next(it)
    wgate_ref = next(it) if has_gate else None
    r_ref = next(it) if has_res else None
    o_ref = next(it)
    ogate_ref = next(it) if has_gate else None
    h_scr = next(it) if has_norm else None
    if has_norm:
        @pl.when(pl.program_id(1) == 0)
        def _():
            h = _rms(x_ref[...].astype(F32), g_ref[...])
            h_scr[...] = h.astype(BF16)
            if has_gate:
                both = _dot(h.astype(BF16), wgate_ref[...])
                ogate_ref[...] = both[:, :LANES] + both[:, LANES:]
        h = h_scr[...]
    else:
        h = x_ref[...]
    acc = _dot(h, w_ref[...])
    if has_res:
        acc = acc + r_ref[...]
    o_ref[...] = acc.astype(o_ref.dtype)


def _pick(n, cands):
    for c in cands:
        if n % c == 0:
            return c
    raise ValueError(f"no tile for {n}")


def _matmul(x, w, *, layer=None, norm_g=None, gate_w=None, residual=None, out_dtype=BF16, bm=1024, bn=None,
            name="mm"):
    m, k = x.shape
    n = w.shape[-1]
    bm = min(bm, m)
    if bn is None:
        bn = _pick(n, (1024, 512, 256, 128)) if k <= 2048 else _pick(n, (512, 256, 128))
    assert m % bm == 0 and n % bn == 0
    has_norm, has_res, has_gate = norm_g is not None, residual is not None, gate_w is not None
    assert has_norm or not has_gate
    in_specs = [pl.BlockSpec((bm, k), lambda i, j: (i, 0))]
    args = [x]
    if has_norm:
        in_specs.append(pl.BlockSpec((1, k), lambda i, j: (0, 0)))
        args.append(norm_g.reshape(1, k).astype(F32))
    if layer is None:
        in_specs.append(pl.BlockSpec((k, bn), lambda i, j: (0, j)))
    else:
        in_specs.append(pl.BlockSpec((None, k, bn), lambda i, j: (layer, 0, j)))
    args.append(w)
    if has_gate:
        g_hi = gate_w.astype(BF16)
        g_lo = (gate_w - g_hi.astype(F32)).astype(BF16)
        in_specs.append(pl.BlockSpec((k, 2 * LANES), lambda i, j: (0, 0)))
        args.append(jnp.concatenate([g_hi, g_lo], axis=1))
    if has_res:
        in_specs.append(pl.BlockSpec((bm, bn), lambda i, j: (i, j)))
        args.append(residual)
    out_specs = [pl.BlockSpec((bm, bn), lambda i, j: (i, j))]
    out_shape = [jax.ShapeDtypeStruct((m, n), out_dtype)]
    if has_gate:
        out_specs.append(pl.BlockSpec((bm, LANES), lambda i, j: (i, 0)))
        out_shape.append(jax.ShapeDtypeStruct((m, LANES), F32))
    outs = pl.pallas_call(
        functools.partial(_mm_kernel, has_norm=has_norm, has_res=has_res, has_gate=has_gate),
        grid=(m // bm, n // bn),
        in_specs=in_specs,
        out_specs=out_specs,
        out_shape=out_shape,
        scratch_shapes=[pltpu.VMEM((bm, k), BF16)] if has_norm else [],
        compiler_params=pltpu.CompilerParams(
            dimension_semantics=("arbitrary", "arbitrary"), vmem_limit_bytes=VMEM_LIMIT_BYTES),
        name=name,
    )(*args)
    return outs if has_gate else outs[0]


def _final_norm_kernel(x_ref, g_ref, o_ref):
    o_ref[...] = _rms(x_ref[...], g_ref[...])


def _final_norm(x, g, *, bm=512):
    m, k = x.shape
    bm = min(bm, m)
    return pl.pallas_call(
        _final_norm_kernel,
        grid=(m // bm,),
        in_specs=[pl.BlockSpec((bm, k), lambda i: (i, 0)), pl.BlockSpec((1, k), lambda i: (0, 0))],
        out_specs=pl.BlockSpec((bm, k), lambda i: (i, 0)),
        out_shape=jax.ShapeDtypeStruct((m, k), F32),
        compiler_params=pltpu.CompilerParams(
            dimension_semantics=("arbitrary",), vmem_limit_bytes=VMEM_LIMIT_BYTES),
        name="final_norm",
    )(x, g.reshape(1, k).astype(F32))


def _ffn_in_kernel(x_ref, g_ref, wg_ref, wu_ref, buf_ref, cw_ref, cb_ref, act_ref, tail_ref, h_scr, halo_scr,
                   *, tiles_per_seq, seq_rows):
    i, j = pl.program_id(0), pl.program_id(1)
    bm, bn = act_ref.shape

    @pl.when(j == 0)
    def _():
        h_scr[...] = _rms(x_ref[...], g_ref[...]).astype(BF16)

    nseq = bm // seq_rows
    first = (i % tiles_per_seq) == 0
    halo = jnp.where(first, buf_ref[...], halo_scr[j])
    prev2, prev1 = halo[:, 6:7], halo[:, 7:8]
    cw, cb = cw_ref[...], cb_ref[...]
    wg, wu = wg_ref[...], wu_ref[...]
    rc = min(FFN_ROW_CHUNK, bm)
    spc = max(rc // seq_rows, 1)
    tt = rc // spc
    tails = []
    for r in range(bm // rc):
        rows = slice(r * rc, (r + 1) * rc)
        h = h_scr[rows, :]
        g3 = _dot(h, wg).reshape(spc, tt, bn)
        u3 = _dot(h, wu).reshape(spc, tt, bn)
        pv = [prev1, prev2] if nseq == 1 else [prev1[r * spc:(r + 1) * spc], prev2[r * spc:(r + 1) * spc]]
        gc = _causal_taps(g3, pv, cw) + cb
        act_ref[rows, :] = (_silu(gc) * u3).reshape(rc, bn).astype(act_ref.dtype)
        if nseq == 1:
            prev2, prev1 = g3[:, tt - 2:tt - 1], g3[:, tt - 1:tt]
            tails = [g3[:, tt - SUBLANES:]]
        else:
            tails.append(g3[:, tt - SUBLANES:])
    tail = tails[0] if len(tails) == 1 else jnp.concatenate(tails, axis=0)
    halo_scr[j] = tail
    tail_ref[...] = tail


def _ffn_in(x, norm_g, w_gate, w_up, layer, buf8, conv_w, conv_b, *, seq_len, bm=1024, bn=512):
    m, k = x.shape
    f = w_gate.shape[2]
    batch = m // seq_len
    bm = min(bm, m)
    assert f % bn == 0 and m % bm == 0 and (seq_len % bm == 0 or bm % seq_len == 0)
    nf = f // bn
    tiles_per_seq = max(seq_len // bm, 1)
    seq_rows = min(seq_len, bm)
    nseq = bm // seq_rows
    seq_blk = lambda i, j: (i // tiles_per_seq, 0, j)
    act, tails = pl.pallas_call(
        functools.partial(_ffn_in_kernel, tiles_per_seq=tiles_per_seq, seq_rows=seq_rows),
        grid=(m // bm, nf),
        in_specs=[
            pl.BlockSpec((bm, k), lambda i, j: (i, 0)),
            pl.BlockSpec((1, k), lambda i, j: (0, 0)),
            pl.BlockSpec((None, k, bn), lambda i, j: (layer, 0, j)),
            pl.BlockSpec((None, k, bn), lambda i, j: (layer, 0, j)),
            pl.BlockSpec((nseq, SUBLANES, bn), seq_blk),
            pl.BlockSpec((3, bn), lambda i, j: (0, j)),
            pl.BlockSpec((1, bn), lambda i, j: (0, j)),
        ],
        out_specs=[pl.BlockSpec((bm, bn), lambda i, j: (i, j)),
                   pl.BlockSpec((nseq, SUBLANES, bn), lambda i, j: (i, 0, j))],
        out_shape=[jax.ShapeDtypeStruct((m, f), BF16),
                   jax.ShapeDtypeStruct((m // bm * nseq, SUBLANES, f), F32)],
        scratch_shapes=[pltpu.VMEM((bm, k), BF16), pltpu.VMEM((nf, nseq, SUBLANES, bn), F32)],
        compiler_params=pltpu.CompilerParams(
            dimension_semantics=("arbitrary", "arbitrary"), vmem_limit_bytes=VMEM_LIMIT_BYTES),
        name="ffn_in",
    )(x, norm_g.reshape(1, k).astype(F32), w_gate, w_up, buf8, conv_w.astype(F32),
      conv_b.reshape(1, f).astype(F32))
    return act, tails.reshape(batch, tiles_per_seq, SUBLANES, f)[:, tiles_per_seq - 1]


def _mlstm_kernel(*refs, heads, dk, dv, has_init):
    it = iter(refs)
    z_ref, zg_ref, bias_ref, norm_ref = next(it), next(it), next(it), next(it)
    if has_init:
        c0_ref, n0_ref, m0_ref = next(it), next(it), next(it)
    o_ref, c_ref, n_ref, m_ref = next(it), next(it), next(it), next(it)
    ch = pl.program_id(1)
    L = z_ref.shape[0]

    @pl.when(ch == 0)
    def _():
        if has_init:
            c_ref[...] = c0_ref[...]
            n_ref[...] = n0_ref[...]
            m_ref[...] = m0_ref[...]
        else:
            c_ref[...] = jnp.zeros_like(c_ref)
            n_ref[...] = jnp.zeros_like(n_ref)
            m_ref[...] = jnp.zeros_like(m_ref)

    g = zg_ref[...] + bias_ref[...]
    lane = lax.broadcasted_iota(jnp.int32, g.shape, 1)
    u = jnp.where(lane < heads, g, _cumsum_rows(_log_sigmoid(g)))
    ut = _transpose_rows(u)
    tri, _ = _tri_masks(L)
    m_all = m_ref[0]
    m_lane = lax.broadcasted_iota(jnp.int32, m_all.shape, 1)
    qoff, koff, voff, ooff = 0, heads * dk, 2 * heads * dk, 2 * heads * dk + heads * dv
    hs = range(heads)
    ig_col = [u[:, h:h + 1] for h in hs]
    ig_row = [ut[h:h + 1, :] for h in hs]
    b_col = [u[:, heads + h:heads + h + 1] for h in hs]
    b_row = [ut[heads + h:heads + h + 1, :] for h in hs]
    m_prev = [m_all[:, h:h + 1] for h in hs]
    q = [(z_ref[:, qoff + h * dk:qoff + (h + 1) * dk].astype(F32) * dk ** -0.5).astype(BF16) for h in hs]
    k = [z_ref[:, koff + h * dk:koff + (h + 1) * dk] for h in hs]
    v = [z_ref[:, voff + h * dv:voff + (h + 1) * dv] for h in hs]
    c_old = [c_ref[0, h] for h in hs]
    n_old = [n_ref[0, h:h + 1, :] for h in hs]
    d = [jnp.where(tri, b_col[h] - b_row[h] + ig_row[h], -jnp.inf) for h in hs]
    li = [b_col[h] + m_prev[h] for h in hs]
    m_t = [jnp.maximum(li[h], jnp.max(d[h], axis=1, keepdims=True)) for h in hs]
    w_inter = [jnp.exp(li[h] - m_t[h]) for h in hs]
    a = [_dot_nt(q[h], k[h]) * jnp.exp(d[h] - m_t[h]) for h in hs]
    num = [_dot(a[h].astype(BF16), v[h]) + w_inter[h] * _dot(q[h], c_old[h].astype(BF16)) for h in hs]
    den = [jnp.sum(a[h], axis=1, keepdims=True)
           + w_inter[h] * jnp.sum(q[h].astype(F32) * n_old[h], axis=1, keepdims=True) for h in hs]
    hh = [num[h] / jnp.maximum(jnp.abs(den[h]), jnp.exp(-m_t[h])) for h in hs]
    for h in hs:
        m_new = m_t[h][L - 1:L, :]
        b_last = b_col[h][L - 1:L, :]
        w_k = jnp.exp(b_last - b_col[h] + ig_col[h] - m_new)
        dec = jnp.exp(b_last + m_prev[h] - m_new)
        kw = k[h].astype(F32) * w_k
        c_ref[0, h] = dec * c_old[h] + _dot_tn(kw.astype(BF16), v[h])
        n_ref[0, h:h + 1, :] = dec * n_old[h] + jnp.sum(kw, axis=0, keepdims=True)
        m_all = jnp.where(m_lane == h, m_new, m_all)
    for h in hs:
        og = z_ref[:, ooff + h * dv:ooff + (h + 1) * dv].astype(F32)
        y = _sigmoid(og) * _rms(hh[h], norm_ref[...])
        o_ref[:, h * dv:(h + 1) * dv] = y.astype(o_ref.dtype)
    m_ref[0] = m_all


def _mlstm(z, zg, bias, norm, init, layer, *, batch, seq_len, heads, dk, dv):
    L = min(MLSTM_CHUNK, seq_len)
    nc = seq_len // L
    has_init = init is not None
    row = lambda b, c: (b * nc + c, 0)
    in_specs = [pl.BlockSpec((L, z.shape[1]), row),
                pl.BlockSpec((L, LANES), row),
                pl.BlockSpec((1, LANES), lambda b, c: (0, 0)),
                pl.BlockSpec((1, dv), lambda b, c: (0, 0))]
    args = [z, zg, bias, norm.reshape(1, dv).astype(F32)]
    state_specs = [pl.BlockSpec((1, heads, dk, dv), lambda b, c: (b, 0, 0, 0)),
                   pl.BlockSpec((1, heads, dk), lambda b, c: (b, 0, 0)),
                   pl.BlockSpec((1, 1, heads), lambda b, c: (b, 0, 0))]
    if has_init:
        in_specs += [pl.BlockSpec((None, 1, heads, dk, dv), lambda b, c: (layer, b, 0, 0, 0)),
                     pl.BlockSpec((None, 1, heads, dk), lambda b, c: (layer, b, 0, 0)),
                     state_specs[2]]
        args += [init[0], init[1], init[2][layer].reshape(batch, 1, heads)]
    o, c_new, n_new, m_new = pl.pallas_call(
        functools.partial(_mlstm_kernel, heads=heads, dk=dk, dv=dv, has_init=has_init),
        grid=(batch, nc),
        in_specs=in_specs,
        out_specs=[pl.BlockSpec((L, heads * dv), row)] + state_specs,
        out_shape=[jax.ShapeDtypeStruct((batch * seq_len, heads * dv), BF16),
                   jax.ShapeDtypeStruct((batch, heads, dk, dv), F32),
                   jax.ShapeDtypeStruct((batch, heads, dk), F32),
                   jax.ShapeDtypeStruct((batch, 1, heads), F32)],
        compiler_params=pltpu.CompilerParams(
            dimension_semantics=("arbitrary", "arbitrary"), vmem_limit_bytes=VMEM_LIMIT_BYTES),
        name="mlstm",
    )(*args)
    return o, (c_new, n_new, m_new.reshape(batch, heads))


def _ret_kernel(*refs, heads, dk, dv, has_init):
    it = iter(refs)
    cos_ref, sin_ref, z_ref, norm_ref = next(it), next(it), next(it), next(it)
    s0_ref = next(it) if has_init else None
    o_ref, s_ref = next(it), next(it)
    ch = pl.program_id(1)
    L = z_ref.shape[0]
    half = dk // 2

    @pl.when(ch == 0)
    def _():
        s_ref[...] = s0_ref[...] if has_init else jnp.zeros_like(s_ref)

    cos, sin = cos_ref[...], sin_ref[...]
    t_col = lax.broadcasted_iota(jnp.int32, (L, 1), 0).astype(F32)
    rel = (lax.broadcasted_iota(jnp.int32, (L, L), 0) - lax.broadcasted_iota(jnp.int32, (L, L), 1)).astype(F32)
    koff, voff, goff = heads * dk, 2 * heads * dk, 2 * heads * dk + heads * dv

    def rope(x):
        x1, x2 = x[:, :half], x[:, half:]
        return jnp.concatenate([x1 * cos - x2 * sin, x1 * sin + x2 * cos], axis=-1)

    for h in range(heads):
        lg = math.log1p(-2.0 ** (-5.0 - h))
        q = rope(z_ref[:, h * dk:(h + 1) * dk].astype(F32)) * dk ** -0.5
        k = rope(z_ref[:, koff + h * dk:koff + (h + 1) * dk].astype(F32))
        v = z_ref[:, voff + h * dv:voff + (h + 1) * dv]
        gate = z_ref[:, goff + h * dv:goff + (h + 1) * dv].astype(F32)
        qb = q.astype(BF16)
        dec = jnp.where(rel >= 0, jnp.exp(jnp.maximum(rel, 0.0) * lg), 0.0)
        a = _dot_nt(qb, k.astype(BF16)) * dec
        s_old = s_ref[0, h]
        o = _dot(a.astype(BF16), v) + jnp.exp((t_col + 1.0) * lg) * _dot(qb, s_old.astype(BF16))
        kd = k * jnp.exp((L - 1.0 - t_col) * lg)
        s_ref[0, h] = math.exp(L * lg) * s_old + _dot_tn(kd.astype(BF16), v)
        y = _silu(gate) * _rms(o, norm_ref[...])
        o_ref[:, h * dv:(h + 1) * dv] = y.astype(o_ref.dtype)


def _retention(z, norm, init, layer, pos0, *, batch, seq_len, heads, dk, dv):
    L = min(RET_CHUNK, seq_len)
    nc = seq_len // L
    has_init = init is not None
    half = dk // 2
    inv = ROPE_BASE ** (-jnp.arange(half, dtype=F32) / half)
    ang = (pos0 + jnp.arange(seq_len)).astype(F32)[:, None] * inv[None]
    row = lambda b, c: (b * nc + c, 0)
    in_specs = [pl.BlockSpec((L, half), lambda b, c: (c, 0)),
                pl.BlockSpec((L, half), lambda b, c: (c, 0)),
                pl.BlockSpec((L, z.shape[1]), row),
                pl.BlockSpec((1, dv), lambda b, c: (0, 0))]
    args = [jnp.cos(ang), jnp.sin(ang), z, norm.reshape(1, dv).astype(F32)]
    state_spec = pl.BlockSpec((1, heads, dk, dv), lambda b, c: (b, 0, 0, 0))
    if has_init:
        in_specs.append(pl.BlockSpec((None, 1, heads, dk, dv), lambda b, c: (layer, b, 0, 0, 0)))
        args.append(init)
    o, s_new = pl.pallas_call(
        functools.partial(_ret_kernel, heads=heads, dk=dk, dv=dv, has_init=has_init),
        grid=(batch, nc),
        in_specs=in_specs,
        out_specs=[pl.BlockSpec((L, heads * dv), row), state_spec],
        out_shape=[jax.ShapeDtypeStruct((batch * seq_len, heads * dv), BF16),
                   jax.ShapeDtypeStruct((batch, heads, dk, dv), F32)],
        compiler_params=pltpu.CompilerParams(
            dimension_semantics=("arbitrary", "arbitrary"), vmem_limit_bytes=VMEM_LIMIT_BYTES),
        name="retention",
    )(*args)
    return o, s_new


def _gla_kernel(*refs, heads, dk, dv, has_init):
    it = iter(refs)
    z_ref, zlr_ref, w2_ref, bg_ref, norm_ref = next(it), next(it), next(it), next(it), next(it)
    s0_ref = next(it) if has_init else None
    o_ref, s_ref, st_scr = next(it), next(it), next(it)
    ch = pl.program_id(1)
    nch = pl.num_programs(1)
    L = z_ref.shape[0]

    @pl.when(ch == 0)
    def _():
        for h in range(heads):
            st_scr[h] = s0_ref[0, h].T if has_init else jnp.zeros((dv, dk), F32)

    la = _log_sigmoid(_dot(zlr_ref[...].astype(BF16), w2_ref[...]) + bg_ref[...]) / GLA_TAU
    b = _cumsum_rows(la)
    sub = min(GLA_SUB, L)
    tri, _ = _tri_masks(sub)
    koff, voff, roff = heads * dk, 2 * heads * dk, 2 * heads * dk + heads * dv
    hs = range(heads)
    bh = [b[:, h * dk:(h + 1) * dk] for h in hs]
    bl = [x[L - 1:L, :] for x in bh]
    q = [z_ref[:, h * dk:(h + 1) * dk].astype(F32) * dk ** -0.5 for h in hs]
    k = [z_ref[:, koff + h * dk:koff + (h + 1) * dk].astype(F32) for h in hs]
    v = [z_ref[:, voff + h * dv:voff + (h + 1) * dv] for h in hs]
    st_old = [st_scr[h] for h in hs]
    sb = [x.astype(BF16) for x in st_old]
    parts = [[] for _ in hs]
    for c in range(L // sub):
        rows = slice(c * sub, (c + 1) * sub)
        ref = [x[c * sub - 1:c * sub, :] if c else jnp.zeros_like(x[0:1, :]) for x in bh]
        qd = [(q[h][rows] * jnp.exp(bh[h][rows] - ref[h])).astype(BF16) for h in hs]
        kd = [(k[h][rows] * jnp.exp(ref[h] - bh[h][rows])).astype(BF16) for h in hs]
        o = [_dot(jnp.where(tri, _dot_nt(qd[h], kd[h]), 0.0).astype(BF16), v[h][rows]) for h in hs]
        for c2 in range(c):
            rows2 = slice(c2 * sub, (c2 + 1) * sub)
            ke = [(k[h][rows2] * jnp.exp(ref[h] - bh[h][rows2])).astype(BF16) for h in hs]
            o = [o[h] + _dot(_dot_nt(qd[h], ke[h]).astype(BF16), v[h][rows2]) for h in hs]
        o = [o[h] + _dot_nt((q[h][rows] * jnp.exp(bh[h][rows])).astype(BF16), sb[h]) for h in hs]
        for h in hs:
            parts[h].append(o[h])
    for h in hs:
        ke = (k[h] * jnp.exp(bl[h] - bh[h])).astype(BF16)
        st_scr[h] = jnp.exp(bl[h]) * st_old[h] + _dot_tn(v[h], ke)
    for h in hs:
        o = parts[h][0] if len(parts[h]) == 1 else jnp.concatenate(parts[h], axis=0)
        r = z_ref[:, roff + h * dv:roff + (h + 1) * dv].astype(F32)
        y = _silu(r) * _rms(o, norm_ref[...])
        o_ref[:, h * dv:(h + 1) * dv] = y.astype(o_ref.dtype)

    @pl.when(ch == nch - 1)
    def _():
        for h in range(heads):
            s_ref[0, h] = st_scr[h].T


def _gla(z, zlr, w_gate2, b_gate, norm, init, layer, *, batch, seq_len, heads, dk, dv):
    L = min(GLA_CHUNK, seq_len)
    nc = seq_len // L
    has_init = init is not None
    hk = heads * dk
    w2 = jnp.zeros((LANES, hk), BF16).at[:GLA_RANK].set(w_gate2.astype(BF16))
    row = lambda b, c: (b * nc + c, 0)
    in_specs = [pl.BlockSpec((L, z.shape[1]), row),
                pl.BlockSpec((L, LANES), row),
                pl.BlockSpec((LANES, hk), lambda b, c: (0, 0)),
                pl.BlockSpec((1, hk), lambda b, c: (0, 0)),
                pl.BlockSpec((1, dv), lambda b, c: (0, 0))]
    args = [z, zlr, w2, b_gate.reshape(1, hk).astype(F32), norm.reshape(1, dv).astype(F32)]
    state_spec = pl.BlockSpec((1, heads, dk, dv), lambda b, c: (b, 0, 0, 0))
    if has_init:
        in_specs.append(pl.BlockSpec((None, 1, heads, dk, dv), lambda b, c: (layer, b, 0, 0, 0)))
        args.append(init)
    o, s_new = pl.pallas_call(
        functools.partial(_gla_kernel, heads=heads, dk=dk, dv=dv, has_init=has_init),
        grid=(batch, nc),
        in_specs=in_specs,
        out_specs=[pl.BlockSpec((L, heads * dv), row), state_spec],
        out_shape=[jax.ShapeDtypeStruct((batch * seq_len, heads * dv), BF16),
                   jax.ShapeDtypeStruct((batch, heads, dk, dv), F32)],
        scratch_shapes=[pltpu.VMEM((heads, dv, dk), F32)],
        compiler_params=pltpu.CompilerParams(
            dimension_semantics=("arbitrary", "arbitrary"), vmem_limit_bytes=VMEM_LIMIT_BYTES),
        name="gla",
    )(*args)
    return o, s_new


def _unit_lower_inverses(mats):
    n = mats[0].shape[0]
    r = lax.broadcasted_iota(jnp.int32, (n, n), 0)
    c = lax.broadcasted_iota(jnp.int32, (n, n), 1)
    eye = jnp.where(r == c, 1.0, 0.0)
    ps = [eye - a for a in mats]
    pws = [_dot(a.astype(BF16), a.astype(BF16)) for a in mats]
    covered = 2
    while covered < n:
        if 2 * covered < n:
            both = [_dot(jnp.concatenate([p, pw], axis=0).astype(BF16), pw.astype(BF16)) for p, pw in zip(ps, pws)]
            ps = [p + b[:n] for p, b in zip(ps, both)]
            pws = [b[n:] for b in both]
        else:
            ps = [p + _dot(p.astype(BF16), pw.astype(BF16)) for p, pw in zip(ps, pws)]
        covered *= 2
    return ps


def _gdn_kernel(*refs, qk_heads, v_heads, d, has_init, chunk, lookahead):
    it = iter(refs)
    z_ref, zba_ref = next(it), next(it)
    zn_ref, zban_ref = (next(it), next(it)) if lookahead else (None, None)
    cw_ref, aneg_ref, dtb_ref, norm_ref = (next(it) for _ in range(4))
    if has_init:
        s0_ref, buf0_ref = next(it), next(it)
    o_ref, s_ref = next(it), next(it)
    xs_scr = next(it)
    sets = [tuple(next(it) for _ in range(4)) for _ in range(2 if lookahead else 1)]
    blk_i = pl.program_id(1)
    L = chunk
    rep = v_heads // qk_heads
    conv_ch = (2 * qk_heads + v_heads) * d
    width = cw_ref.shape[0]
    hist = GDN_CONV_HISTORY

    rows = lax.broadcasted_iota(jnp.int32, ((width - 1) * L, L + 2 * hist), 0)
    col = lax.broadcasted_iota(jnp.int32, ((width - 1) * L, L + 2 * hist), 1)
    src = rows % L - (rows // L + 1)
    hi_col = jnp.where(src >= 0, src, src + L + hist)
    lo_col = jnp.where(src >= 0, -1, src + L + 2 * hist)
    shift = jnp.where((col == hi_col) | (col == lo_col), 1.0, 0.0).astype(BF16)

    def prep(zsrc, basrc, r0, scr):
        qkv_scr, gc_scr, gr_scr, bc_scr = scr
        ba = basrc[r0:r0 + L, :]
        lane = lax.broadcasted_iota(jnp.int32, ba.shape, 1)
        beta = _sigmoid(ba)
        la = aneg_ref[...] * _softplus(ba + dtb_ref[...])
        g = _cumsum_rows(jnp.where(lane >= v_heads, la, 0.0))
        gt = _transpose_rows(g)
        for j in range(v_heads):
            gc_scr[j] = jnp.broadcast_to(g[:, v_heads + j:v_heads + j + 1], (L, LANES))
            bc_scr[j] = jnp.broadcast_to(beta[:, j:j + 1], (L, LANES))
        for a in range(qk_heads):
            pair_row = jnp.concatenate(
                [gt[v_heads + rep * a + r:v_heads + rep * a + r + 1, :] for r in range(rep)], axis=-1)
            gr_scr[a] = jnp.broadcast_to(pair_row, (SUBLANES, rep * L))
        cb = GDN_CONV_BLOCK
        for blk in range(conv_ch // cb):
            sl = slice(blk * cb, (blk + 1) * cb)
            cur = zsrc[r0:r0 + L, sl]
            xs_scr[0:L, sl] = cur
            sh = _dot(shift, xs_scr[:, sl])
            w = cw_ref[:, sl]
            acc = cur.astype(F32) * w[width - 1:width]
            for t in range(1, width):
                acc = acc + sh[(t - 1) * L:t * L] * w[width - 1 - t:width - t]
            xs_scr[L:L + hist, sl] = zsrc[r0 + L - hist:r0 + L, sl]
            xs_scr[L + hist:L + 2 * hist, sl] = jnp.zeros((hist, cb), BF16)
            x = _silu(acc)
            for c in range(cb // d):
                head = blk * (cb // d) + c
                xh = x[:, c * d:(c + 1) * d]
                if head < 2 * qk_heads:
                    xh = xh * lax.rsqrt(jnp.sum(xh * xh, axis=-1, keepdims=True) + EPS)
                    if head < qk_heads:
                        xh = xh * d ** -0.5
                qkv_scr[:, head * d:(head + 1) * d] = xh

    assert rep == 2
    lane2 = lax.broadcasted_iota(jnp.int32, (L, 2 * L), 1)
    row2 = lax.broadcasted_iota(jnp.int32, (L, 2 * L), 0)
    left = lane2 < L
    tcol = jnp.where(left, lane2, lane2 - L)
    causal, strict = row2 >= tcol, row2 > tcol
    eye = jnp.where(row2 == tcol, 1.0, 0.0)

    def blockdiag(x):
        return jnp.concatenate([jnp.where(left, x, 0.0), jnp.where(left, 0.0, x)], axis=0).astype(BF16)

    def pair_inverses(mats):
        ps = [eye - a for a in mats]
        pws = [_dot(a.astype(BF16), blockdiag(a)) for a in mats]
        covered = 2
        while covered < L:
            if 2 * covered < L:
                both = [_dot(jnp.concatenate([p, pw], axis=0).astype(BF16), blockdiag(pw)) for p, pw in zip(ps, pws)]
                ps = [p + b[:L] for p, b in zip(ps, both)]
                pws = [b[L:] for b in both]
            else:
                ps = [p + _dot(p.astype(BF16), blockdiag(pw)) for p, pw in zip(ps, pws)]
            covered *= 2
        return ps

    gv = GDN_V_HEADS_PER_STEP
    gk = gv // rep

    def head_group(grp, r0, scr):
        qkv_scr, gc_scr, gr_scr, bc_scr = scr
        out_rows = slice(r0, r0 + L)

        def cols(base):
            return slice(base * d, (base + 1) * d)
        pairs = range(gk)
        heads = range(gv)
        js = [grp * gv + r for r in heads]
        k = lambda a: qkv_scr[:, cols(qk_heads + grp * gk + a)]
        g_col = lambda r: gc_scr[js[r]]
        b_col = lambda r: bc_scr[js[r]]
        qb = [qkv_scr[:, cols(grp * gk + a)].astype(BF16) for a in pairs]
        kb = [k(a).astype(BF16) for a in pairs]
        kb2 = [jnp.concatenate([x, x], axis=0) for x in kb]
        kk = [_dot_nt(kb[a], kb2[a]) for a in pairs]
        qk = [_dot_nt(qb[a], kb2[a]) for a in pairs]
        g_pair = [jnp.where(left, g_col(2 * a)[:, :2 * L], g_col(2 * a + 1)[:, :2 * L]) for a in pairs]
        b_pair = [jnp.where(left, b_col(2 * a)[:, :2 * L], b_col(2 * a + 1)[:, :2 * L]) for a in pairs]
        g_row = [gr_scr[grp * gk + a][0:1, :] for a in pairs]
        dmat = [jnp.exp(jnp.where(causal, g_pair[a] - g_row[a], -jnp.inf)) for a in pairs]
        tinv = pair_inverses([jnp.where(strict, b_pair[a] * dmat[a] * kk[a], 0.0) for a in pairs])
        rhs = [jnp.concatenate([b_col(r) * qkv_scr[:, cols(2 * qk_heads + js[r])],
                                b_col(r) * jnp.exp(g_col(r)) * k(r // rep)], axis=-1).astype(BF16) for r in heads]
        sol = [_dot(blockdiag(tinv[a]), jnp.concatenate([rhs[2 * a], rhs[2 * a + 1]], axis=0)) for a in pairs]
        sol = [sol[r // rep][(r % rep) * L:(r % rep + 1) * L] for r in heads]
        ss = [_dot(jnp.concatenate([sol[r][:, d:].astype(BF16), qb[r // rep]], axis=0),
                   s_ref[0, js[r]].astype(BF16)) for r in heads]
        ub = [(sol[r][:, :d] - ss[r][:L]).astype(BF16) for r in heads]
        ou = [_dot(blockdiag(qk[a] * dmat[a]), jnp.concatenate([ub[2 * a], ub[2 * a + 1]], axis=0)) for a in pairs]
        for r in heads:
            o = jnp.exp(g_col(r)) * ss[r][L:] + ou[r // rep][(r % rep) * L:(r % rep + 1) * L]
            zg = z_ref[out_rows, cols(conv_ch // d + js[r])].astype(F32)
            y = _rms(o, norm_ref[...]) * _silu(zg)
            o_ref[out_rows, cols(js[r])] = y.astype(o_ref.dtype)
        for r in heads:
            g_last = g_col(r)[L - 1:L, :]
            kd = (jnp.exp(g_last - g_col(r)) * k(r // rep)).astype(BF16)
            s_ref[0, js[r]] = jnp.exp(g_last) * s_ref[0, js[r]] + _dot_tn(kd, ub[r])

    assert v_heads % gv == 0 and gv % rep == 0

    def heads(r0, scr):
        for grp in range(v_heads // gv):
            head_group(grp, r0, scr)

    @pl.when(blk_i == 0)
    def _():
        s_ref[...] = s0_ref[...] if has_init else jnp.zeros_like(s_ref)
        prev = jnp.zeros((hist, conv_ch), F32)
        if has_init:
            prev = jnp.concatenate([prev[:hist - width + 1], buf0_ref[0]], axis=0)
        hi = prev.astype(BF16)
        xs_scr[L:L + hist, :] = hi
        xs_scr[L + hist:L + 2 * hist, :] = (prev - hi.astype(F32)).astype(BF16)
        if lookahead:
            prep(z_ref, zba_ref, 0, sets[0])

    if lookahead:
        prep(z_ref, zba_ref, L, sets[1])
        heads(0, sets[0])
        prep(zn_ref, zban_ref, 0, sets[0])
        heads(L, sets[1])
    else:
        prep(z_ref, zba_ref, 0, sets[0])
        heads(0, sets[0])


def _gdn(z, zba, conv_w, a_log, dt_bias, norm, init, layer, *, batch, seq_len, qk_heads, v_heads, d):
    L = min(CHUNK, seq_len)
    nc = seq_len // L
    lookahead = nc >= 2 and nc % 2 == 0
    cpb = 2 if lookahead else 1
    nb = nc // cpb
    has_init = init is not None
    conv_ch = (2 * qk_heads + v_heads) * d
    width = conv_w.shape[0]
    lane_pad = lambda x: jnp.zeros((1, LANES), F32).at[0, v_heads:2 * v_heads].set(x.astype(F32))
    row = lambda b, i: (b * nb + i, 0)
    nxt = lambda b, i: (b * nc + jnp.minimum(cpb * (i + 1), nc - 1), 0)
    const = lambda b, i: (0, 0)
    in_specs = [pl.BlockSpec((cpb * L, z.shape[1]), row), pl.BlockSpec((cpb * L, LANES), row)]
    args = [z, zba]
    if lookahead:
        in_specs += [pl.BlockSpec((L, z.shape[1]), nxt), pl.BlockSpec((L, LANES), nxt)]
        args += [z, zba]
    in_specs += [pl.BlockSpec((width, conv_ch), const),
                 pl.BlockSpec((1, LANES), const),
                 pl.BlockSpec((1, LANES), const),
                 pl.BlockSpec((1, d), const)]
    args += [conv_w.astype(F32), lane_pad(-jnp.exp(a_log)), lane_pad(dt_bias), norm.reshape(1, d).astype(F32)]
    state_spec = pl.BlockSpec((1, v_heads, d, d), lambda b, i: (b, 0, 0, 0))
    if has_init:
        in_specs += [pl.BlockSpec((None, 1, v_heads, d, d), lambda b, i: (layer, b, 0, 0, 0)),
                     pl.BlockSpec((None, 1, width - 1, conv_ch), lambda b, i: (layer, b, 0, 0))]
        args += [init[0], init[1]]
    scratch_set = [pltpu.VMEM((L, conv_ch), F32),
                   pltpu.VMEM((v_heads, L, LANES), F32),
                   pltpu.VMEM((qk_heads, SUBLANES, v_heads // qk_heads * L), F32),
                   pltpu.VMEM((v_heads, L, LANES), F32)]
    o, s_new = pl.pallas_call(
        functools.partial(_gdn_kernel, qk_heads=qk_heads, v_heads=v_heads, d=d, has_init=has_init,
                          chunk=L, lookahead=lookahead),
        grid=(batch, nb),
        in_specs=in_specs,
        out_specs=[pl.BlockSpec((cpb * L, v_heads * d), row), state_spec],
        out_shape=[jax.ShapeDtypeStruct((batch * seq_len, v_heads * d), BF16),
                   jax.ShapeDtypeStruct((batch, v_heads, d, d), F32)],
        scratch_shapes=[pltpu.VMEM((L + 2 * GDN_CONV_HISTORY, conv_ch), BF16)] + scratch_set * cpb,
        compiler_params=pltpu.CompilerParams(
            dimension_semantics=("arbitrary", "arbitrary"), vmem_limit_bytes=VMEM_LIMIT_BYTES),
        name="gdn",
    )(*args)
    return o, s_new


def _pad_cols(w, n):
    return jnp.zeros((w.shape[0], n), F32).at[:, :w.shape[1]].set(w.astype(F32))


def _trunk(x, states, pos0, p, *, batch, seq_len):
    m, dm = x.shape
    has_init = states is not None
    if has_init:
        mC, mn, mm, rS, gS, dS, dconv, fconv = states
    depth = p['norm_mix'].shape[0]
    outs = [[] for _ in range(8)]
    for i in range(depth):
        kind, j = i % 4, i // 4
        g_mix = p['norm_mix'][i]
        if kind == 0:
            heads, dk, dv = p['mlstm_n_shape']
            w_in = p['mlstm_w_in'][j]
            nm = 2 * heads * dk + 2 * heads * dv
            z, zg = _matmul(x, w_in[:, :nm].astype(BF16), norm_g=g_mix, gate_w=_pad_cols(w_in[:, nm:], LANES),
                            name="mlstm_in")
            bias = _pad_cols(p['mlstm_b_gates'][j][None], LANES)
            init = (mC, mn, mm) if has_init else None
            a, (c_, n_, m_) = _mlstm(z, zg, bias, p['mlstm_norm'][j], init, j,
                                     batch=batch, seq_len=seq_len, heads=heads, dk=dk, dv=dv)
            outs[0].append(c_); outs[1].append(n_); outs[2].append(m_)
            w_out = p['mlstm_w_out'][j]
        elif kind == 1:
            heads, dk, dv = p['ret_shape']
            z = _matmul(x, p['ret_w_in'][j].astype(BF16), norm_g=g_mix, name="ret_in")
            a, s_ = _retention(z, p['ret_norm'][j], rS if has_init else None, j, pos0,
                               batch=batch, seq_len=seq_len, heads=heads, dk=dk, dv=dv)
            outs[3].append(s_)
            w_out = p['ret_w_out'][j]
        elif kind == 2:
            heads, dk, dv = p['gla_shape']
            w_in = p['gla_w_in'][j]
            nm = 2 * heads * dk + 2 * heads * dv
            z, zlr = _matmul(x, w_in[:, :nm].astype(BF16), norm_g=g_mix, gate_w=_pad_cols(w_in[:, nm:], LANES),
                             name="gla_in")
            a, s_ = _gla(z, zlr, p['gla_w_gate2'][j], p['gla_b_gate'][j], p['gla_norm'][j],
                         gS if has_init else None, j,
                         batch=batch, seq_len=seq_len, heads=heads, dk=dk, dv=dv)
            outs[4].append(s_)
            w_out = p['gla_w_out'][j]
        else:
            qk_heads, v_heads, d = p['gdn_shape']
            w_in = p['gdn_w_in'][j]
            conv_ch = (2 * qk_heads + v_heads) * d
            nm = conv_ch + v_heads * d
            width = p['gdn_conv_w'].shape[1]
            z, zba = _matmul(x, w_in[:, :nm].astype(BF16), norm_g=g_mix, gate_w=_pad_cols(w_in[:, nm:], LANES),
                             name="gdn_in")
            init = (dS, dconv) if has_init else None
            a, s_ = _gdn(z, zba, p['gdn_conv_w'][j], p['gdn_A_log'][j], p['gdn_dt_bias'][j], p['gdn_norm'][j],
                         init, j, batch=batch, seq_len=seq_len, qk_heads=qk_heads, v_heads=v_heads, d=d)
            outs[5].append(s_)
            outs[6].append(z.reshape(batch, seq_len, nm)[:, seq_len - (width - 1):, :conv_ch].astype(F32))
            w_out = p['gdn_w_out'][j]
        x = _matmul(a, w_out.astype(BF16), residual=x, out_dtype=F32, name="mix_out")

        f = p['ffn_w_gate'].shape[2]
        fw = p['ffn_conv_w'].shape[1]
        buf8 = jnp.zeros((batch, SUBLANES, f), F32)
        if has_init:
            buf8 = buf8.at[:, SUBLANES - (fw - 1):].set(fconv[i].astype(F32))
        act, tail = _ffn_in(x, p['norm_ffn'][i], p['ffn_w_gate_bf16'], p['ffn_w_up_bf16'], i,
                            buf8, p['ffn_conv_w'][i], p['ffn_conv_b'][i], seq_len=seq_len)
        outs[7].append(tail[:, SUBLANES - (fw - 1):])
        x = _matmul(act, p['ffn_w_down_bf16'], layer=i, residual=x, out_dtype=F32, name="ffn_out")
    y = _final_norm(x, p['norm_final'])
    return y.reshape(batch, seq_len, dm), tuple(o[0][None] if len(o) == 1 else jnp.stack(o) for o in outs)


def kernel(x_prompt, x_sample, state_mlstm_C, state_mlstm_n, state_mlstm_m, state_ret_S, state_gla_S, state_gdn_S, state_gdn_conv, state_ffn_conv, norm_mix, norm_ffn, norm_final, mlstm_w_in, mlstm_b_gates, mlstm_norm, mlstm_w_out, ret_w_in, ret_norm, ret_w_out, gla_w_in, gla_w_gate2, gla_b_gate, gla_norm, gla_w_out, gdn_w_in, gdn_conv_w, gdn_A_log, gdn_dt_bias, gdn_norm, gdn_w_out, ffn_w_gate, ffn_w_up, ffn_conv_w, ffn_conv_b, ffn_w_down):
    p = dict(norm_mix=norm_mix, norm_ffn=norm_ffn, norm_final=norm_final,
             mlstm_w_in=mlstm_w_in, mlstm_b_gates=mlstm_b_gates, mlstm_norm=mlstm_norm, mlstm_w_out=mlstm_w_out,
             ret_w_in=ret_w_in, ret_norm=ret_norm, ret_w_out=ret_w_out,
             gla_w_in=gla_w_in, gla_w_gate2=gla_w_gate2, gla_b_gate=gla_b_gate, gla_norm=gla_norm, gla_w_out=gla_w_out,
             gdn_w_in=gdn_w_in, gdn_conv_w=gdn_conv_w, gdn_A_log=gdn_A_log, gdn_dt_bias=gdn_dt_bias,
             gdn_norm=gdn_norm, gdn_w_out=gdn_w_out,
             ffn_w_gate=ffn_w_gate, ffn_w_up=ffn_w_up, ffn_conv_w=ffn_conv_w, ffn_conv_b=ffn_conv_b,
             ffn_w_down=ffn_w_down)
    for name in ('ffn_w_gate', 'ffn_w_up', 'ffn_w_down'):
        p[name + '_bf16'] = p[name].astype(BF16)
    p['mlstm_n_shape'] = state_mlstm_C.shape[2:]
    p['ret_shape'] = state_ret_S.shape[2:]
    p['gla_shape'] = state_gla_S.shape[2:]
    d = state_gdn_S.shape[-1]
    v_heads = state_gdn_S.shape[2]
    p['gdn_shape'] = ((state_gdn_conv.shape[-1] // d - v_heads) // 2, v_heads, d)

    bp, tp, dm = x_prompt.shape
    bs, ts, _ = x_sample.shape
    y_prompt, p_states = _trunk(x_prompt.reshape(bp * tp, dm), None, 0, p, batch=bp, seq_len=tp)
    cache = (state_mlstm_C, state_mlstm_n, state_mlstm_m, state_ret_S, state_gla_S, state_gdn_S,
             state_gdn_conv, state_ffn_conv)
    y_sample, s_states = _trunk(x_sample.reshape(bs * ts, dm), cache, PAST_LEN, p, batch=bs, seq_len=ts)
    return (y_prompt, y_sample) + p_states + s_states
```

```python
import functools
import math

import jax
import jax.numpy as jnp
from jax import lax
from jax.experimental import pallas as pl
from jax.experimental.pallas import tpu as pltpu

F32 = jnp.float32
BF16 = jnp.bfloat16
EPS = 1e-6
PAST_LEN = 2048
ROPE_BASE = 10000.0
GLA_TAU = 16.0
GLA_RANK = 16
CHUNK = 64
MLSTM_CHUNK = 256
RET_CHUNK = 256
GLA_CHUNK = 128
GLA_SUB = 64
GDN_V_HEADS_PER_STEP = 32
GDN_CONV_HISTORY = 16
GDN_CONV_BLOCK = 256
FFN_ROW_CHUNK = 256
LANES = 128
SUBLANES = 8
VMEM_LIMIT_BYTES = 56 * 1024 * 1024


def _sigmoid(x):
    return 1.0 / (1.0 + jnp.exp(-x))


def _silu(x):
    return x * _sigmoid(x)


def _log1p_exp_neg_abs(x):
    return jnp.log(1.0 + jnp.exp(-jnp.abs(x)))


def _log_sigmoid(x):
    return jnp.minimum(x, 0.0) - _log1p_exp_neg_abs(x)


def _softplus(x):
    return jnp.maximum(x, 0.0) + _log1p_exp_neg_abs(x)


def _rms(x, g):
    return x * lax.rsqrt(jnp.mean(x * x, axis=-1, keepdims=True) + EPS) * g


def _dot(a, b):
    return jnp.dot(a, b, preferred_element_type=F32)


def _dot_nt(a, b):
    return lax.dot_general(a, b, (((1,), (1,)), ((), ())), preferred_element_type=F32)


def _dot_tn(a, b):
    return lax.dot_general(a, b, (((0,), (0,)), ((), ())), preferred_element_type=F32)


def _split3(x):
    hi = x.astype(BF16)
    r = x - hi.astype(F32)
    mid = r.astype(BF16)
    lo = (r - mid.astype(F32)).astype(BF16)
    return hi, mid, lo


def _cumsum_rows(x):
    n = x.shape[0]
    r = lax.broadcasted_iota(jnp.int32, (n, n), 0)
    c = lax.broadcasted_iota(jnp.int32, (n, n), 1)
    tril = jnp.where(r >= c, 1.0, 0.0).astype(BF16)
    hi, mid, lo = _split3(x)
    return _dot(tril, hi) + _dot(tril, mid) + _dot(tril, lo)


def _transpose_rows(x):
    n = x.shape[0]
    if n < LANES:
        x = jnp.concatenate([x, jnp.zeros((LANES - n, x.shape[1]), x.dtype)], axis=0)
    return x.T[:, :n]


def _causal_taps(x3, prev, cw):
    width = cw.shape[0]
    t8 = lax.broadcasted_iota(jnp.int32, (x3.shape[0], SUBLANES, x3.shape[2]), 1)
    acc = x3 * cw[width - 1:width]
    for s in range(1, width):
        sh = pltpu.roll(x3, s, 1)
        head = sh[:, :SUBLANES]
        for e in range(s):
            head = jnp.where(t8 == e, prev[s - 1 - e], head)
        sh = jnp.concatenate([head, sh[:, SUBLANES:]], axis=1)
        acc = acc + sh * cw[width - 1 - s:width - s]
    return acc


def _tri_masks(n):
    r = lax.broadcasted_iota(jnp.int32, (n, n), 0)
    c = lax.broadcasted_iota(jnp.int32, (n, n), 1)
    return r >= c, r > c


def _mm_kernel(*refs, has_norm, has_res, has_gate):
    it = iter(refs)
    x_ref = next(it)
    g_ref = next(it) if has_norm else None
    w_ref = next(it)
    wgate_ref = next(it) if has_gate else None
    r_ref = next(it) if has_res else None
    o_ref = next(it)
    ogate_ref = next(it) if has_gate else None
    h_scr = next(it) if has_norm else None
    if has_norm:
        @pl.when(pl.program_id(1) == 0)
        def _():
            h = _rms(x_ref[...].astype(F32), g_ref[...])
            h_scr[...] = h.astype(BF16)
            if has_gate:
                both = _dot(h.astype(BF16), wgate_ref[...])
                ogate_ref[...] = both[:, :LANES] + both[:, LANES:]
        h = h_scr[...]
    else:
        h = x_ref[...]
    acc = _dot(h, w_ref[...])
    if has_res:
        acc = acc + r_ref[...]
    o_ref[...] = acc.astype(o_ref.dtype)


def _pick(n, cands):
    for c in cands:
        if n % c == 0:
            return c
    raise ValueError(f"no tile for {n}")


def _matmul(x, w, *, layer=None, norm_g=None, gate_w=None, residual=None, out_dtype=BF16, bm=1024, bn=None,
            name="mm"):
    m, k = x.shape
    n = w.shape[-1]
    bm = min(bm, m)
    if bn is None:
        bn = _pick(n, (1024, 512, 256, 128)) if k <= 2048 else _pick(n, (512, 256, 128))
    assert m % bm == 0 and n % bn == 0
    has_norm, has_res, has_gate = norm_g is not None, residual is not None, gate_w is not None
    assert has_norm or not has_gate
    in_specs = [pl.BlockSpec((bm, k), lambda i, j: (i, 0))]
    args = [x]
    if has_norm:
        in_specs.append(pl.BlockSpec((1, k), lambda i, j: (0, 0)))
        args.append(norm_g.reshape(1, k).astype(F32))
    if layer is None:
        in_specs.append(pl.BlockSpec((k, bn), lambda i, j: (0, j)))
    else:
        in_specs.append(pl.BlockSpec((None, k, bn), lambda i, j: (layer, 0, j)))
    args.append(w)
    if has_gate:
        g_hi = gate_w.astype(BF16)
        g_lo = (gate_w - g_hi.astype(F32)).astype(BF16)
        in_specs.append(pl.BlockSpec((k, 2 * LANES), lambda i, j: (0, 0)))
        args.append(jnp.concatenate([g_hi, g_lo], axis=1))
    if has_res:
        in_specs.append(pl.BlockSpec((bm, bn), lambda i, j: (i, j)))
        args.append(residual)
    out_specs = [pl.BlockSpec((bm, bn), lambda i, j: (i, j))]
    out_shape = [jax.ShapeDtypeStruct((m, n), out_dtype)]
    if has_gate:
        out_specs.append(pl.BlockSpec((bm, LANES), lambda i, j: (i, 0)))
        out_shape.append(jax.ShapeDtypeStruct((m, LANES), F32))
    outs = pl.pallas_call(
        functools.partial(_mm_kernel, has_norm=has_norm, has_res=has_res, has_gate=has_gate),
        grid=(m // bm, n // bn),
        in_specs=in_specs,
        out_specs=out_specs,
        out_shape=out_shape,
        scratch_shapes=[pltpu.VMEM((bm, k), BF16)] if has_norm else [],
        compiler_params=pltpu.CompilerParams(
            dimension_semantics=("arbitrary", "arbitrary"), vmem_limit_bytes=VMEM_LIMIT_BYTES),
        name=name,
    )(*args)
    return outs if has_gate else outs[0]


def _final_norm_kernel(x_ref, g_ref, o_ref):
    o_ref[...] = _rms(x_ref[...], g_ref[...])


def _final_norm(x, g, *, bm=512):
    m, k = x.shape
    bm = min(bm, m)
    return pl.pallas_call(
        _final_norm_kernel,
        grid=(m // bm,),
        in_specs=[pl.BlockSpec((bm, k), lambda i: (i, 0)), pl.BlockSpec((1, k), lambda i: (0, 0))],
        out_specs=pl.BlockSpec((bm, k), lambda i: (i, 0)),
        out_shape=jax.ShapeDtypeStruct((m, k), F32),
        compiler_params=pltpu.CompilerParams(
            dimension_semantics=("arbitrary",), vmem_limit_bytes=VMEM_LIMIT_BYTES),
        name="final_norm",
    )(x, g.reshape(1, k).astype(F32))


def _ffn_in_kernel(x_ref, g_ref, wg_ref, wu_ref, buf_ref, cw_ref, cb_ref, act_ref, tail_ref, h_scr, halo_scr,
                   *, tiles_per_seq, seq_rows):
    i, j = pl.program_id(0), pl.program_id(1)
    bm, bn = act_ref.shape

    @pl.when(j == 0)
    def _():
        h_scr[...] = _rms(x_ref[...], g_ref[...]).astype(BF16)

    nseq = bm // seq_rows
    first = (i % tiles_per_seq) == 0
    halo = jnp.where(first, buf_ref[...], halo_scr[j])
    prev2, prev1 = halo[:, 6:7], halo[:, 7:8]
    cw, cb = cw_ref[...], cb_ref[...]
    wg, wu = wg_ref[...], wu_ref[...]
    rc = min(FFN_ROW_CHUNK, bm)
    spc = max(rc // seq_rows, 1)
    tt = rc // spc
    tails = []
    for r in range(bm // rc):
        rows = slice(r * rc, (r + 1) * rc)
        h = h_scr[rows, :]
        g3 = _dot(h, wg).reshape(spc, tt, bn)
        u3 = _dot(h, wu).reshape(spc, tt, bn)
        pv = [prev1, prev2] if nseq == 1 else [prev1[r * spc:(r + 1) * spc], prev2[r * spc:(r + 1) * spc]]
        gc = _causal_taps(g3, pv, cw) + cb
        act_ref[rows, :] = (_silu(gc) * u3).reshape(rc, bn).astype(act_ref.dtype)
        if nseq == 1:
            prev2, prev1 = g3[:, tt - 2:tt - 1], g3[:, tt - 1:tt]
            tails = [g3[:, tt - SUBLANES:]]
        else:
            tails.append(g3[:, tt - SUBLANES:])
    tail = tails[0] if len(tails) == 1 else jnp.concatenate(tails, axis=0)
    halo_scr[j] = tail
    tail_ref[...] = tail


def _ffn_in(x, norm_g, w_gate, w_up, layer, buf8, conv_w, conv_b, *, seq_len, bm=1024, bn=512):
    m, k = x.shape
    f = w_gate.shape[2]
    batch = m // seq_len
    bm = min(bm, m)
    assert f % bn == 0 and m % bm == 0 and (seq_len % bm == 0 or bm % seq_len == 0)
    nf = f // bn
    tiles_per_seq = max(seq_len // bm, 1)
    seq_rows = min(seq_len, bm)
    nseq = bm // seq_rows
    seq_blk = lambda i, j: (i // tiles_per_seq, 0, j)
    act, tails = pl.pallas_call(
        functools.partial(_ffn_in_kernel, tiles_per_seq=tiles_per_seq, seq_rows=seq_rows),
        grid=(m // bm, nf),
        in_specs=[
            pl.BlockSpec((bm, k), lambda i, j: (i, 0)),
            pl.BlockSpec((1, k), lambda i, j: (0, 0)),
            pl.BlockSpec((None, k, bn), lambda i, j: (layer, 0, j)),
            pl.BlockSpec((None, k, bn), lambda i, j: (layer, 0, j)),
            pl.BlockSpec((nseq, SUBLANES, bn), seq_blk),
            pl.BlockSpec((3, bn), lambda i, j: (0, j)),
            pl.BlockSpec((1, bn), lambda i, j: (0, j)),
        ],
        out_specs=[pl.BlockSpec((bm, bn), lambda i, j: (i, j)),
                   pl.BlockSpec((nseq, SUBLANES, bn), lambda i, j: (i, 0, j))],
        out_shape=[jax.ShapeDtypeStruct((m, f), BF16),
                   jax.ShapeDtypeStruct((m // bm * nseq, SUBLANES, f), F32)],
        scratch_shapes=[pltpu.VMEM((bm, k), BF16), pltpu.VMEM((nf, nseq, SUBLANES, bn), F32)],
        compiler_params=pltpu.CompilerParams(
            dimension_semantics=("arbitrary", "arbitrary"), vmem_limit_bytes=VMEM_LIMIT_BYTES),
        name="ffn_in",
    )(x, norm_g.reshape(1, k).astype(F32), w_gate, w_up, buf8, conv_w.astype(F32),
      conv_b.reshape(1, f).astype(F32))
    return act, tails.reshape(batch, tiles_per_seq, SUBLANES, f)[:, tiles_per_seq - 1]


def _mlstm_kernel(*refs, heads, dk, dv, has_init):
    it = iter(refs)
    z_ref, zg_ref, bias_ref, norm_ref = next(it), next(it), next(it), next(it)
    if has_init:
        c0_ref, n0_ref, m0_ref = next(it), next(it), next(it)
    o_ref, c_ref, n_ref, m_ref = next(it), next(it), next(it), next(it)
    ch = pl.program_id(1)
    L = z_ref.shape[0]

    @pl.when(ch == 0)
    def _():
        if has_init:
            c_ref[...] = c0_ref[...]
            n_ref[...] = n0_ref[...]
            m_ref[...] = m0_ref[...]
        else:
            c_ref[...] = jnp.zeros_like(c_ref)
            n_ref[...] = jnp.zeros_like(n_ref)
            m_ref[...] = jnp.zeros_like(m_ref)

    g = zg_ref[...] + bias_ref[...]
    lane = lax.broadcasted_iota(jnp.int32, g.shape, 1)
    u = jnp.where(lane < heads, g, _cumsum_rows(_log_sigmoid(g)))
    ut = _transpose_rows(u)
    tri, _ = _tri_masks(L)
    m_all = m_ref[0]
    m_lane = lax.broadcasted_iota(jnp.int32, m_all.shape, 1)
    qoff, koff, voff, ooff = 0, heads * dk, 2 * heads * dk, 2 * heads * dk + heads * dv
    hs = range(heads)
    ig_col = [u[:, h:h + 1] for h in hs]
    ig_row = [ut[h:h + 1, :] for h in hs]
    b_col = [u[:, heads + h:heads + h + 1] for h in hs]
    b_row = [ut[heads + h:heads + h + 1, :] for h in hs]
    m_prev = [m_all[:, h:h + 1] for h in hs]
    q = [(z_ref[:, qoff + h * dk:qoff + (h + 1) * dk].astype(F32) * dk ** -0.5).astype(BF16) for h in hs]
    k = [z_ref[:, koff + h * dk:koff + (h + 1) * dk] for h in hs]
    v = [z_ref[:, voff + h * dv:voff + (h + 1) * dv] for h in hs]
    c_old = [c_ref[0, h] for h in hs]
    n_old = [n_ref[0, h:h + 1, :] for h in hs]
    d = [jnp.where(tri, b_col[h] - b_row[h] + ig_row[h], -jnp.inf) for h in hs]
    li = [b_col[h] + m_prev[h] for h in hs]
    m_t = [jnp.maximum(li[h], jnp.max(d[h], axis=1, keepdims=True)) for h in hs]
    w_inter = [jnp.exp(li[h] - m_t[h]) for h in hs]
    a = [_dot_nt(q[h], k[h]) * jnp.exp(d[h] - m_t[h]) for h in hs]
    num = [_dot(a[h].astype(BF16), v[h]) + w_inter[h] * _dot(q[h], c_old[h].astype(BF16)) for h in hs]
    den = [jnp.sum(a[h], axis=1, keepdims=True)
           + w_inter[h] * jnp.sum(q[h].astype(F32) * n_old[h], axis=1, keepdims=True) for h in hs]
    hh = [num[h] / jnp.maximum(jnp.abs(den[h]), jnp.exp(-m_t[h])) for h in hs]
    for h in hs:
        m_new = m_t[h][L - 1:L, :]
        b_last = b_col[h][L - 1:L, :]
        w_k = jnp.exp(b_last - b_col[h] + ig_col[h] - m_new)
        dec = jnp.exp(b_last + m_prev[h] - m_new)
        kw = k[h].astype(F32) * w_k
        c_ref[0, h] = dec * c_old[h] + _dot_tn(kw.astype(BF16), v[h])
        n_ref[0, h:h + 1, :] = dec * n_old[h] + jnp.sum(kw, axis=0, keepdims=True)
        m_all = jnp.where(m_lane == h, m_new, m_all)
    for h in hs:
        og = z_ref[:, ooff + h * dv:ooff + (h + 1) * dv].astype(F32)
        y = _sigmoid(og) * _rms(hh[h], norm_ref[...])
        o_ref[:, h * dv:(h + 1) * dv] = y.astype(o_ref.dtype)
    m_ref[0] = m_all


def _mlstm(z, zg, bias, norm, init, layer, *, batch, seq_len, heads, dk, dv):
    L = min(MLSTM_CHUNK, seq_len)
    nc = seq_len // L
    has_init = init is not None
    row = lambda b, c: (b * nc + c, 0)
    in_specs = [pl.BlockSpec((L, z.shape[1]), row),
                pl.BlockSpec((L, LANES), row),
                pl.BlockSpec((1, LANES), lambda b, c: (0, 0)),
                pl.BlockSpec((1, dv), lambda b, c: (0, 0))]
    args = [z, zg, bias, norm.reshape(1, dv).astype(F32)]
    state_specs = [pl.BlockSpec((1, heads, dk, dv), lambda b, c: (b, 0, 0, 0)),
                   pl.BlockSpec((1, heads, dk), lambda b, c: (b, 0, 0)),
                   pl.BlockSpec((1, 1, heads), lambda b, c: (b, 0, 0))]
    if has_init:
        in_specs += [pl.BlockSpec((None, 1, heads, dk, dv), lambda b, c: (layer, b, 0, 0, 0)),
                     pl.BlockSpec((None, 1, heads, dk), lambda b, c: (layer, b, 0, 0)),
                     state_specs[2]]
        args += [init[0], init[1], init[2][layer].reshape(batch, 1, heads)]
    o, c_new, n_new, m_new = pl.pallas_call(
        functools.partial(_mlstm_kernel, heads=heads, dk=dk, dv=dv, has_init=has_init),
        grid=(batch, nc),
        in_specs=in_specs,
        out_specs=[pl.BlockSpec((L, heads * dv), row)] + state_specs,
        out_shape=[jax.ShapeDtypeStruct((batch * seq_len, heads * dv), BF16),
                   jax.ShapeDtypeStruct((batch, heads, dk, dv), F32),
                   jax.ShapeDtypeStruct((batch, heads, dk), F32),
                   jax.ShapeDtypeStruct((batch, 1, heads), F32)],
        compiler_params=pltpu.CompilerParams(
            dimension_semantics=("arbitrary", "arbitrary"), vmem_limit_bytes=VMEM_LIMIT_BYTES),
        name="mlstm",
    )(*args)
    return o, (c_new, n_new, m_new.reshape(batch, heads))


def _ret_kernel(*refs, heads, dk, dv, has_init):
    it = iter(refs)
    cos_ref, sin_ref, z_ref, norm_ref = next(it), next(it), next(it), next(it)
    s0_ref = next(it) if has_init else None
    o_ref, s_ref, dec_scr = next(it), next(it), next(it)
    ch = pl.program_id(1)
    L = z_ref.shape[0]
    half = dk // 2

    @pl.when(ch == 0)
    def _():
        s_ref[...] = s0_ref[...] if has_init else jnp.zeros_like(s_ref)
        rel = (lax.broadcasted_iota(jnp.int32, (L, L), 0) - lax.broadcasted_iota(jnp.int32, (L, L), 1)).astype(F32)
        for h in range(heads):
            lg = math.log1p(-2.0 ** (-5.0 - h))
            dec_scr[h] = jnp.where(rel >= 0, jnp.exp(jnp.maximum(rel, 0.0) * lg), 0.0)

    cos, sin = cos_ref[...], sin_ref[...]
    t_col = lax.broadcasted_iota(jnp.int32, (L, 1), 0).astype(F32)
    koff, voff, goff = heads * dk, 2 * heads * dk, 2 * heads * dk + heads * dv

    def rope(x):
        x1, x2 = x[:, :half], x[:, half:]
        return jnp.concatenate([x1 * cos - x2 * sin, x1 * sin + x2 * cos], axis=-1)

    for h in range(heads):
        lg = math.log1p(-2.0 ** (-5.0 - h))
        q = rope(z_ref[:, h * dk:(h + 1) * dk].astype(F32)) * dk ** -0.5
        k = rope(z_ref[:, koff + h * dk:koff + (h + 1) * dk].astype(F32))
        v = z_ref[:, voff + h * dv:voff + (h + 1) * dv]
        gate = z_ref[:, goff + h * dv:goff + (h + 1) * dv].astype(F32)
        qb = q.astype(BF16)
        a = _dot_nt(qb, k.astype(BF16)) * dec_scr[h]
        s_old = s_ref[0, h]
        o = _dot(a.astype(BF16), v) + jnp.exp((t_col + 1.0) * lg) * _dot(qb, s_old.astype(BF16))
        kd = k * jnp.exp((L - 1.0 - t_col) * lg)
        s_ref[0, h] = math.exp(L * lg) * s_old + _dot_tn(kd.astype(BF16), v)
        y = _silu(gate) * _rms(o, norm_ref[...])
        o_ref[:, h * dv:(h + 1) * dv] = y.astype(o_ref.dtype)


def _retention(z, norm, init, layer, pos0, *, batch, seq_len, heads, dk, dv):
    L = min(RET_CHUNK, seq_len)
    nc = seq_len // L
    has_init = init is not None
    half = dk // 2
    inv = ROPE_BASE ** (-jnp.arange(half, dtype=F32) / half)
    ang = (pos0 + jnp.arange(seq_len)).astype(F32)[:, None] * inv[None]
    row = lambda b, c: (b * nc + c, 0)
    in_specs = [pl.BlockSpec((L, half), lambda b, c: (c, 0)),
                pl.BlockSpec((L, half), lambda b, c: (c, 0)),
                pl.BlockSpec((L, z.shape[1]), row),
                pl.BlockSpec((1, dv), lambda b, c: (0, 0))]
    args = [jnp.cos(ang), jnp.sin(ang), z, norm.reshape(1, dv).astype(F32)]
    state_spec = pl.BlockSpec((1, heads, dk, dv), lambda b, c: (b, 0, 0, 0))
    if has_init:
        in_specs.append(pl.BlockSpec((None, 1, heads, dk, dv), lambda b, c: (layer, b, 0, 0, 0)))
        args.append(init)
    o, s_new = pl.pallas_call(
        functools.partial(_ret_kernel, heads=heads, dk=dk, dv=dv, has_init=has_init),
        grid=(batch, nc),
        in_specs=in_specs,
        out_specs=[pl.BlockSpec((L, heads * dv), row), state_spec],
        out_shape=[jax.ShapeDtypeStruct((batch * seq_len, heads * dv), BF16),
                   jax.ShapeDtypeStruct((batch, heads, dk, dv), F32)],
        scratch_shapes=[pltpu.VMEM((heads, L, L), F32)],
        compiler_params=pltpu.CompilerParams(
            dimension_semantics=("arbitrary", "arbitrary"), vmem_limit_bytes=VMEM_LIMIT_BYTES),
        name="retention",
    )(*args)
    return o, s_new


def _gla_kernel(*refs, heads, dk, dv, has_init):
    it = iter(refs)
    z_ref, zlr_ref, w2_ref, bg_ref, norm_ref = next(it), next(it), next(it), next(it), next(it)
    s0_ref = next(it) if has_init else None
    o_ref, s_ref, st_scr = next(it), next(it), next(it)
    ch = pl.program_id(1)
    nch = pl.num_programs(1)
    L = z_ref.shape[0]

    @pl.when(ch == 0)
    def _():
        for h in range(heads):
            st_scr[h] = s0_ref[0, h].T if has_init else jnp.zeros((dv, dk), F32)

    la = _log_sigmoid(_dot(zlr_ref[...].astype(BF16), w2_ref[...]) + bg_ref[...]) / GLA_TAU
    b = _cumsum_rows(la)
    sub = min(GLA_SUB, L)
    tri, _ = _tri_masks(sub)
    koff, voff, roff = heads * dk, 2 * heads * dk, 2 * heads * dk + heads * dv
    hs = range(heads)
    bh = [b[:, h * dk:(h + 1) * dk] for h in hs]
    bl = [x[L - 1:L, :] for x in bh]
    q = [z_ref[:, h * dk:(h + 1) * dk].astype(F32) * dk ** -0.5 for h in hs]
    k = [z_ref[:, koff + h * dk:koff + (h + 1) * dk].astype(F32) for h in hs]
    v = [z_ref[:, voff + h * dv:voff + (h + 1) * dv] for h in hs]
    st_old = [st_scr[h] for h in hs]
    sb = [x.astype(BF16) for x in st_old]
    parts = [[] for _ in hs]
    for c in range(L // sub):
        rows = slice(c * sub, (c + 1) * sub)
        ref = [x[c * sub - 1:c * sub, :] if c else jnp.zeros_like(x[0:1, :]) for x in bh]
        qd = [(q[h][rows] * jnp.exp(bh[h][rows] - ref[h])).astype(BF16) for h in hs]
        kd = [(k[h][rows] * jnp.exp(ref[h] - bh[h][rows])).astype(BF16) for h in hs]
        o = [_dot(jnp.where(tri, _dot_nt(qd[h], kd[h]), 0.0).astype(BF16), v[h][rows]) for h in hs]
        for c2 in range(c):
            rows2 = slice(c2 * sub, (c2 + 1) * sub)
            ke = [(k[h][rows2] * jnp.exp(ref[h] - bh[h][rows2])).astype(BF16) for h in hs]
            o = [o[h] + _dot(_dot_nt(qd[h], ke[h]).astype(BF16), v[h][rows2]) for h in hs]
        o = [o[h] + _dot_nt((q[h][rows] * jnp.exp(bh[h][rows])).astype(BF16), sb[h]) for h in hs]
        for h in hs:
            parts[h].append(o[h])
    for h in hs:
        ke = (k[h] * jnp.exp(bl[h] - bh[h])).astype(BF16)
        st_scr[h] = jnp.exp(bl[h]) * st_old[h] + _dot_tn(v[h], ke)
    for h in hs:
        o = parts[h][0] if len(parts[h]) == 1 else jnp.concatenate(parts[h], axis=0)
        r = z_ref[:, roff + h * dv:roff + (h + 1) * dv].astype(F32)
        y = _silu(r) * _rms(o, norm_ref[...])
        o_ref[:, h * dv:(h + 1) * dv] = y.astype(o_ref.dtype)

    @pl.when(ch == nch - 1)
    def _():
        for h in range(heads):
            s_ref[0, h] = st_scr[h].T


def _gla(z, zlr, w_gate2, b_gate, norm, init, layer, *, batch, seq_len, heads, dk, dv):
    L = min(GLA_CHUNK, seq_len)
    nc = seq_len // L
    has_init = init is not None
    hk = heads * dk
    w2 = jnp.zeros((LANES, hk), BF16).at[:GLA_RANK].set(w_gate2.astype(BF16))
    row = lambda b, c: (b * nc + c, 0)
    in_specs = [pl.BlockSpec((L, z.shape[1]), row),
                pl.BlockSpec((L, LANES), row),
                pl.BlockSpec((LANES, hk), lambda b, c: (0, 0)),
                pl.BlockSpec((1, hk), lambda b, c: (0, 0)),
                pl.BlockSpec((1, dv), lambda b, c: (0, 0))]
    args = [z, zlr, w2, b_gate.reshape(1, hk).astype(F32), norm.reshape(1, dv).astype(F32)]
    state_spec = pl.BlockSpec((1, heads, dk, dv), lambda b, c: (b, 0, 0, 0))
    if has_init:
        in_specs.append(pl.BlockSpec((None, 1, heads, dk, dv), lambda b, c: (layer, b, 0, 0, 0)))
        args.append(init)
    o, s_new = pl.pallas_call(
        functools.partial(_gla_kernel, heads=heads, dk=dk, dv=dv, has_init=has_init),
        grid=(batch, nc),
        in_specs=in_specs,
        out_specs=[pl.BlockSpec((L, heads * dv), row), state_spec],
        out_shape=[jax.ShapeDtypeStruct((batch * seq_len, heads * dv), BF16),
                   jax.ShapeDtypeStruct((batch, heads, dk, dv), F32)],
        scratch_shapes=[pltpu.VMEM((heads, dv, dk), F32)],
        compiler_params=pltpu.CompilerParams(
            dimension_semantics=("arbitrary", "arbitrary"), vmem_limit_bytes=VMEM_LIMIT_BYTES),
        name="gla",
    )(*args)
    return o, s_new


def _gdn_kernel(*refs, qk_heads, v_heads, d, has_init, chunk, lookahead):
    it = iter(refs)
    z_ref, zba_ref = next(it), next(it)
    zn_ref, zban_ref = (next(it), next(it)) if lookahead else (None, None)
    cw_ref, aneg_ref, dtb_ref, norm_ref = (next(it) for _ in range(4))
    if has_init:
        s0_ref, buf0_ref = next(it), next(it)
    o_ref, s_ref = next(it), next(it)
    xs_scr = next(it)
    sets = [tuple(next(it) for _ in range(4)) for _ in range(2 if lookahead else 1)]
    blk_i = pl.program_id(1)
    L = chunk
    rep = v_heads // qk_heads
    conv_ch = (2 * qk_heads + v_heads) * d
    width = cw_ref.shape[0]
    hist = GDN_CONV_HISTORY

    rows = lax.broadcasted_iota(jnp.int32, ((width - 1) * L, L + 2 * hist), 0)
    col = lax.broadcasted_iota(jnp.int32, ((width - 1) * L, L + 2 * hist), 1)
    src = rows % L - (rows // L + 1)
    hi_col = jnp.where(src >= 0, src, src + L + hist)
    lo_col = jnp.where(src >= 0, -1, src + L + 2 * hist)
    shift = jnp.where((col == hi_col) | (col == lo_col), 1.0, 0.0).astype(BF16)

    def prep(zsrc, basrc, r0, scr):
        qkv_scr, gc_scr, gr_scr, bc_scr = scr
        ba = basrc[r0:r0 + L, :]
        lane = lax.broadcasted_iota(jnp.int32, ba.shape, 1)
        beta = _sigmoid(ba)
        la = aneg_ref[...] * _softplus(ba + dtb_ref[...])
        g = _cumsum_rows(jnp.where(lane >= v_heads, la, 0.0))
        gt = _transpose_rows(g)
        for j in range(v_heads):
            gc_scr[j] = jnp.broadcast_to(g[:, v_heads + j:v_heads + j + 1], (L, LANES))
            bc_scr[j] = jnp.broadcast_to(beta[:, j:j + 1], (L, LANES))
        for a in range(qk_heads):
            pair_row = jnp.concatenate(
                [gt[v_heads + rep * a + r:v_heads + rep * a + r + 1, :] for r in range(rep)], axis=-1)
            gr_scr[a] = jnp.broadcast_to(pair_row, (SUBLANES, rep * L))
        cb = GDN_CONV_BLOCK
        for blk in range(conv_ch // cb):
            sl = slice(blk * cb, (blk + 1) * cb)
            cur = zsrc[r0:r0 + L, sl]
            xs_scr[0:L, sl] = cur
            sh = _dot(shift, xs_scr[:, sl])
            w = cw_ref[:, sl]
            acc = cur.astype(F32) * w[width - 1:width]
            for t in range(1, width):
                acc = acc + sh[(t - 1) * L:t * L] * w[width - 1 - t:width - t]
            xs_scr[L:L + hist, sl] = zsrc[r0 + L - hist:r0 + L, sl]
            xs_scr[L + hist:L + 2 * hist, sl] = jnp.zeros((hist, cb), BF16)
            x = _silu(acc)
            for c in range(cb // d):
                head = blk * (cb // d) + c
                xh = x[:, c * d:(c + 1) * d]
                if head < 2 * qk_heads:
                    xh = xh * lax.rsqrt(jnp.sum(xh * xh, axis=-1, keepdims=True) + EPS)
                    if head < qk_heads:
                        xh = xh * d ** -0.5
                qkv_scr[:, head * d:(head + 1) * d] = xh

    assert rep == 2
    lane2 = lax.broadcasted_iota(jnp.int32, (L, 2 * L), 1)
    row2 = lax.broadcasted_iota(jnp.int32, (L, 2 * L), 0)
    left = lane2 < L
    tcol = jnp.where(left, lane2, lane2 - L)
    causal, strict = row2 >= tcol, row2 > tcol
    eye = jnp.where(row2 == tcol, 1.0, 0.0)

    def blockdiag(x):
        return jnp.concatenate([jnp.where(left, x, 0.0), jnp.where(left, 0.0, x)], axis=0).astype(BF16)

    def pair_inverses(mats):
        ps = [eye - a for a in mats]
        pws = [_dot(a.astype(BF16), blockdiag(a)) for a in mats]
        covered = 2
        while covered < L:
            if 2 * covered < L:
                both = [_dot(jnp.concatenate([p, pw], axis=0).astype(BF16), blockdiag(pw)) for p, pw in zip(ps, pws)]
                ps = [p + b[:L] for p, b in zip(ps, both)]
                pws = [b[L:] for b in both]
            else:
                ps = [p + _dot(p.astype(BF16), blockdiag(pw)) for p, pw in zip(ps, pws)]
            covered *= 2
        return ps

    gv = GDN_V_HEADS_PER_STEP
    gk = gv // rep

    def head_group(grp, r0, scr):
        qkv_scr, gc_scr, gr_scr, bc_scr = scr
        out_rows = slice(r0, r0 + L)

        def cols(base):
            return slice(base * d, (base + 1) * d)
        pairs = range(gk)
        heads = range(gv)
        js = [grp * gv + r for r in heads]
        k = lambda a: qkv_scr[:, cols(qk_heads + grp * gk + a)]
        g_col = lambda r: gc_scr[js[r]]
        b_col = lambda r: bc_scr[js[r]]
        qb = [qkv_scr[:, cols(grp * gk + a)].astype(BF16) for a in pairs]
        kb = [k(a).astype(BF16) for a in pairs]
        kb2 = [jnp.concatenate([x, x], axis=0) for x in kb]
        kk = [_dot_nt(kb[a], kb2[a]) for a in pairs]
        qk = [_dot_nt(qb[a], kb2[a]) for a in pairs]
        g_pair = [jnp.where(left, g_col(2 * a)[:, :2 * L], g_col(2 * a + 1)[:, :2 * L]) for a in pairs]
        b_pair = [jnp.where(left, b_col(2 * a)[:, :2 * L], b_col(2 * a + 1)[:, :2 * L]) for a in pairs]
        g_row = [gr_scr[grp * gk + a][0:1, :] for a in pairs]
        dmat = [jnp.exp(jnp.where(causal, g_pair[a] - g_row[a], -jnp.inf)) for a in pairs]
        tinv = pair_inverses([jnp.where(strict, b_pair[a] * dmat[a] * kk[a], 0.0) for a in pairs])
        rhs = [jnp.concatenate([b_col(r) * qkv_scr[:, cols(2 * qk_heads + js[r])],
                                b_col(r) * jnp.exp(g_col(r)) * k(r // rep)], axis=-1).astype(BF16) for r in heads]
        sol = [_dot(blockdiag(tinv[a]), jnp.concatenate([rhs[2 * a], rhs[2 * a + 1]], axis=0)) for a in pairs]
        sol = [sol[r // rep][(r % rep) * L:(r % rep + 1) * L] for r in heads]
        ss = [_dot(jnp.concatenate([sol[r][:, d:].astype(BF16), qb[r // rep]], axis=0),
                   s_ref[0, js[r]].astype(BF16)) for r in heads]
        ub = [(sol[r][:, :d] - ss[r][:L]).astype(BF16) for r in heads]
        ou = [_dot(blockdiag(qk[a] * dmat[a]), jnp.concatenate([ub[2 * a], ub[2 * a + 1]], axis=0)) for a in pairs]
        for r in heads:
            o = jnp.exp(g_col(r)) * ss[r][L:] + ou[r // rep][(r % rep) * L:(r % rep + 1) * L]
            zg = z_ref[out_rows, cols(conv_ch // d + js[r])].astype(F32)
            y = _rms(o, norm_ref[...]) * _silu(zg)
            o_ref[out_rows, cols(js[r])] = y.astype(o_ref.dtype)
        for r in heads:
            g_last = g_col(r)[L - 1:L, :]
            kd = (jnp.exp(g_last - g_col(r)) * k(r // rep)).astype(BF16)
            s_ref[0, js[r]] = jnp.exp(g_last) * s_ref[0, js[r]] + _dot_tn(kd, ub[r])

    assert v_heads % gv == 0 and gv % rep == 0

    def heads(r0, scr):
        for grp in range(v_heads // gv):
            head_group(grp, r0, scr)

    @pl.when(blk_i == 0)
    def _():
        s_ref[...] = s0_ref[...] if has_init else jnp.zeros_like(s_ref)
        prev = jnp.zeros((hist, conv_ch), F32)
        if has_init:
            prev = jnp.concatenate([prev[:hist - width + 1], buf0_ref[0]], axis=0)
        hi = prev.astype(BF16)
        xs_scr[L:L + hist, :] = hi
        xs_scr[L + hist:L + 2 * hist, :] = (prev - hi.astype(F32)).astype(BF16)
        if lookahead:
            prep(z_ref, zba_ref, 0, sets[0])

    if lookahead:
        prep(z_ref, zba_ref, L, sets[1])
        heads(0, sets[0])
        prep(zn_ref, zban_ref, 0, sets[0])
        heads(L, sets[1])
    else:
        prep(z_ref, zba_ref, 0, sets[0])
        heads(0, sets[0])


def _gdn(z, zba, conv_w, a_log, dt_bias, norm, init, layer, *, batch, seq_len, qk_heads, v_heads, d):
    L = min(CHUNK, seq_len)
    nc = seq_len // L
    lookahead = nc >= 2 and nc % 2 == 0
    cpb = 2 if lookahead else 1
    nb = nc // cpb
    has_init = init is not None
    conv_ch = (2 * qk_heads + v_heads) * d
    width = conv_w.shape[0]
    lane_pad = lambda x: jnp.zeros((1, LANES), F32).at[0, v_heads:2 * v_heads].set(x.astype(F32))
    row = lambda b, i: (b * nb + i, 0)
    nxt = lambda b, i: (b * nc + jnp.minimum(cpb * (i + 1), nc - 1), 0)
    const = lambda b, i: (0, 0)
    in_specs = [pl.BlockSpec((cpb * L, z.shape[1]), row), pl.BlockSpec((cpb * L, LANES), row)]
    args = [z, zba]
    if lookahead:
        in_specs += [pl.BlockSpec((L, z.shape[1]), nxt), pl.BlockSpec((L, LANES), nxt)]
        args += [z, zba]
    in_specs += [pl.BlockSpec((width, conv_ch), const),
                 pl.BlockSpec((1, LANES), const),
                 pl.BlockSpec((1, LANES), const),
                 pl.BlockSpec((1, d), const)]
    args += [conv_w.astype(F32), lane_pad(-jnp.exp(a_log)), lane_pad(dt_bias), norm.reshape(1, d).astype(F32)]
    state_spec = pl.BlockSpec((1, v_heads, d, d), lambda b, i: (b, 0, 0, 0))
    if has_init:
        in_specs += [pl.BlockSpec((None, 1, v_heads, d, d), lambda b, i: (layer, b, 0, 0, 0)),
                     pl.BlockSpec((None, 1, width - 1, conv_ch), lambda b, i: (layer, b, 0, 0))]
        args += [init[0], init[1]]
    scratch_set = [pltpu.VMEM((L, conv_ch), F32),
                   pltpu.VMEM((v_heads, L, LANES), F32),
                   pltpu.VMEM((qk_heads, SUBLANES, v_heads // qk_heads * L), F32),
                   pltpu.VMEM((v_heads, L, LANES), F32)]
    o, s_new = pl.pallas_call(
        functools.partial(_gdn_kernel, qk_heads=qk_heads, v_heads=v_heads, d=d, has_init=has_init,
                          chunk=L, lookahead=lookahead),
        grid=(batch, nb),
        in_specs=in_specs,
        out_specs=[pl.BlockSpec((cpb * L, v_heads * d), row), state_spec],
        out_shape=[jax.ShapeDtypeStruct((batch * seq_len, v_heads * d), BF16),
                   jax.ShapeDtypeStruct((batch, v_heads, d, d), F32)],
        scratch_shapes=[pltpu.VMEM((L + 2 * GDN_CONV_HISTORY, conv_ch), BF16)] + scratch_set * cpb,
        compiler_params=pltpu.CompilerParams(
            dimension_semantics=("arbitrary", "arbitrary"), vmem_limit_bytes=VMEM_LIMIT_BYTES),
        name="gdn",
    )(*args)
    return o, s_new


def _pad_cols(w, n):
    return jnp.zeros((w.shape[0], n), F32).at[:, :w.shape[1]].set(w.astype(F32))


def _trunk(x, states, pos0, p, *, batch, seq_len):
    m, dm = x.shape
    has_init = states is not None
    if has_init:
        mC, mn, mm, rS, gS, dS, dconv, fconv = states
    depth = p['norm_mix'].shape[0]
    outs = [[] for _ in range(8)]
    for i in range(depth):
        kind, j = i % 4, i // 4
        g_mix = p['norm_mix'][i]
        if kind == 0:
            heads, dk, dv = p['mlstm_n_shape']
            w_in = p['mlstm_w_in'][j]
            nm = 2 * heads * dk + 2 * heads * dv
            z, zg = _matmul(x, w_in[:, :nm].astype(BF16), norm_g=g_mix, gate_w=_pad_cols(w_in[:, nm:], LANES),
                            name="mlstm_in")
            bias = _pad_cols(p['mlstm_b_gates'][j][None], LANES)
            init = (mC, mn, mm) if has_init else None
            a, (c_, n_, m_) = _mlstm(z, zg, bias, p['mlstm_norm'][j], init, j,
                                     batch=batch, seq_len=seq_len, heads=heads, dk=dk, dv=dv)
            outs[0].append(c_); outs[1].append(n_); outs[2].append(m_)
            w_out = p['mlstm_w_out'][j]
        elif kind == 1:
            heads, dk, dv = p['ret_shape']
            z = _matmul(x, p['ret_w_in'][j].astype(BF16), norm_g=g_mix, name="ret_in")
            a, s_ = _retention(z, p['ret_norm'][j], rS if has_init else None, j, pos0,
                               batch=batch, seq_len=seq_len, heads=heads, dk=dk, dv=dv)
            outs[3].append(s_)
            w_out = p['ret_w_out'][j]
        elif kind == 2:
            heads, dk, dv = p['gla_shape']
            w_in = p['gla_w_in'][j]
            nm = 2 * heads * dk + 2 * heads * dv
            z, zlr = _matmul(x, w_in[:, :nm].astype(BF16), norm_g=g_mix, gate_w=_pad_cols(w_in[:, nm:], LANES),
                             name="gla_in")
            a, s_ = _gla(z, zlr, p['gla_w_gate2'][j], p['gla_b_gate'][j], p['gla_norm'][j],
                         gS if has_init else None, j,
                         batch=batch, seq_len=seq_len, heads=heads, dk=dk, dv=dv)
            outs[4].append(s_)
            w_out = p['gla_w_out'][j]
        else:
            qk_heads, v_heads, d = p['gdn_shape']
            w_in = p['gdn_w_in'][j]
            conv_ch = (2 * qk_heads + v_heads) * d
            nm = conv_ch + v_heads * d
            width = p['gdn_conv_w'].shape[1]
            z, zba = _matmul(x, w_in[:, :nm].astype(BF16), norm_g=g_mix, gate_w=_pad_cols(w_in[:, nm:], LANES),
                             name="gdn_in")
            init = (dS, dconv) if has_init else None
            a, s_ = _gdn(z, zba, p['gdn_conv_w'][j], p['gdn_A_log'][j], p['gdn_dt_bias'][j], p['gdn_norm'][j],
                         init, j, batch=batch, seq_len=seq_len, qk_heads=qk_heads, v_heads=v_heads, d=d)
            outs[5].append(s_)
            outs[6].append(z.reshape(batch, seq_len, nm)[:, seq_len - (width - 1):, :conv_ch].astype(F32))
            w_out = p['gdn_w_out'][j]
        x = _matmul(a, w_out.astype(BF16), residual=x, out_dtype=F32, name="mix_out")

        f = p['ffn_w_gate'].shape[2]
        fw = p['ffn_conv_w'].shape[1]
        buf8 = jnp.zeros((batch, SUBLANES, f), F32)
        if has_init:
            buf8 = buf8.at[:, SUBLANES - (fw - 1):].set(fconv[i].astype(F32))
        act, tail = _ffn_in(x, p['norm_ffn'][i], p['ffn_w_gate_bf16'], p['ffn_w_up_bf16'], i,
                            buf8, p['ffn_conv_w'][i], p['ffn_conv_b'][i], seq_len=seq_len)
        outs[7].append(tail[:, SUBLANES - (fw - 1):])
        x = _matmul(act, p['ffn_w_down_bf16'], layer=i, residual=x, out_dtype=F32, name="ffn_out")
    y = _final_norm(x, p['norm_final'])
    return y.reshape(batch, seq_len, dm), tuple(o[0][None] if len(o) == 1 else jnp.stack(o) for o in outs)


def kernel(x_prompt, x_sample, state_mlstm_C, state_mlstm_n, state_mlstm_m, state_ret_S, state_gla_S, state_gdn_S, state_gdn_conv, state_ffn_conv, norm_mix, norm_ffn, norm_final, mlstm_w_in, mlstm_b_gates, mlstm_norm, mlstm_w_out, ret_w_in, ret_norm, ret_w_out, gla_w_in, gla_w_gate2, gla_b_gate, gla_norm, gla_w_out, gdn_w_in, gdn_conv_w, gdn_A_log, gdn_dt_bias, gdn_norm, gdn_w_out, ffn_w_gate, ffn_w_up, ffn_conv_w, ffn_conv_b, ffn_w_down):
    p = dict(norm_mix=norm_mix, norm_ffn=norm_ffn, norm_final=norm_final,
             mlstm_w_in=mlstm_w_in, mlstm_b_gates=mlstm_b_gates, mlstm_norm=mlstm_norm, mlstm_w_out=mlstm_w_out,
             ret_w_in=ret_w_in, ret_norm=ret_norm, ret_w_out=ret_w_out,
             gla_w_in=gla_w_in, gla_w_gate2=gla_w_gate2, gla_b_gate=gla_b_gate, gla_norm=gla_norm, gla_w_out=gla_w_out,
             gdn_w_in=gdn_w_in, gdn_conv_w=gdn_conv_w, gdn_A_log=gdn_A_log, gdn_dt_bias=gdn_dt_bias,
             gdn_norm=gdn_norm, gdn_w_out=gdn_w_out,
             ffn_w_gate=ffn_w_gate, ffn_w_up=ffn_w_up, ffn_conv_w=ffn_conv_w, ffn_conv_b=ffn_conv_b,
             ffn_w_down=ffn_w_down)
    for name in ('ffn_w_gate', 'ffn_w_up', 'ffn_w_down'):
        p[name + '_bf16'] = p[name].astype(BF16)
    p['mlstm_n_shape'] = state_mlstm_C.shape[2:]
    p['ret_shape'] = state_ret_S.shape[2:]
    p['gla_shape'] = state_gla_S.shape[2:]
    d = state_gdn_S.shape[-1]
    v_heads = state_gdn_S.shape[2]
    p['gdn_shape'] = ((state_gdn_conv.shape[-1] // d - v_heads) // 2, v_heads, d)

    bp, tp, dm = x_prompt.shape
    bs, ts, _ = x_sample.shape
    y_prompt, p_states = _trunk(x_prompt.reshape(bp * tp, dm), None, 0, p, batch=bp, seq_len=tp)
    cache = (state_mlstm_C, state_mlstm_n, state_mlstm_m, state_ret_S, state_gla_S, state_gdn_S,
             state_gdn_conv, state_ffn_conv)
    y_sample, s_states = _trunk(x_sample.reshape(bs * ts, dm), cache, PAST_LEN, p, batch=bs, seq_len=ts)
    return (y_prompt, y_sample) + p_states + s_states
```

```python
import functools
import math

import jax
import jax.numpy as jnp
from jax import lax
from jax.experimental import pallas as pl
from jax.experimental.pallas import tpu as pltpu

F32 = jnp.float32
BF16 = jnp.bfloat16
EPS = 1e-6
PAST_LEN = 2048
ROPE_BASE = 10000.0
GLA_TAU = 16.0
GLA_RANK = 16
CHUNK = 64
MLSTM_CHUNK = 256
RET_CHUNK = 256
GLA_CHUNK = 128
GLA_SUB = 64
GDN_V_HEADS_PER_STEP = 32
GDN_CONV_HISTORY = 16
GDN_CONV_BLOCK = 256
FFN_ROW_CHUNK = 256
LANES = 128
SUBLANES = 8
VMEM_LIMIT_BYTES = 56 * 1024 * 1024


def _sigmoid(x):
    return 1.0 / (1.0 + jnp.exp(-x))


def _silu(x):
    return x * _sigmoid(x)


def _log1p_exp_neg_abs(x):
    return jnp.log(1.0 + jnp.exp(-jnp.abs(x)))


def _log_sigmoid(x):
    return jnp.minimum(x, 0.0) - _log1p_exp_neg_abs(x)


def _softplus(x):
    return jnp.maximum(x, 0.0) + _log1p_exp_neg_abs(x)


def _rms(x, g):
    return x * lax.rsqrt(jnp.mean(x * x, axis=-1, keepdims=True) + EPS) * g


def _dot(a, b):
    return jnp.dot(a, b, preferred_element_type=F32)


def _dot_nt(a, b):
    return lax.dot_general(a, b, (((1,), (1,)), ((), ())), preferred_element_type=F32)


def _dot_tn(a, b):
    return lax.dot_general(a, b, (((0,), (0,)), ((), ())), preferred_element_type=F32)


def _split3(x):
    hi = x.astype(BF16)
    r = x - hi.astype(F32)
    mid = r.astype(BF16)
    lo = (r - mid.astype(F32)).astype(BF16)
    return hi, mid, lo


def _cumsum_rows(x):
    n = x.shape[0]
    r = lax.broadcasted_iota(jnp.int32, (n, n), 0)
    c = lax.broadcasted_iota(jnp.int32, (n, n), 1)
    tril = jnp.where(r >= c, 1.0, 0.0).astype(BF16)
    hi, mid, lo = _split3(x)
    return _dot(tril, hi) + _dot(tril, mid) + _dot(tril, lo)


def _transpose_rows(x):
    n = x.shape[0]
    if n < LANES:
        x = jnp.concatenate([x, jnp.zeros((LANES - n, x.shape[1]), x.dtype)], axis=0)
    return x.T[:, :n]


def _causal_taps(x3, prev, cw):
    width = cw.shape[0]
    t8 = lax.broadcasted_iota(jnp.int32, (x3.shape[0], SUBLANES, x3.shape[2]), 1)
    acc = x3 * cw[width - 1:width]
    for s in range(1, width):
        sh = pltpu.roll(x3, s, 1)
        head = sh[:, :SUBLANES]
        for e in range(s):
            head = jnp.where(t8 == e, prev[s - 1 - e], head)
        sh = jnp.concatenate([head, sh[:, SUBLANES:]], axis=1)
        acc = acc + sh * cw[width - 1 - s:width - s]
    return acc


def _tri_masks(n):
    r = lax.broadcasted_iota(jnp.int32, (n, n), 0)
    c = lax.broadcasted_iota(jnp.int32, (n, n), 1)
    return r >= c, r > c


def _mm_kernel(*refs, has_norm, has_res, has_gate):
    it = iter(refs)
    x_ref = next(it)
    g_ref = next(it) if has_norm else None
    w_ref = next(it)
    wgate_ref = next(it) if has_gate else None
    r_ref = next(it) if has_res else None
    o_ref = next(it)
    ogate_ref = next(it) if has_gate else None
    h_scr = next(it) if has_norm else None
    if has_norm:
        @pl.when(pl.program_id(1) == 0)
        def _():
            h = _rms(x_ref[...].astype(F32), g_ref[...])
            h_scr[...] = h.astype(BF16)
            if has_gate:
                both = _dot(h.astype(BF16), wgate_ref[...])
                ogate_ref[...] = both[:, :LANES] + both[:, LANES:]
        h = h_scr[...]
    else:
        h = x_ref[...]
    acc = _dot(h, w_ref[...])
    if has_res:
        acc = acc + r_ref[...]
    o_ref[...] = acc.astype(o_ref.dtype)


def _pick(n, cands):
    for c in cands:
        if n % c == 0:
            return c
    raise ValueError(f"no tile for {n}")


def _matmul(x, w, *, layer=None, norm_g=None, gate_w=None, residual=None, out_dtype=BF16, bm=1024, bn=None,
            name="mm"):
    m, k = x.shape
    n = w.shape[-1]
    bm = min(bm, m)
    if bn is None:
        bn = _pick(n, (1024, 512, 256, 128)) if k <= 4096 else _pick(n, (512, 256, 128))
    assert m % bm == 0 and n % bn == 0
    has_norm, has_res, has_gate = norm_g is not None, residual is not None, gate_w is not None
    assert has_norm or not has_gate
    in_specs = [pl.BlockSpec((bm, k), lambda i, j: (i, 0))]
    args = [x]
    if has_norm:
        in_specs.append(pl.BlockSpec((1, k), lambda i, j: (0, 0)))
        args.append(norm_g.reshape(1, k).astype(F32))
    if layer is None:
        in_specs.append(pl.BlockSpec((k, bn), lambda i, j: (0, j)))
    else:
        in_specs.append(pl.BlockSpec((None, k, bn), lambda i, j: (layer, 0, j)))
    args.append(w)
    if has_gate:
        g_hi = gate_w.astype(BF16)
        g_lo = (gate_w - g_hi.astype(F32)).astype(BF16)
        in_specs.append(pl.BlockSpec((k, 2 * LANES), lambda i, j: (0, 0)))
        args.append(jnp.concatenate([g_hi, g_lo], axis=1))
    if has_res:
        in_specs.append(pl.BlockSpec((bm, bn), lambda i, j: (i, j)))
        args.append(residual)
    out_specs = [pl.BlockSpec((bm, bn), lambda i, j: (i, j))]
    out_shape = [jax.ShapeDtypeStruct((m, n), out_dtype)]
    if has_gate:
        out_specs.append(pl.BlockSpec((bm, LANES), lambda i, j: (i, 0)))
        out_shape.append(jax.ShapeDtypeStruct((m, LANES), F32))
    outs = pl.pallas_call(
        functools.partial(_mm_kernel, has_norm=has_norm, has_res=has_res, has_gate=has_gate),
        grid=(m // bm, n // bn),
        in_specs=in_specs,
        out_specs=out_specs,
        out_shape=out_shape,
        scratch_shapes=[pltpu.VMEM((bm, k), BF16)] if has_norm else [],
        compiler_params=pltpu.CompilerParams(
            dimension_semantics=("arbitrary", "arbitrary"), vmem_limit_bytes=VMEM_LIMIT_BYTES),
        name=name,
    )(*args)
    return outs if has_gate else outs[0]


def _final_norm_kernel(x_ref, g_ref, o_ref):
    o_ref[...] = _rms(x_ref[...], g_ref[...])


def _final_norm(x, g, *, bm=512):
    m, k = x.shape
    bm = min(bm, m)
    return pl.pallas_call(
        _final_norm_kernel,
        grid=(m // bm,),
        in_specs=[pl.BlockSpec((bm, k), lambda i: (i, 0)), pl.BlockSpec((1, k), lambda i: (0, 0))],
        out_specs=pl.BlockSpec((bm, k), lambda i: (i, 0)),
        out_shape=jax.ShapeDtypeStruct((m, k), F32),
        compiler_params=pltpu.CompilerParams(
            dimension_semantics=("arbitrary",), vmem_limit_bytes=VMEM_LIMIT_BYTES),
        name="final_norm",
    )(x, g.reshape(1, k).astype(F32))


def _ffn_in_kernel(x_ref, g_ref, wg_ref, wu_ref, buf_ref, cw_ref, cb_ref, act_ref, tail_ref, h_scr, halo_scr,
                   *, tiles_per_seq, seq_rows):
    i, j = pl.program_id(0), pl.program_id(1)
    bm, bn = act_ref.shape

    @pl.when(j == 0)
    def _():
        h_scr[...] = _rms(x_ref[...], g_ref[...]).astype(BF16)

    nseq = bm // seq_rows
    first = (i % tiles_per_seq) == 0
    halo = jnp.where(first, buf_ref[...], halo_scr[j])
    prev2, prev1 = halo[:, 6:7], halo[:, 7:8]
    cw, cb = cw_ref[...], cb_ref[...]
    wg, wu = wg_ref[...], wu_ref[...]
    rc = min(FFN_ROW_CHUNK, bm)
    spc = max(rc // seq_rows, 1)
    tt = rc // spc
    tails = []
    for r in range(bm // rc):
        rows = slice(r * rc, (r + 1) * rc)
        h = h_scr[rows, :]
        g3 = _dot(h, wg).reshape(spc, tt, bn)
        u3 = _dot(h, wu).reshape(spc, tt, bn)
        pv = [prev1, prev2] if nseq == 1 else [prev1[r * spc:(r + 1) * spc], prev2[r * spc:(r + 1) * spc]]
        gc = _causal_taps(g3, pv, cw) + cb
        act_ref[rows, :] = (_silu(gc) * u3).reshape(rc, bn).astype(act_ref.dtype)
        if nseq == 1:
            prev2, prev1 = g3[:, tt - 2:tt - 1], g3[:, tt - 1:tt]
            tails = [g3[:, tt - SUBLANES:]]
        else:
            tails.append(g3[:, tt - SUBLANES:])
    tail = tails[0] if len(tails) == 1 else jnp.concatenate(tails, axis=0)
    halo_scr[j] = tail
    tail_ref[...] = tail


def _ffn_in(x, norm_g, w_gate, w_up, layer, buf8, conv_w, conv_b, *, seq_len, bm=1024, bn=512):
    m, k = x.shape
    f = w_gate.shape[2]
    batch = m // seq_len
    bm = min(bm, m)
    assert f % bn == 0 and m % bm == 0 and (seq_len % bm == 0 or bm % seq_len == 0)
    nf = f // bn
    tiles_per_seq = max(seq_len // bm, 1)
    seq_rows = min(seq_len, bm)
    nseq = bm // seq_rows
    seq_blk = lambda i, j: (i // tiles_per_seq, 0, j)
    act, tails = pl.pallas_call(
        functools.partial(_ffn_in_kernel, tiles_per_seq=tiles_per_seq, seq_rows=seq_rows),
        grid=(m // bm, nf),
        in_specs=[
            pl.BlockSpec((bm, k), lambda i, j: (i, 0)),
            pl.BlockSpec((1, k), lambda i, j: (0, 0)),
            pl.BlockSpec((None, k, bn), lambda i, j: (layer, 0, j)),
            pl.BlockSpec((None, k, bn), lambda i, j: (layer, 0, j)),
            pl.BlockSpec((nseq, SUBLANES, bn), seq_blk),
            pl.BlockSpec((3, bn), lambda i, j: (0, j)),
            pl.BlockSpec((1, bn), lambda i, j: (0, j)),
        ],
        out_specs=[pl.BlockSpec((bm, bn), lambda i, j: (i, j)),
                   pl.BlockSpec((nseq, SUBLANES, bn), lambda i, j: (i, 0, j))],
        out_shape=[jax.ShapeDtypeStruct((m, f), BF16),
                   jax.ShapeDtypeStruct((m // bm * nseq, SUBLANES, f), F32)],
        scratch_shapes=[pltpu.VMEM((bm, k), BF16), pltpu.VMEM((nf, nseq, SUBLANES, bn), F32)],
        compiler_params=pltpu.CompilerParams(
            dimension_semantics=("arbitrary", "arbitrary"), vmem_limit_bytes=VMEM_LIMIT_BYTES),
        name="ffn_in",
    )(x, norm_g.reshape(1, k).astype(F32), w_gate, w_up, buf8, conv_w.astype(F32),
      conv_b.reshape(1, f).astype(F32))
    return act, tails.reshape(batch, tiles_per_seq, SUBLANES, f)[:, tiles_per_seq - 1]


def _mlstm_kernel(*refs, heads, dk, dv, has_init):
    it = iter(refs)
    z_ref, zg_ref, bias_ref, norm_ref = next(it), next(it), next(it), next(it)
    if has_init:
        c0_ref, n0_ref, m0_ref = next(it), next(it), next(it)
    o_ref, c_ref, n_ref, m_ref = next(it), next(it), next(it), next(it)
    ch = pl.program_id(1)
    L = z_ref.shape[0]

    @pl.when(ch == 0)
    def _():
        if has_init:
            c_ref[...] = c0_ref[...]
            n_ref[...] = n0_ref[...]
            m_ref[...] = m0_ref[...]
        else:
            c_ref[...] = jnp.zeros_like(c_ref)
            n_ref[...] = jnp.zeros_like(n_ref)
            m_ref[...] = jnp.zeros_like(m_ref)

    g = zg_ref[...] + bias_ref[...]
    lane = lax.broadcasted_iota(jnp.int32, g.shape, 1)
    u = jnp.where(lane < heads, g, _cumsum_rows(_log_sigmoid(g)))
    ut = _transpose_rows(u)
    tri, _ = _tri_masks(L)
    m_all = m_ref[0]
    m_lane = lax.broadcasted_iota(jnp.int32, m_all.shape, 1)
    qoff, koff, voff, ooff = 0, heads * dk, 2 * heads * dk, 2 * heads * dk + heads * dv
    hs = range(heads)
    ig_col = [u[:, h:h + 1] for h in hs]
    ig_row = [ut[h:h + 1, :] for h in hs]
    b_col = [u[:, heads + h:heads + h + 1] for h in hs]
    b_row = [ut[heads + h:heads + h + 1, :] for h in hs]
    m_prev = [m_all[:, h:h + 1] for h in hs]
    q = [(z_ref[:, qoff + h * dk:qoff + (h + 1) * dk].astype(F32) * dk ** -0.5).astype(BF16) for h in hs]
    k = [z_ref[:, koff + h * dk:koff + (h + 1) * dk] for h in hs]
    v = [z_ref[:, voff + h * dv:voff + (h + 1) * dv] for h in hs]
    c_old = [c_ref[0, h] for h in hs]
    n_old = [n_ref[0, h:h + 1, :] for h in hs]
    d = [jnp.where(tri, b_col[h] - b_row[h] + ig_row[h], -jnp.inf) for h in hs]
    li = [b_col[h] + m_prev[h] for h in hs]
    m_t = [jnp.maximum(li[h], jnp.max(d[h], axis=1, keepdims=True)) for h in hs]
    w_inter = [jnp.exp(li[h] - m_t[h]) for h in hs]
    a = [_dot_nt(q[h], k[h]) * jnp.exp(d[h] - m_t[h]) for h in hs]
    num = [_dot(a[h].astype(BF16), v[h]) + w_inter[h] * _dot(q[h], c_old[h].astype(BF16)) for h in hs]
    den = [jnp.sum(a[h], axis=1, keepdims=True)
           + w_inter[h] * jnp.sum(q[h].astype(F32) * n_old[h], axis=1, keepdims=True) for h in hs]
    hh = [num[h] / jnp.maximum(jnp.abs(den[h]), jnp.exp(-m_t[h])) for h in hs]
    for h in hs:
        m_new = m_t[h][L - 1:L, :]
        b_last = b_col[h][L - 1:L, :]
        w_k = jnp.exp(b_last - b_col[h] + ig_col[h] - m_new)
        dec = jnp.exp(b_last + m_prev[h] - m_new)
        kw = k[h].astype(F32) * w_k
        c_ref[0, h] = dec * c_old[h] + _dot_tn(kw.astype(BF16), v[h])
        n_ref[0, h:h + 1, :] = dec * n_old[h] + jnp.sum(kw, axis=0, keepdims=True)
        m_all = jnp.where(m_lane == h, m_new, m_all)
    for h in hs:
        og = z_ref[:, ooff + h * dv:ooff + (h + 1) * dv].astype(F32)
        y = _sigmoid(og) * _rms(hh[h], norm_ref[...])
        o_ref[:, h * dv:(h + 1) * dv] = y.astype(o_ref.dtype)
    m_ref[0] = m_all


def _mlstm(z, zg, bias, norm, init, layer, *, batch, seq_len, heads, dk, dv):
    L = min(MLSTM_CHUNK, seq_len)
    nc = seq_len // L
    has_init = init is not None
    row = lambda b, c: (b * nc + c, 0)
    in_specs = [pl.BlockSpec((L, z.shape[1]), row),
                pl.BlockSpec((L, LANES), row),
                pl.BlockSpec((1, LANES), lambda b, c: (0, 0)),
                pl.BlockSpec((1, dv), lambda b, c: (0, 0))]
    args = [z, zg, bias, norm.reshape(1, dv).astype(F32)]
    state_specs = [pl.BlockSpec((1, heads, dk, dv), lambda b, c: (b, 0, 0, 0)),
                   pl.BlockSpec((1, heads, dk), lambda b, c: (b, 0, 0)),
                   pl.BlockSpec((1, 1, heads), lambda b, c: (b, 0, 0))]
    if has_init:
        in_specs += [pl.BlockSpec((None, 1, heads, dk, dv), lambda b, c: (layer, b, 0, 0, 0)),
                     pl.BlockSpec((None, 1, heads, dk), lambda b, c: (layer, b, 0, 0)),
                     state_specs[2]]
        args += [init[0], init[1], init[2][layer].reshape(batch, 1, heads)]
    o, c_new, n_new, m_new = pl.pallas_call(
        functools.partial(_mlstm_kernel, heads=heads, dk=dk, dv=dv, has_init=has_init),
        grid=(batch, nc),
        in_specs=in_specs,
        out_specs=[pl.BlockSpec((L, heads * dv), row)] + state_specs,
        out_shape=[jax.ShapeDtypeStruct((batch * seq_len, heads * dv), BF16),
                   jax.ShapeDtypeStruct((batch, heads, dk, dv), F32),
                   jax.ShapeDtypeStruct((batch, heads, dk), F32),
                   jax.ShapeDtypeStruct((batch, 1, heads), F32)],
        compiler_params=pltpu.CompilerParams(
            dimension_semantics=("arbitrary", "arbitrary"), vmem_limit_bytes=VMEM_LIMIT_BYTES),
        name="mlstm",
    )(*args)
    return o, (c_new, n_new, m_new.reshape(batch, heads))


def _ret_kernel(*refs, heads, dk, dv, has_init):
    it = iter(refs)
    cos_ref, sin_ref, z_ref, norm_ref = next(it), next(it), next(it), next(it)
    s0_ref = next(it) if has_init else None
    o_ref, s_ref, dec_scr = next(it), next(it), next(it)
    ch = pl.program_id(1)
    L = z_ref.shape[0]
    half = dk // 2

    @pl.when(ch == 0)
    def _():
        s_ref[...] = s0_ref[...] if has_init else jnp.zeros_like(s_ref)
        rel = (lax.broadcasted_iota(jnp.int32, (L, L), 0) - lax.broadcasted_iota(jnp.int32, (L, L), 1)).astype(F32)
        for h in range(heads):
            lg = math.log1p(-2.0 ** (-5.0 - h))
            dec_scr[h] = jnp.where(rel >= 0, jnp.exp(jnp.maximum(rel, 0.0) * lg), 0.0)

    cos, sin = cos_ref[...], sin_ref[...]
    t_col = lax.broadcasted_iota(jnp.int32, (L, 1), 0).astype(F32)
    koff, voff, goff = heads * dk, 2 * heads * dk, 2 * heads * dk + heads * dv

    def rope(x):
        x1, x2 = x[:, :half], x[:, half:]
        return jnp.concatenate([x1 * cos - x2 * sin, x1 * sin + x2 * cos], axis=-1)

    for h in range(heads):
        lg = math.log1p(-2.0 ** (-5.0 - h))
        q = rope(z_ref[:, h * dk:(h + 1) * dk].astype(F32)) * dk ** -0.5
        k = rope(z_ref[:, koff + h * dk:koff + (h + 1) * dk].astype(F32))
        v = z_ref[:, voff + h * dv:voff + (h + 1) * dv]
        gate = z_ref[:, goff + h * dv:goff + (h + 1) * dv].astype(F32)
        qb = q.astype(BF16)
        a = _dot_nt(qb, k.astype(BF16)) * dec_scr[h]
        s_old = s_ref[0, h]
        o = _dot(a.astype(BF16), v) + jnp.exp((t_col + 1.0) * lg) * _dot(qb, s_old.astype(BF16))
        kd = k * jnp.exp((L - 1.0 - t_col) * lg)
        s_ref[0, h] = math.exp(L * lg) * s_old + _dot_tn(kd.astype(BF16), v)
        y = _silu(gate) * _rms(o, norm_ref[...])
        o_ref[:, h * dv:(h + 1) * dv] = y.astype(o_ref.dtype)


def _retention(z, norm, init, layer, pos0, *, batch, seq_len, heads, dk, dv):
    L = min(RET_CHUNK, seq_len)
    nc = seq_len // L
    has_init = init is not None
    half = dk // 2
    inv = ROPE_BASE ** (-jnp.arange(half, dtype=F32) / half)
    ang = (pos0 + jnp.arange(seq_len)).astype(F32)[:, None] * inv[None]
    row = lambda b, c: (b * nc + c, 0)
    in_specs = [pl.BlockSpec((L, half), lambda b, c: (c, 0)),
                pl.BlockSpec((L, half), lambda b, c: (c, 0)),
                pl.BlockSpec((L, z.shape[1]), row),
                pl.BlockSpec((1, dv), lambda b, c: (0, 0))]
    args = [jnp.cos(ang), jnp.sin(ang), z, norm.reshape(1, dv).astype(F32)]
    state_spec = pl.BlockSpec((1, heads, dk, dv), lambda b, c: (b, 0, 0, 0))
    if has_init:
        in_specs.append(pl.BlockSpec((None, 1, heads, dk, dv), lambda b, c: (layer, b, 0, 0, 0)))
        args.append(init)
    o, s_new = pl.pallas_call(
        functools.partial(_ret_kernel, heads=heads, dk=dk, dv=dv, has_init=has_init),
        grid=(batch, nc),
        in_specs=in_specs,
        out_specs=[pl.BlockSpec((L, heads * dv), row), state_spec],
        out_shape=[jax.ShapeDtypeStruct((batch * seq_len, heads * dv), BF16),
                   jax.ShapeDtypeStruct((batch, heads, dk, dv), F32)],
        scratch_shapes=[pltpu.VMEM((heads, L, L), F32)],
        compiler_params=pltpu.CompilerParams(
            dimension_semantics=("arbitrary", "arbitrary"), vmem_limit_bytes=VMEM_LIMIT_BYTES),
        name="retention",
    )(*args)
    return o, s_new


def _gla_kernel(*refs, heads, dk, dv, has_init):
    it = iter(refs)
    z_ref, zlr_ref, w2_ref, bg_ref, norm_ref = next(it), next(it), next(it), next(it), next(it)
    s0_ref = next(it) if has_init else None
    o_ref, s_ref, st_scr = next(it), next(it), next(it)
    ch = pl.program_id(1)
    nch = pl.num_programs(1)
    L = z_ref.shape[0]

    @pl.when(ch == 0)
    def _():
        for h in range(heads):
            st_scr[h] = s0_ref[0, h].T if has_init else jnp.zeros((dv, dk), F32)

    la = _log_sigmoid(_dot(zlr_ref[...].astype(BF16), w2_ref[...]) + bg_ref[...]) / GLA_TAU
    b = _cumsum_rows(la)
    sub = min(GLA_SUB, L)
    tri, _ = _tri_masks(sub)
    koff, voff, roff = heads * dk, 2 * heads * dk, 2 * heads * dk + heads * dv
    hs = range(heads)
    bh = [b[:, h * dk:(h + 1) * dk] for h in hs]
    bl = [x[L - 1:L, :] for x in bh]
    q = [z_ref[:, h * dk:(h + 1) * dk].astype(F32) * dk ** -0.5 for h in hs]
    k = [z_ref[:, koff + h * dk:koff + (h + 1) * dk].astype(F32) for h in hs]
    v = [z_ref[:, voff + h * dv:voff + (h + 1) * dv] for h in hs]
    st_old = [st_scr[h] for h in hs]
    sb = [x.astype(BF16) for x in st_old]
    parts = [[] for _ in hs]
    for c in range(L // sub):
        rows = slice(c * sub, (c + 1) * sub)
        ref = [x[c * sub - 1:c * sub, :] if c else jnp.zeros_like(x[0:1, :]) for x in bh]
        qd = [(q[h][rows] * jnp.exp(bh[h][rows] - ref[h])).astype(BF16) for h in hs]
        kd = [(k[h][rows] * jnp.exp(ref[h] - bh[h][rows])).astype(BF16) for h in hs]
        o = [_dot(jnp.where(tri, _dot_nt(qd[h], kd[h]), 0.0).astype(BF16), v[h][rows]) for h in hs]
        for c2 in range(c):
            rows2 = slice(c2 * sub, (c2 + 1) * sub)
            ke = [(k[h][rows2] * jnp.exp(ref[h] - bh[h][rows2])).astype(BF16) for h in hs]
            o = [o[h] + _dot(_dot_nt(qd[h], ke[h]).astype(BF16), v[h][rows2]) for h in hs]
        o = [o[h] + _dot_nt((q[h][rows] * jnp.exp(bh[h][rows])).astype(BF16), sb[h]) for h in hs]
        for h in hs:
            parts[h].append(o[h])
    for h in hs:
        ke = (k[h] * jnp.exp(bl[h] - bh[h])).astype(BF16)
        st_scr[h] = jnp.exp(bl[h]) * st_old[h] + _dot_tn(v[h], ke)
    for h in hs:
        o = parts[h][0] if len(parts[h]) == 1 else jnp.concatenate(parts[h], axis=0)
        r = z_ref[:, roff + h * dv:roff + (h + 1) * dv].astype(F32)
        y = _silu(r) * _rms(o, norm_ref[...])
        o_ref[:, h * dv:(h + 1) * dv] = y.astype(o_ref.dtype)

    @pl.when(ch == nch - 1)
    def _():
        for h in range(heads):
            s_ref[0, h] = st_scr[h].T


def _gla(z, zlr, w_gate2, b_gate, norm, init, layer, *, batch, seq_len, heads, dk, dv):
    L = min(GLA_CHUNK, seq_len)
    nc = seq_len // L
    has_init = init is not None
    hk = heads * dk
    w2 = jnp.zeros((LANES, hk), BF16).at[:GLA_RANK].set(w_gate2.astype(BF16))
    row = lambda b, c: (b * nc + c, 0)
    in_specs = [pl.BlockSpec((L, z.shape[1]), row),
                pl.BlockSpec((L, LANES), row),
                pl.BlockSpec((LANES, hk), lambda b, c: (0, 0)),
                pl.BlockSpec((1, hk), lambda b, c: (0, 0)),
                pl.BlockSpec((1, dv), lambda b, c: (0, 0))]
    args = [z, zlr, w2, b_gate.reshape(1, hk).astype(F32), norm.reshape(1, dv).astype(F32)]
    state_spec = pl.BlockSpec((1, heads, dk, dv), lambda b, c: (b, 0, 0, 0))
    if has_init:
        in_specs.append(pl.BlockSpec((None, 1, heads, dk, dv), lambda b, c: (layer, b, 0, 0, 0)))
        args.append(init)
    o, s_new = pl.pallas_call(
        functools.partial(_gla_kernel, heads=heads, dk=dk, dv=dv, has_init=has_init),
        grid=(batch, nc),
        in_specs=in_specs,
        out_specs=[pl.BlockSpec((L, heads * dv), row), state_spec],
        out_shape=[jax.ShapeDtypeStruct((batch * seq_len, heads * dv), BF16),
                   jax.ShapeDtypeStruct((batch, heads, dk, dv), F32)],
        scratch_shapes=[pltpu.VMEM((heads, dv, dk), F32)],
        compiler_params=pltpu.CompilerParams(
            dimension_semantics=("arbitrary", "arbitrary"), vmem_limit_bytes=VMEM_LIMIT_BYTES),
        name="gla",
    )(*args)
    return o, s_new


def _gdn_kernel(*refs, qk_heads, v_heads, d, has_init, chunk, lookahead):
    it = iter(refs)
    z_ref, zba_ref = next(it), next(it)
    zn_ref, zban_ref = (next(it), next(it)) if lookahead else (None, None)
    cw_ref, aneg_ref, dtb_ref, norm_ref = (next(it) for _ in range(4))
    if has_init:
        s0_ref, buf0_ref = next(it), next(it)
    o_ref, s_ref = next(it), next(it)
    xs_scr = next(it)
    sets = [tuple(next(it) for _ in range(4)) for _ in range(2 if lookahead else 1)]
    blk_i = pl.program_id(1)
    L = chunk
    rep = v_heads // qk_heads
    conv_ch = (2 * qk_heads + v_heads) * d
    width = cw_ref.shape[0]
    hist = GDN_CONV_HISTORY

    rows = lax.broadcasted_iota(jnp.int32, ((width - 1) * L, L + 2 * hist), 0)
    col = lax.broadcasted_iota(jnp.int32, ((width - 1) * L, L + 2 * hist), 1)
    src = rows % L - (rows // L + 1)
    hi_col = jnp.where(src >= 0, src, src + L + hist)
    lo_col = jnp.where(src >= 0, -1, src + L + 2 * hist)
    shift = jnp.where((col == hi_col) | (col == lo_col), 1.0, 0.0).astype(BF16)

    def prep(zsrc, basrc, r0, scr):
        qkv_scr, gc_scr, gr_scr, bc_scr = scr
        ba = basrc[r0:r0 + L, :]
        lane = lax.broadcasted_iota(jnp.int32, ba.shape, 1)
        beta = _sigmoid(ba)
        la = aneg_ref[...] * _softplus(ba + dtb_ref[...])
        g = _cumsum_rows(jnp.where(lane >= v_heads, la, 0.0))
        gt = _transpose_rows(g)
        for j in range(v_heads):
            gc_scr[j] = jnp.broadcast_to(g[:, v_heads + j:v_heads + j + 1], (L, LANES))
            bc_scr[j] = jnp.broadcast_to(beta[:, j:j + 1], (L, LANES))
        for a in range(qk_heads):
            pair_row = jnp.concatenate(
                [gt[v_heads + rep * a + r:v_heads + rep * a + r + 1, :] for r in range(rep)], axis=-1)
            gr_scr[a] = jnp.broadcast_to(pair_row, (SUBLANES, rep * L))
        cb = GDN_CONV_BLOCK
        for blk in range(conv_ch // cb):
            sl = slice(blk * cb, (blk + 1) * cb)
            cur = zsrc[r0:r0 + L, sl]
            xs_scr[0:L, sl] = cur
            sh = _dot(shift, xs_scr[:, sl])
            w = cw_ref[:, sl]
            acc = cur.astype(F32) * w[width - 1:width]
            for t in range(1, width):
                acc = acc + sh[(t - 1) * L:t * L] * w[width - 1 - t:width - t]
            xs_scr[L:L + hist, sl] = zsrc[r0 + L - hist:r0 + L, sl]
            xs_scr[L + hist:L + 2 * hist, sl] = jnp.zeros((hist, cb), BF16)
            x = _silu(acc)
            for c in range(cb // d):
                head = blk * (cb // d) + c
                xh = x[:, c * d:(c + 1) * d]
                if head < 2 * qk_heads:
                    xh = xh * lax.rsqrt(jnp.sum(xh * xh, axis=-1, keepdims=True) + EPS)
                    if head < qk_heads:
                        xh = xh * d ** -0.5
                qkv_scr[:, head * d:(head + 1) * d] = xh

    assert rep == 2
    lane2 = lax.broadcasted_iota(jnp.int32, (L, 2 * L), 1)
    row2 = lax.broadcasted_iota(jnp.int32, (L, 2 * L), 0)
    left = lane2 < L
    tcol = jnp.where(left, lane2, lane2 - L)
    causal, strict = row2 >= tcol, row2 > tcol
    eye = jnp.where(row2 == tcol, 1.0, 0.0)

    def blockdiag(x):
        return jnp.concatenate([jnp.where(left, x, 0.0), jnp.where(left, 0.0, x)], axis=0).astype(BF16)

    def pair_inverses(mats):
        ps = [eye - a for a in mats]
        pws = [_dot(a.astype(BF16), blockdiag(a)) for a in mats]
        covered = 2
        while covered < L:
            if 2 * covered < L:
                both = [_dot(jnp.concatenate([p, pw], axis=0).astype(BF16), blockdiag(pw)) for p, pw in zip(ps, pws)]
                ps = [p + b[:L] for p, b in zip(ps, both)]
                pws = [b[L:] for b in both]
            else:
                ps = [p + _dot(p.astype(BF16), blockdiag(pw)) for p, pw in zip(ps, pws)]
            covered *= 2
        return ps

    gv = GDN_V_HEADS_PER_STEP
    gk = gv // rep

    def head_group(grp, r0, scr):
        qkv_scr, gc_scr, gr_scr, bc_scr = scr
        out_rows = slice(r0, r0 + L)

        def cols(base):
            return slice(base * d, (base + 1) * d)
        pairs = range(gk)
        heads = range(gv)
        js = [grp * gv + r for r in heads]
        k = lambda a: qkv_scr[:, cols(qk_heads + grp * gk + a)]
        g_col = lambda r: gc_scr[js[r]]
        b_col = lambda r: bc_scr[js[r]]
        qb = [qkv_scr[:, cols(grp * gk + a)].astype(BF16) for a in pairs]
        kb = [k(a).astype(BF16) for a in pairs]
        kb2 = [jnp.concatenate([x, x], axis=0) for x in kb]
        kk = [_dot_nt(kb[a], kb2[a]) for a in pairs]
        qk = [_dot_nt(qb[a], kb2[a]) for a in pairs]
        g_pair = [jnp.where(left, g_col(2 * a)[:, :2 * L], g_col(2 * a + 1)[:, :2 * L]) for a in pairs]
        b_pair = [jnp.where(left, b_col(2 * a)[:, :2 * L], b_col(2 * a + 1)[:, :2 * L]) for a in pairs]
        g_row = [gr_scr[grp * gk + a][0:1, :] for a in pairs]
        dmat = [jnp.exp(jnp.where(causal, g_pair[a] - g_row[a], -jnp.inf)) for a in pairs]
        tinv = pair_inverses([jnp.where(strict, b_pair[a] * dmat[a] * kk[a], 0.0) for a in pairs])
        rhs = [jnp.concatenate([b_col(r) * qkv_scr[:, cols(2 * qk_heads + js[r])],
                                b_col(r) * jnp.exp(g_col(r)) * k(r // rep)], axis=-1).astype(BF16) for r in heads]
        sol = [_dot(blockdiag(tinv[a]), jnp.concatenate([rhs[2 * a], rhs[2 * a + 1]], axis=0)) for a in pairs]
        sol = [sol[r // rep][(r % rep) * L:(r % rep + 1) * L] for r in heads]
        ss = [_dot(jnp.concatenate([sol[r][:, d:].astype(BF16), qb[r // rep]], axis=0),
                   s_ref[0, js[r]].astype(BF16)) for r in heads]
        ub = [(sol[r][:, :d] - ss[r][:L]).astype(BF16) for r in heads]
        ou = [_dot(blockdiag(qk[a] * dmat[a]), jnp.concatenate([ub[2 * a], ub[2 * a + 1]], axis=0)) for a in pairs]
        for r in heads:
            o = jnp.exp(g_col(r)) * ss[r][L:] + ou[r // rep][(r % rep) * L:(r % rep + 1) * L]
            zg = z_ref[out_rows, cols(conv_ch // d + js[r])].astype(F32)
            y = _rms(o, norm_ref[...]) * _silu(zg)
            o_ref[out_rows, cols(js[r])] = y.astype(o_ref.dtype)
        for r in heads:
            g_last = g_col(r)[L - 1:L, :]
            kd = (jnp.exp(g_last - g_col(r)) * k(r // rep)).astype(BF16)
            s_ref[0, js[r]] = jnp.exp(g_last) * s_ref[0, js[r]] + _dot_tn(kd, ub[r])

    assert v_heads % gv == 0 and gv % rep == 0

    def heads(r0, scr):
        for grp in range(v_heads // gv):
            head_group(grp, r0, scr)

    @pl.when(blk_i == 0)
    def _():
        s_ref[...] = s0_ref[...] if has_init else jnp.zeros_like(s_ref)
        prev = jnp.zeros((hist, conv_ch), F32)
        if has_init:
            prev = jnp.concatenate([prev[:hist - width + 1], buf0_ref[0]], axis=0)
        hi = prev.astype(BF16)
        xs_scr[L:L + hist, :] = hi
        xs_scr[L + hist:L + 2 * hist, :] = (prev - hi.astype(F32)).astype(BF16)
        if lookahead:
            prep(z_ref, zba_ref, 0, sets[0])

    if lookahead:
        prep(z_ref, zba_ref, L, sets[1])
        heads(0, sets[0])
        prep(zn_ref, zban_ref, 0, sets[0])
        heads(L, sets[1])
    else:
        prep(z_ref, zba_ref, 0, sets[0])
        heads(0, sets[0])


def _gdn(z, zba, conv_w, a_log, dt_bias, norm, init, layer, *, batch, seq_len, qk_heads, v_heads, d):
    L = min(CHUNK, seq_len)
    nc = seq_len // L
    lookahead = nc >= 2 and nc % 2 == 0
    cpb = 2 if lookahead else 1
    nb = nc // cpb
    has_init = init is not None
    conv_ch = (2 * qk_heads + v_heads) * d
    width = conv_w.shape[0]
    lane_pad = lambda x: jnp.zeros((1, LANES), F32).at[0, v_heads:2 * v_heads].set(x.astype(F32))
    row = lambda b, i: (b * nb + i, 0)
    nxt = lambda b, i: (b * nc + jnp.minimum(cpb * (i + 1), nc - 1), 0)
    const = lambda b, i: (0, 0)
    in_specs = [pl.BlockSpec((cpb * L, z.shape[1]), row), pl.BlockSpec((cpb * L, LANES), row)]
    args = [z, zba]
    if lookahead:
        in_specs += [pl.BlockSpec((L, z.shape[1]), nxt), pl.BlockSpec((L, LANES), nxt)]
        args += [z, zba]
    in_specs += [pl.BlockSpec((width, conv_ch), const),
                 pl.BlockSpec((1, LANES), const),
                 pl.BlockSpec((1, LANES), const),
                 pl.BlockSpec((1, d), const)]
    args += [conv_w.astype(F32), lane_pad(-jnp.exp(a_log)), lane_pad(dt_bias), norm.reshape(1, d).astype(F32)]
    state_spec = pl.BlockSpec((1, v_heads, d, d), lambda b, i: (b, 0, 0, 0))
    if has_init:
        in_specs += [pl.BlockSpec((None, 1, v_heads, d, d), lambda b, i: (layer, b, 0, 0, 0)),
                     pl.BlockSpec((None, 1, width - 1, conv_ch), lambda b, i: (layer, b, 0, 0))]
        args += [init[0], init[1]]
    scratch_set = [pltpu.VMEM((L, conv_ch), F32),
                   pltpu.VMEM((v_heads, L, LANES), F32),
                   pltpu.VMEM((qk_heads, SUBLANES, v_heads // qk_heads * L), F32),
                   pltpu.VMEM((v_heads, L, LANES), F32)]
    o, s_new = pl.pallas_call(
        functools.partial(_gdn_kernel, qk_heads=qk_heads, v_heads=v_heads, d=d, has_init=has_init,
                          chunk=L, lookahead=lookahead),
        grid=(batch, nb),
        in_specs=in_specs,
        out_specs=[pl.BlockSpec((cpb * L, v_heads * d), row), state_spec],
        out_shape=[jax.ShapeDtypeStruct((batch * seq_len, v_heads * d), BF16),
                   jax.ShapeDtypeStruct((batch, v_heads, d, d), F32)],
        scratch_shapes=[pltpu.VMEM((L + 2 * GDN_CONV_HISTORY, conv_ch), BF16)] + scratch_set * cpb,
        compiler_params=pltpu.CompilerParams(
            dimension_semantics=("arbitrary", "arbitrary"), vmem_limit_bytes=VMEM_LIMIT_BYTES),
        name="gdn",
    )(*args)
    return o, s_new


def _pad_cols(w, n):
    return jnp.zeros((w.shape[0], n), F32).at[:, :w.shape[1]].set(w.astype(F32))


def _trunk(x, states, pos0, p, *, batch, seq_len):
    m, dm = x.shape
    has_init = states is not None
    if has_init:
        mC, mn, mm, rS, gS, dS, dconv, fconv = states
    depth = p['norm_mix'].shape[0]
    outs = [[] for _ in range(8)]
    for i in range(depth):
        kind, j = i % 4, i // 4
        g_mix = p['norm_mix'][i]
        if kind == 0:
            heads, dk, dv = p['mlstm_n_shape']
            w_in = p['mlstm_w_in'][j]
            nm = 2 * heads * dk + 2 * heads * dv
            z, zg = _matmul(x, w_in[:, :nm].astype(BF16), norm_g=g_mix, gate_w=_pad_cols(w_in[:, nm:], LANES),
                            name="mlstm_in")
            bias = _pad_cols(p['mlstm_b_gates'][j][None], LANES)
            init = (mC, mn, mm) if has_init else None
            a, (c_, n_, m_) = _mlstm(z, zg, bias, p['mlstm_norm'][j], init, j,
                                     batch=batch, seq_len=seq_len, heads=heads, dk=dk, dv=dv)
            outs[0].append(c_); outs[1].append(n_); outs[2].append(m_)
            w_out = p['mlstm_w_out'][j]
        elif kind == 1:
            heads, dk, dv = p['ret_shape']
            z = _matmul(x, p['ret_w_in'][j].astype(BF16), norm_g=g_mix, name="ret_in")
            a, s_ = _retention(z, p['ret_norm'][j], rS if has_init else None, j, pos0,
                               batch=batch, seq_len=seq_len, heads=heads, dk=dk, dv=dv)
            outs[3].append(s_)
            w_out = p['ret_w_out'][j]
        elif kind == 2:
            heads, dk, dv = p['gla_shape']
            w_in = p['gla_w_in'][j]
            nm = 2 * heads * dk + 2 * heads * dv
            z, zlr = _matmul(x, w_in[:, :nm].astype(BF16), norm_g=g_mix, gate_w=_pad_cols(w_in[:, nm:], LANES),
                             name="gla_in")
            a, s_ = _gla(z, zlr, p['gla_w_gate2'][j], p['gla_b_gate'][j], p['gla_norm'][j],
                         gS if has_init else None, j,
                         batch=batch, seq_len=seq_len, heads=heads, dk=dk, dv=dv)
            outs[4].append(s_)
            w_out = p['gla_w_out'][j]
        else:
            qk_heads, v_heads, d = p['gdn_shape']
            w_in = p['gdn_w_in'][j]
            conv_ch = (2 * qk_heads + v_heads) * d
            nm = conv_ch + v_heads * d
            width = p['gdn_conv_w'].shape[1]
            z, zba = _matmul(x, w_in[:, :nm].astype(BF16), norm_g=g_mix, gate_w=_pad_cols(w_in[:, nm:], LANES),
                             name="gdn_in")
            init = (dS, dconv) if has_init else None
            a, s_ = _gdn(z, zba, p['gdn_conv_w'][j], p['gdn_A_log'][j], p['gdn_dt_bias'][j], p['gdn_norm'][j],
                         init, j, batch=batch, seq_len=seq_len, qk_heads=qk_heads, v_heads=v_heads, d=d)
            outs[5].append(s_)
            outs[6].append(z.reshape(batch, seq_len, nm)[:, seq_len - (width - 1):, :conv_ch].astype(F32))
            w_out = p['gdn_w_out'][j]
        x = _matmul(a, w_out.astype(BF16), residual=x, out_dtype=F32, name="mix_out")

        f = p['ffn_w_gate'].shape[2]
        fw = p['ffn_conv_w'].shape[1]
        buf8 = jnp.zeros((batch, SUBLANES, f), F32)
        if has_init:
            buf8 = buf8.at[:, SUBLANES - (fw - 1):].set(fconv[i].astype(F32))
        act, tail = _ffn_in(x, p['norm_ffn'][i], p['ffn_w_gate_bf16'], p['ffn_w_up_bf16'], i,
                            buf8, p['ffn_conv_w'][i], p['ffn_conv_b'][i], seq_len=seq_len)
        outs[7].append(tail[:, SUBLANES - (fw - 1):])
        x = _matmul(act, p['ffn_w_down_bf16'], layer=i, residual=x, out_dtype=F32, name="ffn_out")
    y = _final_norm(x, p['norm_final'])
    return y.reshape(batch, seq_len, dm), tuple(o[0][None] if len(o) == 1 else jnp.stack(o) for o in outs)


def kernel(x_prompt, x_sample, state_mlstm_C, state_mlstm_n, state_mlstm_m, state_ret_S, state_gla_S, state_gdn_S, state_gdn_conv, state_ffn_conv, norm_mix, norm_ffn, norm_final, mlstm_w_in, mlstm_b_gates, mlstm_norm, mlstm_w_out, ret_w_in, ret_norm, ret_w_out, gla_w_in, gla_w_gate2, gla_b_gate, gla_norm, gla_w_out, gdn_w_in, gdn_conv_w, gdn_A_log, gdn_dt_bias, gdn_norm, gdn_w_out, ffn_w_gate, ffn_w_up, ffn_conv_w, ffn_conv_b, ffn_w_down):
    p = dict(norm_mix=norm_mix, norm_ffn=norm_ffn, norm_final=norm_final,
             mlstm_w_in=mlstm_w_in, mlstm_b_gates=mlstm_b_gates, mlstm_norm=mlstm_norm, mlstm_w_out=mlstm_w_out,
             ret_w_in=ret_w_in, ret_norm=ret_norm, ret_w_out=ret_w_out,
             gla_w_in=gla_w_in, gla_w_gate2=gla_w_gate2, gla_b_gate=gla_b_gate, gla_norm=gla_norm, gla_w_out=gla_w_out,
             gdn_w_in=gdn_w_in, gdn_conv_w=gdn_conv_w, gdn_A_log=gdn_A_log, gdn_dt_bias=gdn_dt_bias,
             gdn_norm=gdn_norm, gdn_w_out=gdn_w_out,
             ffn_w_gate=ffn_w_gate, ffn_w_up=ffn_w_up, ffn_conv_w=ffn_conv_w, ffn_conv_b=ffn_conv_b,
             ffn_w_down=ffn_w_down)
    for name in ('ffn_w_gate', 'ffn_w_up', 'ffn_w_down'):
        p[name + '_bf16'] = p[name].astype(BF16)
    p['mlstm_n_shape'] = state_mlstm_C.shape[2:]
    p['ret_shape'] = state_ret_S.shape[2:]
    p['gla_shape'] = state_gla_S.shape[2:]
    d = state_gdn_S.shape[-1]
    v_heads = state_gdn_S.shape[2]
    p['gdn_shape'] = ((state_gdn_conv.shape[-1] // d - v_heads) // 2, v_heads, d)

    bp, tp, dm = x_prompt.shape
    bs, ts, _ = x_sample.shape
    y_prompt, p_states = _trunk(x_prompt.reshape(bp * tp, dm), None, 0, p, batch=bp, seq_len=tp)
    cache = (state_mlstm_C, state_mlstm_n, state_mlstm_m, state_ret_S, state_gla_S, state_gdn_S,
             state_gdn_conv, state_ffn_conv)
    y_sample, s_states = _trunk(x_sample.reshape(bs * ts, dm), cache, PAST_LEN, p, batch=bs, seq_len=ts)
    return (y_prompt, y_sample) + p_states + s_states
```

```python
import functools
import math

import jax
import jax.numpy as jnp
from jax import lax
from jax.experimental import pallas as pl
from jax.experimental.pallas import tpu as pltpu

F32 = jnp.float32
BF16 = jnp.bfloat16
EPS = 1e-6
PAST_LEN = 2048
ROPE_BASE = 10000.0
GLA_TAU = 16.0
GLA_RANK = 16
CHUNK = 64
MLSTM_CHUNK = 256
RET_CHUNK = 256
GLA_CHUNK = 128
GLA_SUB = 64
GDN_V_HEADS_PER_STEP = 32
GDN_CONV_HISTORY = 16
GDN_CONV_BLOCK = 256
FFN_ROW_CHUNK = 256
LANES = 128
SUBLANES = 8
VMEM_LIMIT_BYTES = 56 * 1024 * 1024


def _sigmoid(x):
    return 1.0 / (1.0 + jnp.exp(-x))


def _silu(x):
    return x * _sigmoid(x)


def _log1p_exp_neg_abs(x):
    return jnp.log(1.0 + jnp.exp(-jnp.abs(x)))


def _log_sigmoid(x):
    return jnp.minimum(x, 0.0) - _log1p_exp_neg_abs(x)


def _softplus(x):
    return jnp.maximum(x, 0.0) + _log1p_exp_neg_abs(x)


def _rms(x, g):
    return x * lax.rsqrt(jnp.mean(x * x, axis=-1, keepdims=True) + EPS) * g


def _dot(a, b):
    return jnp.dot(a, b, preferred_element_type=F32)


def _dot_nt(a, b):
    return lax.dot_general(a, b, (((1,), (1,)), ((), ())), preferred_element_type=F32)


def _dot_tn(a, b):
    return lax.dot_general(a, b, (((0,), (0,)), ((), ())), preferred_element_type=F32)


def _split3(x):
    hi = x.astype(BF16)
    r = x - hi.astype(F32)
    mid = r.astype(BF16)
    lo = (r - mid.astype(F32)).astype(BF16)
    return hi, mid, lo


def _cumsum_rows(x):
    n = x.shape[0]
    r = lax.broadcasted_iota(jnp.int32, (n, n), 0)
    c = lax.broadcasted_iota(jnp.int32, (n, n), 1)
    tril = jnp.where(r >= c, 1.0, 0.0).astype(BF16)
    hi, mid, lo = _split3(x)
    return _dot(tril, hi) + _dot(tril, mid) + _dot(tril, lo)


def _transpose_rows(x):
    n = x.shape[0]
    if n < LANES:
        x = jnp.concatenate([x, jnp.zeros((LANES - n, x.shape[1]), x.dtype)], axis=0)
    return x.T[:, :n]


def _causal_taps(x3, prev, cw):
    width = cw.shape[0]
    t8 = lax.broadcasted_iota(jnp.int32, (x3.shape[0], SUBLANES, x3.shape[2]), 1)
    acc = x3 * cw[width - 1:width]
    for s in range(1, width):
        sh = pltpu.roll(x3, s, 1)
        head = sh[:, :SUBLANES]
        for e in range(s):
            head = jnp.where(t8 == e, prev[s - 1 - e], head)
        sh = jnp.concatenate([head, sh[:, SUBLANES:]], axis=1)
        acc = acc + sh * cw[width - 1 - s:width - s]
    return acc


def _tri_masks(n):
    r = lax.broadcasted_iota(jnp.int32, (n, n), 0)
    c = lax.broadcasted_iota(jnp.int32, (n, n), 1)
    return r >= c, r > c


def _mm_kernel(*refs, has_norm, has_res, has_gate):
    it = iter(refs)
    x_ref = next(it)
    g_ref = next(it) if has_norm else None
    w_ref = next(it)
    wgate_ref = next(it) if has_gate else None
    r_ref = next(it) if has_res else None
    o_ref = next(it)
    ogate_ref = next(it) if has_gate else None
    h_scr = next(it) if has_norm else None
    if has_norm:
        @pl.when(pl.program_id(1) == 0)
        def _():
            h = _rms(x_ref[...].astype(F32), g_ref[...])
            h_scr[...] = h.astype(BF16)
            if has_gate:
                both = _dot(h.astype(BF16), wgate_ref[...])
                ogate_ref[...] = both[:, :LANES] + both[:, LANES:]
        h = h_scr[...]
    else:
        h = x_ref[...]
    acc = _dot(h, w_ref[...])
    if has_res:
        acc = acc + r_ref[...]
    o_ref[...] = acc.astype(o_ref.dtype)


def _pick(n, cands):
    for c in cands:
        if n % c == 0:
            return c
    raise ValueError(f"no tile for {n}")


def _matmul(x, w, *, layer=None, norm_g=None, gate_w=None, residual=None, out_dtype=BF16, bm=1024, bn=None,
            name="mm"):
    m, k = x.shape
    n = w.shape[-1]
    bm = min(bm, m)
    if bn is None:
        if norm_g is not None and residual is None and k <= 2048:
            bn = _pick(n, (2048, 1024, 512, 256, 128))
        else:
            bn = _pick(n, (1024, 512, 256, 128)) if k <= 4096 else _pick(n, (512, 256, 128))
    assert m % bm == 0 and n % bn == 0
    has_norm, has_res, has_gate = norm_g is not None, residual is not None, gate_w is not None
    assert has_norm or not has_gate
    in_specs = [pl.BlockSpec((bm, k), lambda i, j: (i, 0))]
    args = [x]
    if has_norm:
        in_specs.append(pl.BlockSpec((1, k), lambda i, j: (0, 0)))
        args.append(norm_g.reshape(1, k).astype(F32))
    if layer is None:
        in_specs.append(pl.BlockSpec((k, bn), lambda i, j: (0, j)))
    else:
        in_specs.append(pl.BlockSpec((None, k, bn), lambda i, j: (layer, 0, j)))
    args.append(w)
    if has_gate:
        g_hi = gate_w.astype(BF16)
        g_lo = (gate_w - g_hi.astype(F32)).astype(BF16)
        in_specs.append(pl.BlockSpec((k, 2 * LANES), lambda i, j: (0, 0)))
        args.append(jnp.concatenate([g_hi, g_lo], axis=1))
    if has_res:
        in_specs.append(pl.BlockSpec((bm, bn), lambda i, j: (i, j)))
        args.append(residual)
    out_specs = [pl.BlockSpec((bm, bn), lambda i, j: (i, j))]
    out_shape = [jax.ShapeDtypeStruct((m, n), out_dtype)]
    if has_gate:
        out_specs.append(pl.BlockSpec((bm, LANES), lambda i, j: (i, 0)))
        out_shape.append(jax.ShapeDtypeStruct((m, LANES), F32))
    outs = pl.pallas_call(
        functools.partial(_mm_kernel, has_norm=has_norm, has_res=has_res, has_gate=has_gate),
        grid=(m // bm, n // bn),
        in_specs=in_specs,
        out_specs=out_specs,
        out_shape=out_shape,
        scratch_shapes=[pltpu.VMEM((bm, k), BF16)] if has_norm else [],
        compiler_params=pltpu.CompilerParams(
            dimension_semantics=("arbitrary", "arbitrary"), vmem_limit_bytes=VMEM_LIMIT_BYTES),
        name=name,
    )(*args)
    return outs if has_gate else outs[0]


def _final_norm_kernel(x_ref, g_ref, o_ref):
    o_ref[...] = _rms(x_ref[...], g_ref[...])


def _final_norm(x, g, *, bm=512):
    m, k = x.shape
    bm = min(bm, m)
    return pl.pallas_call(
        _final_norm_kernel,
        grid=(m // bm,),
        in_specs=[pl.BlockSpec((bm, k), lambda i: (i, 0)), pl.BlockSpec((1, k), lambda i: (0, 0))],
        out_specs=pl.BlockSpec((bm, k), lambda i: (i, 0)),
        out_shape=jax.ShapeDtypeStruct((m, k), F32),
        compiler_params=pltpu.CompilerParams(
            dimension_semantics=("arbitrary",), vmem_limit_bytes=VMEM_LIMIT_BYTES),
        name="final_norm",
    )(x, g.reshape(1, k).astype(F32))


def _ffn_in_kernel(x_ref, g_ref, wg_ref, wu_ref, buf_ref, cw_ref, cb_ref, act_ref, tail_ref, h_scr, halo_scr,
                   *, tiles_per_seq, seq_rows):
    i, j = pl.program_id(0), pl.program_id(1)
    bm, bn = act_ref.shape

    @pl.when(j == 0)
    def _():
        h_scr[...] = _rms(x_ref[...], g_ref[...]).astype(BF16)

    nseq = bm // seq_rows
    first = (i % tiles_per_seq) == 0
    halo = jnp.where(first, buf_ref[...], halo_scr[j])
    prev2, prev1 = halo[:, 6:7], halo[:, 7:8]
    cw, cb = cw_ref[...], cb_ref[...]
    wg, wu = wg_ref[...], wu_ref[...]
    rc = min(FFN_ROW_CHUNK, bm)
    spc = max(rc // seq_rows, 1)
    tt = rc // spc
    tails = []
    for r in range(bm // rc):
        rows = slice(r * rc, (r + 1) * rc)
        h = h_scr[rows, :]
        g3 = _dot(h, wg).reshape(spc, tt, bn)
        u3 = _dot(h, wu).reshape(spc, tt, bn)
        pv = [prev1, prev2] if nseq == 1 else [prev1[r * spc:(r + 1) * spc], prev2[r * spc:(r + 1) * spc]]
        gc = _causal_taps(g3, pv, cw) + cb
        act_ref[rows, :] = (_silu(gc) * u3).reshape(rc, bn).astype(act_ref.dtype)
        if nseq == 1:
            prev2, prev1 = g3[:, tt - 2:tt - 1], g3[:, tt - 1:tt]
            tails = [g3[:, tt - SUBLANES:]]
        else:
            tails.append(g3[:, tt - SUBLANES:])
    tail = tails[0] if len(tails) == 1 else jnp.concatenate(tails, axis=0)
    halo_scr[j] = tail
    tail_ref[...] = tail


def _ffn_in(x, norm_g, w_gate, w_up, layer, buf8, conv_w, conv_b, *, seq_len, bm=1024, bn=512):
    m, k = x.shape
    f = w_gate.shape[2]
    batch = m // seq_len
    bm = min(bm, m)
    assert f % bn == 0 and m % bm == 0 and (seq_len % bm == 0 or bm % seq_len == 0)
    nf = f // bn
    tiles_per_seq = max(seq_len // bm, 1)
    seq_rows = min(seq_len, bm)
    nseq = bm // seq_rows
    seq_blk = lambda i, j: (i // tiles_per_seq, 0, j)
    act, tails = pl.pallas_call(
        functools.partial(_ffn_in_kernel, tiles_per_seq=tiles_per_seq, seq_rows=seq_rows),
        grid=(m // bm, nf),
        in_specs=[
            pl.BlockSpec((bm, k), lambda i, j: (i, 0)),
            pl.BlockSpec((1, k), lambda i, j: (0, 0)),
            pl.BlockSpec((None, k, bn), lambda i, j: (layer, 0, j)),
            pl.BlockSpec((None, k, bn), lambda i, j: (layer, 0, j)),
            pl.BlockSpec((nseq, SUBLANES, bn), seq_blk),
            pl.BlockSpec((3, bn), lambda i, j: (0, j)),
            pl.BlockSpec((1, bn), lambda i, j: (0, j)),
        ],
        out_specs=[pl.BlockSpec((bm, bn), lambda i, j: (i, j)),
                   pl.BlockSpec((nseq, SUBLANES, bn), lambda i, j: (i, 0, j))],
        out_shape=[jax.ShapeDtypeStruct((m, f), BF16),
                   jax.ShapeDtypeStruct((m // bm * nseq, SUBLANES, f), F32)],
        scratch_shapes=[pltpu.VMEM((bm, k), BF16), pltpu.VMEM((nf, nseq, SUBLANES, bn), F32)],
        compiler_params=pltpu.CompilerParams(
            dimension_semantics=("arbitrary", "arbitrary"), vmem_limit_bytes=VMEM_LIMIT_BYTES),
        name="ffn_in",
    )(x, norm_g.reshape(1, k).astype(F32), w_gate, w_up, buf8, conv_w.astype(F32),
      conv_b.reshape(1, f).astype(F32))
    return act, tails.reshape(batch, tiles_per_seq, SUBLANES, f)[:, tiles_per_seq - 1]


def _mlstm_kernel(*refs, heads, dk, dv, has_init):
    it = iter(refs)
    z_ref, zg_ref, bias_ref, norm_ref = next(it), next(it), next(it), next(it)
    if has_init:
        c0_ref, n0_ref, m0_ref = next(it), next(it), next(it)
    o_ref, c_ref, n_ref, m_ref = next(it), next(it), next(it), next(it)
    ch = pl.program_id(1)
    L = z_ref.shape[0]

    @pl.when(ch == 0)
    def _():
        if has_init:
            c_ref[...] = c0_ref[...]
            n_ref[...] = n0_ref[...]
            m_ref[...] = m0_ref[...]
        else:
            c_ref[...] = jnp.zeros_like(c_ref)
            n_ref[...] = jnp.zeros_like(n_ref)
            m_ref[...] = jnp.zeros_like(m_ref)

    g = zg_ref[...] + bias_ref[...]
    lane = lax.broadcasted_iota(jnp.int32, g.shape, 1)
    u = jnp.where(lane < heads, g, _cumsum_rows(_log_sigmoid(g)))
    ut = _transpose_rows(u)
    tri, _ = _tri_masks(L)
    m_all = m_ref[0]
    m_lane = lax.broadcasted_iota(jnp.int32, m_all.shape, 1)
    qoff, koff, voff, ooff = 0, heads * dk, 2 * heads * dk, 2 * heads * dk + heads * dv
    hs = range(heads)
    ig_col = [u[:, h:h + 1] for h in hs]
    ig_row = [ut[h:h + 1, :] for h in hs]
    b_col = [u[:, heads + h:heads + h + 1] for h in hs]
    b_row = [ut[heads + h:heads + h + 1, :] for h in hs]
    m_prev = [m_all[:, h:h + 1] for h in hs]
    q = [(z_ref[:, qoff + h * dk:qoff + (h + 1) * dk].astype(F32) * dk ** -0.5).astype(BF16) for h in hs]
    k = [z_ref[:, koff + h * dk:koff + (h + 1) * dk] for h in hs]
    v = [z_ref[:, voff + h * dv:voff + (h + 1) * dv] for h in hs]
    c_old = [c_ref[0, h] for h in hs]
    n_old = [n_ref[0, h:h + 1, :] for h in hs]
    d = [jnp.where(tri, b_col[h] - b_row[h] + ig_row[h], -jnp.inf) for h in hs]
    li = [b_col[h] + m_prev[h] for h in hs]
    m_t = [jnp.maximum(li[h], jnp.max(d[h], axis=1, keepdims=True)) for h in hs]
    w_inter = [jnp.exp(li[h] - m_t[h]) for h in hs]
    a = [_dot_nt(q[h], k[h]) * jnp.exp(d[h] - m_t[h]) for h in hs]
    num = [_dot(a[h].astype(BF16), v[h]) + w_inter[h] * _dot(q[h], c_old[h].astype(BF16)) for h in hs]
    den = [jnp.sum(a[h], axis=1, keepdims=True)
           + w_inter[h] * jnp.sum(q[h].astype(F32) * n_old[h], axis=1, keepdims=True) for h in hs]
    hh = [num[h] / jnp.maximum(jnp.abs(den[h]), jnp.exp(-m_t[h])) for h in hs]
    for h in hs:
        m_new = m_t[h][L - 1:L, :]
        b_last = b_col[h][L - 1:L, :]
        w_k = jnp.exp(b_last - b_col[h] + ig_col[h] - m_new)
        dec = jnp.exp(b_last + m_prev[h] - m_new)
        kw = k[h].astype(F32) * w_k
        c_ref[0, h] = dec * c_old[h] + _dot_tn(kw.astype(BF16), v[h])
        n_ref[0, h:h + 1, :] = dec * n_old[h] + jnp.sum(kw, axis=0, keepdims=True)
        m_all = jnp.where(m_lane == h, m_new, m_all)
    for h in hs:
        og = z_ref[:, ooff + h * dv:ooff + (h + 1) * dv].astype(F32)
        y = _sigmoid(og) * _rms(hh[h], norm_ref[...])
        o_ref[:, h * dv:(h + 1) * dv] = y.astype(o_ref.dtype)
    m_ref[0] = m_all


def _mlstm(z, zg, bias, norm, init, layer, *, batch, seq_len, heads, dk, dv):
    L = min(MLSTM_CHUNK, seq_len)
    nc = seq_len // L
    has_init = init is not None
    row = lambda b, c: (b * nc + c, 0)
    in_specs = [pl.BlockSpec((L, z.shape[1]), row),
                pl.BlockSpec((L, LANES), row),
                pl.BlockSpec((1, LANES), lambda b, c: (0, 0)),
                pl.BlockSpec((1, dv), lambda b, c: (0, 0))]
    args = [z, zg, bias, norm.reshape(1, dv).astype(F32)]
    state_specs = [pl.BlockSpec((1, heads, dk, dv), lambda b, c: (b, 0, 0, 0)),
                   pl.BlockSpec((1, heads, dk), lambda b, c: (b, 0, 0)),
                   pl.BlockSpec((1, 1, heads), lambda b, c: (b, 0, 0))]
    if has_init:
        in_specs += [pl.BlockSpec((None, 1, heads, dk, dv), lambda b, c: (layer, b, 0, 0, 0)),
                     pl.BlockSpec((None, 1, heads, dk), lambda b, c: (layer, b, 0, 0)),
                     state_specs[2]]
        args += [init[0], init[1], init[2][layer].reshape(batch, 1, heads)]
    o, c_new, n_new, m_new = pl.pallas_call(
        functools.partial(_mlstm_kernel, heads=heads, dk=dk, dv=dv, has_init=has_init),
        grid=(batch, nc),
        in_specs=in_specs,
        out_specs=[pl.BlockSpec((L, heads * dv), row)] + state_specs,
        out_shape=[jax.ShapeDtypeStruct((batch * seq_len, heads * dv), BF16),
                   jax.ShapeDtypeStruct((batch, heads, dk, dv), F32),
                   jax.ShapeDtypeStruct((batch, heads, dk), F32),
                   jax.ShapeDtypeStruct((batch, 1, heads), F32)],
        compiler_params=pltpu.CompilerParams(
            dimension_semantics=("arbitrary", "arbitrary"), vmem_limit_bytes=VMEM_LIMIT_BYTES),
        name="mlstm",
    )(*args)
    return o, (c_new, n_new, m_new.reshape(batch, heads))


def _ret_kernel(*refs, heads, dk, dv, has_init):
    it = iter(refs)
    cos_ref, sin_ref, z_ref, norm_ref = next(it), next(it), next(it), next(it)
    s0_ref = next(it) if has_init else None
    o_ref, s_ref, dec_scr = next(it), next(it), next(it)
    ch = pl.program_id(1)
    L = z_ref.shape[0]
    half = dk // 2

    @pl.when(ch == 0)
    def _():
        s_ref[...] = s0_ref[...] if has_init else jnp.zeros_like(s_ref)
        rel = (lax.broadcasted_iota(jnp.int32, (L, L), 0) - lax.broadcasted_iota(jnp.int32, (L, L), 1)).astype(F32)
        for h in range(heads):
            lg = math.log1p(-2.0 ** (-5.0 - h))
            dec_scr[h] = jnp.where(rel >= 0, jnp.exp(jnp.maximum(rel, 0.0) * lg), 0.0)

    cos, sin = cos_ref[...], sin_ref[...]
    t_col = lax.broadcasted_iota(jnp.int32, (L, 1), 0).astype(F32)
    koff, voff, goff = heads * dk, 2 * heads * dk, 2 * heads * dk + heads * dv

    def rope(x):
        x1, x2 = x[:, :half], x[:, half:]
        return jnp.concatenate([x1 * cos - x2 * sin, x1 * sin + x2 * cos], axis=-1)

    for h in range(heads):
        lg = math.log1p(-2.0 ** (-5.0 - h))
        q = rope(z_ref[:, h * dk:(h + 1) * dk].astype(F32)) * dk ** -0.5
        k = rope(z_ref[:, koff + h * dk:koff + (h + 1) * dk].astype(F32))
        v = z_ref[:, voff + h * dv:voff + (h + 1) * dv]
        gate = z_ref[:, goff + h * dv:goff + (h + 1) * dv].astype(F32)
        qb = q.astype(BF16)
        a = _dot_nt(qb, k.astype(BF16)) * dec_scr[h]
        s_old = s_ref[0, h]
        o = _dot(a.astype(BF16), v) + jnp.exp((t_col + 1.0) * lg) * _dot(qb, s_old.astype(BF16))
        kd = k * jnp.exp((L - 1.0 - t_col) * lg)
        s_ref[0, h] = math.exp(L * lg) * s_old + _dot_tn(kd.astype(BF16), v)
        y = _silu(gate) * _rms(o, norm_ref[...])
        o_ref[:, h * dv:(h + 1) * dv] = y.astype(o_ref.dtype)


def _retention(z, norm, init, layer, pos0, *, batch, seq_len, heads, dk, dv):
    L = min(RET_CHUNK, seq_len)
    nc = seq_len // L
    has_init = init is not None
    half = dk // 2
    inv = ROPE_BASE ** (-jnp.arange(half, dtype=F32) / half)
    ang = (pos0 + jnp.arange(seq_len)).astype(F32)[:, None] * inv[None]
    row = lambda b, c: (b * nc + c, 0)
    in_specs = [pl.BlockSpec((L, half), lambda b, c: (c, 0)),
                pl.BlockSpec((L, half), lambda b, c: (c, 0)),
                pl.BlockSpec((L, z.shape[1]), row),
                pl.BlockSpec((1, dv), lambda b, c: (0, 0))]
    args = [jnp.cos(ang), jnp.sin(ang), z, norm.reshape(1, dv).astype(F32)]
    state_spec = pl.BlockSpec((1, heads, dk, dv), lambda b, c: (b, 0, 0, 0))
    if has_init:
        in_specs.append(pl.BlockSpec((None, 1, heads, dk, dv), lambda b, c: (layer, b, 0, 0, 0)))
        args.append(init)
    o, s_new = pl.pallas_call(
        functools.partial(_ret_kernel, heads=heads, dk=dk, dv=dv, has_init=has_init),
        grid=(batch, nc),
        in_specs=in_specs,
        out_specs=[pl.BlockSpec((L, heads * dv), row), state_spec],
        out_shape=[jax.ShapeDtypeStruct((batch * seq_len, heads * dv), BF16),
                   jax.ShapeDtypeStruct((batch, heads, dk, dv), F32)],
        scratch_shapes=[pltpu.VMEM((heads, L, L), F32)],
        compiler_params=pltpu.CompilerParams(
            dimension_semantics=("arbitrary", "arbitrary"), vmem_limit_bytes=VMEM_LIMIT_BYTES),
        name="retention",
    )(*args)
    return o, s_new


def _gla_kernel(*refs, heads, dk, dv, has_init):
    it = iter(refs)
    z_ref, zlr_ref, w2_ref, bg_ref, norm_ref = next(it), next(it), next(it), next(it), next(it)
    s0_ref = next(it) if has_init else None
    o_ref, s_ref, st_scr = next(it), next(it), next(it)
    ch = pl.program_id(1)
    nch = pl.num_programs(1)
    L = z_ref.shape[0]

    @pl.when(ch == 0)
    def _():
        for h in range(heads):
            st_scr[h] = s0_ref[0, h].T if has_init else jnp.zeros((dv, dk), F32)

    la = _log_sigmoid(_dot(zlr_ref[...].astype(BF16), w2_ref[...]) + bg_ref[...]) / GLA_TAU
    b = _cumsum_rows(la)
    sub = min(GLA_SUB, L)
    tri, _ = _tri_masks(sub)
    koff, voff, roff = heads * dk, 2 * heads * dk, 2 * heads * dk + heads * dv
    hs = range(heads)
    bh = [b[:, h * dk:(h + 1) * dk] for h in hs]
    bl = [x[L - 1:L, :] for x in bh]
    q = [z_ref[:, h * dk:(h + 1) * dk].astype(F32) * dk ** -0.5 for h in hs]
    k = [z_ref[:, koff + h * dk:koff + (h + 1) * dk].astype(F32) for h in hs]
    v = [z_ref[:, voff + h * dv:voff + (h + 1) * dv] for h in hs]
    st_old = [st_scr[h] for h in hs]
    sb = [x.astype(BF16) for x in st_old]
    parts = [[] for _ in hs]
    for c in range(L // sub):
        rows = slice(c * sub, (c + 1) * sub)
        ref = [x[c * sub - 1:c * sub, :] if c else jnp.zeros_like(x[0:1, :]) for x in bh]
        qd = [(q[h][rows] * jnp.exp(bh[h][rows] - ref[h])).astype(BF16) for h in hs]
        kd = [(k[h][rows] * jnp.exp(ref[h] - bh[h][rows])).astype(BF16) for h in hs]
        o = [_dot(jnp.where(tri, _dot_nt(qd[h], kd[h]), 0.0).astype(BF16), v[h][rows]) for h in hs]
        for c2 in range(c):
            rows2 = slice(c2 * sub, (c2 + 1) * sub)
            ke = [(k[h][rows2] * jnp.exp(ref[h] - bh[h][rows2])).astype(BF16) for h in hs]
            o = [o[h] + _dot(_dot_nt(qd[h], ke[h]).astype(BF16), v[h][rows2]) for h in hs]
        o = [o[h] + _dot_nt((q[h][rows] * jnp.exp(bh[h][rows])).astype(BF16), sb[h]) for h in hs]
        for h in hs:
            parts[h].append(o[h])
    for h in hs:
        ke = (k[h] * jnp.exp(bl[h] - bh[h])).astype(BF16)
        st_scr[h] = jnp.exp(bl[h]) * st_old[h] + _dot_tn(v[h], ke)
    for h in hs:
        o = parts[h][0] if len(parts[h]) == 1 else jnp.concatenate(parts[h], axis=0)
        r = z_ref[:, roff + h * dv:roff + (h + 1) * dv].astype(F32)
        y = _silu(r) * _rms(o, norm_ref[...])
        o_ref[:, h * dv:(h + 1) * dv] = y.astype(o_ref.dtype)

    @pl.when(ch == nch - 1)
    def _():
        for h in range(heads):
            s_ref[0, h] = st_scr[h].T


def _gla(z, zlr, w_gate2, b_gate, norm, init, layer, *, batch, seq_len, heads, dk, dv):
    L = min(GLA_CHUNK, seq_len)
    nc = seq_len // L
    has_init = init is not None
    hk = heads * dk
    w2 = jnp.zeros((LANES, hk), BF16).at[:GLA_RANK].set(w_gate2.astype(BF16))
    row = lambda b, c: (b * nc + c, 0)
    in_specs = [pl.BlockSpec((L, z.shape[1]), row),
                pl.BlockSpec((L, LANES), row),
                pl.BlockSpec((LANES, hk), lambda b, c: (0, 0)),
                pl.BlockSpec((1, hk), lambda b, c: (0, 0)),
                pl.BlockSpec((1, dv), lambda b, c: (0, 0))]
    args = [z, zlr, w2, b_gate.reshape(1, hk).astype(F32), norm.reshape(1, dv).astype(F32)]
    state_spec = pl.BlockSpec((1, heads, dk, dv), lambda b, c: (b, 0, 0, 0))
    if has_init:
        in_specs.append(pl.BlockSpec((None, 1, heads, dk, dv), lambda b, c: (layer, b, 0, 0, 0)))
        args.append(init)
    o, s_new = pl.pallas_call(
        functools.partial(_gla_kernel, heads=heads, dk=dk, dv=dv, has_init=has_init),
        grid=(batch, nc),
        in_specs=in_specs,
        out_specs=[pl.BlockSpec((L, heads * dv), row), state_spec],
        out_shape=[jax.ShapeDtypeStruct((batch * seq_len, heads * dv), BF16),
                   jax.ShapeDtypeStruct((batch, heads, dk, dv), F32)],
        scratch_shapes=[pltpu.VMEM((heads, dv, dk), F32)],
        compiler_params=pltpu.CompilerParams(
            dimension_semantics=("arbitrary", "arbitrary"), vmem_limit_bytes=VMEM_LIMIT_BYTES),
        name="gla",
    )(*args)
    return o, s_new


def _gdn_kernel(*refs, qk_heads, v_heads, d, has_init, chunk, lookahead):
    it = iter(refs)
    z_ref, zba_ref = next(it), next(it)
    zn_ref, zban_ref = (next(it), next(it)) if lookahead else (None, None)
    cw_ref, aneg_ref, dtb_ref, norm_ref = (next(it) for _ in range(4))
    if has_init:
        s0_ref, buf0_ref = next(it), next(it)
    o_ref, s_ref = next(it), next(it)
    xs_scr = next(it)
    sets = [tuple(next(it) for _ in range(4)) for _ in range(2 if lookahead else 1)]
    blk_i = pl.program_id(1)
    L = chunk
    rep = v_heads // qk_heads
    conv_ch = (2 * qk_heads + v_heads) * d
    width = cw_ref.shape[0]
    hist = GDN_CONV_HISTORY

    rows = lax.broadcasted_iota(jnp.int32, ((width - 1) * L, L + 2 * hist), 0)
    col = lax.broadcasted_iota(jnp.int32, ((width - 1) * L, L + 2 * hist), 1)
    src = rows % L - (rows // L + 1)
    hi_col = jnp.where(src >= 0, src, src + L + hist)
    lo_col = jnp.where(src >= 0, -1, src + L + 2 * hist)
    shift = jnp.where((col == hi_col) | (col == lo_col), 1.0, 0.0).astype(BF16)

    def prep(zsrc, basrc, r0, scr):
        qkv_scr, gc_scr, gr_scr, bc_scr = scr
        ba = basrc[r0:r0 + L, :]
        lane = lax.broadcasted_iota(jnp.int32, ba.shape, 1)
        beta = _sigmoid(ba)
        la = aneg_ref[...] * _softplus(ba + dtb_ref[...])
        g = _cumsum_rows(jnp.where(lane >= v_heads, la, 0.0))
        gt = _transpose_rows(g)
        for j in range(v_heads):
            gc_scr[j] = jnp.broadcast_to(g[:, v_heads + j:v_heads + j + 1], (L, LANES))
            bc_scr[j] = jnp.broadcast_to(beta[:, j:j + 1], (L, LANES))
        for a in range(qk_heads):
            pair_row = jnp.concatenate(
                [gt[v_heads + rep * a + r:v_heads + rep * a + r + 1, :] for r in range(rep)], axis=-1)
            gr_scr[a] = jnp.broadcast_to(pair_row, (SUBLANES, rep * L))
        cb = GDN_CONV_BLOCK
        for blk in range(conv_ch // cb):
            sl = slice(blk * cb, (blk + 1) * cb)
            cur = zsrc[r0:r0 + L, sl]
            xs_scr[0:L, sl] = cur
            sh = _dot(shift, xs_scr[:, sl])
            w = cw_ref[:, sl]
            acc = cur.astype(F32) * w[width - 1:width]
            for t in range(1, width):
                acc = acc + sh[(t - 1) * L:t * L] * w[width - 1 - t:width - t]
            xs_scr[L:L + hist, sl] = zsrc[r0 + L - hist:r0 + L, sl]
            xs_scr[L + hist:L + 2 * hist, sl] = jnp.zeros((hist, cb), BF16)
            x = _silu(acc)
            for c in range(cb // d):
                head = blk * (cb // d) + c
                xh = x[:, c * d:(c + 1) * d]
                if head < 2 * qk_heads:
                    xh = xh * lax.rsqrt(jnp.sum(xh * xh, axis=-1, keepdims=True) + EPS)
                    if head < qk_heads:
                        xh = xh * d ** -0.5
                qkv_scr[:, head * d:(head + 1) * d] = xh

    assert rep == 2
    lane2 = lax.broadcasted_iota(jnp.int32, (L, 2 * L), 1)
    row2 = lax.broadcasted_iota(jnp.int32, (L, 2 * L), 0)
    left = lane2 < L
    tcol = jnp.where(left, lane2, lane2 - L)
    causal, strict = row2 >= tcol, row2 > tcol
    eye = jnp.where(row2 == tcol, 1.0, 0.0)

    def blockdiag(x):
        return jnp.concatenate([jnp.where(left, x, 0.0), jnp.where(left, 0.0, x)], axis=0).astype(BF16)

    def pair_inverses(mats):
        ps = [eye - a for a in mats]
        pws = [_dot(a.astype(BF16), blockdiag(a)) for a in mats]
        covered = 2
        while covered < L:
            if 2 * covered < L:
                both = [_dot(jnp.concatenate([p, pw], axis=0).astype(BF16), blockdiag(pw)) for p, pw in zip(ps, pws)]
                ps = [p + b[:L] for p, b in zip(ps, both)]
                pws = [b[L:] for b in both]
            else:
                ps = [p + _dot(p.astype(BF16), blockdiag(pw)) for p, pw in zip(ps, pws)]
            covered *= 2
        return ps

    gv = GDN_V_HEADS_PER_STEP
    gk = gv // rep

    def head_group(grp, r0, scr):
        qkv_scr, gc_scr, gr_scr, bc_scr = scr
        out_rows = slice(r0, r0 + L)

        def cols(base):
            return slice(base * d, (base + 1) * d)
        pairs = range(gk)
        heads = range(gv)
        js = [grp * gv + r for r in heads]
        k = lambda a: qkv_scr[:, cols(qk_heads + grp * gk + a)]
        g_col = lambda r: gc_scr[js[r]]
        b_col = lambda r: bc_scr[js[r]]
        qb = [qkv_scr[:, cols(grp * gk + a)].astype(BF16) for a in pairs]
        kb = [k(a).astype(BF16) for a in pairs]
        kb2 = [jnp.concatenate([x, x], axis=0) for x in kb]
        kk = [_dot_nt(kb[a], kb2[a]) for a in pairs]
        qk = [_dot_nt(qb[a], kb2[a]) for a in pairs]
        g_pair = [jnp.where(left, g_col(2 * a)[:, :2 * L], g_col(2 * a + 1)[:, :2 * L]) for a in pairs]
        b_pair = [jnp.where(left, b_col(2 * a)[:, :2 * L], b_col(2 * a + 1)[:, :2 * L]) for a in pairs]
        g_row = [gr_scr[grp * gk + a][0:1, :] for a in pairs]
        dmat = [jnp.exp(jnp.where(causal, g_pair[a] - g_row[a], -jnp.inf)) for a in pairs]
        tinv = pair_inverses([jnp.where(strict, b_pair[a] * dmat[a] * kk[a], 0.0) for a in pairs])
        rhs = [jnp.concatenate([b_col(r) * qkv_scr[:, cols(2 * qk_heads + js[r])],
                                b_col(r) * jnp.exp(g_col(r)) * k(r // rep)], axis=-1).astype(BF16) for r in heads]
        sol = [_dot(blockdiag(tinv[a]), jnp.concatenate([rhs[2 * a], rhs[2 * a + 1]], axis=0)) for a in pairs]
        sol = [sol[r // rep][(r % rep) * L:(r % rep + 1) * L] for r in heads]
        ss = [_dot(jnp.concatenate([sol[r][:, d:].astype(BF16), qb[r // rep]], axis=0),
                   s_ref[0, js[r]].astype(BF16)) for r in heads]
        ub = [(sol[r][:, :d] - ss[r][:L]).astype(BF16) for r in heads]
        ou = [_dot(blockdiag(qk[a] * dmat[a]), jnp.concatenate([ub[2 * a], ub[2 * a + 1]], axis=0)) for a in pairs]
        for r in heads:
            o = jnp.exp(g_col(r)) * ss[r][L:] + ou[r // rep][(r % rep) * L:(r % rep + 1) * L]
            zg = z_ref[out_rows, cols(conv_ch // d + js[r])].astype(F32)
            y = _rms(o, norm_ref[...]) * _silu(zg)
            o_ref[out_rows, cols(js[r])] = y.astype(o_ref.dtype)
        for r in heads:
            g_last = g_col(r)[L - 1:L, :]
            kd = (jnp.exp(g_last - g_col(r)) * k(r // rep)).astype(BF16)
            s_ref[0, js[r]] = jnp.exp(g_last) * s_ref[0, js[r]] + _dot_tn(kd, ub[r])

    assert v_heads % gv == 0 and gv % rep == 0

    def heads(r0, scr):
        for grp in range(v_heads // gv):
            head_group(grp, r0, scr)

    @pl.when(blk_i == 0)
    def _():
        s_ref[...] = s0_ref[...] if has_init else jnp.zeros_like(s_ref)
        prev = jnp.zeros((hist, conv_ch), F32)
        if has_init:
            prev = jnp.concatenate([prev[:hist - width + 1], buf0_ref[0]], axis=0)
        hi = prev.astype(BF16)
        xs_scr[L:L + hist, :] = hi
        xs_scr[L + hist:L + 2 * hist, :] = (prev - hi.astype(F32)).astype(BF16)
        if lookahead:
            prep(z_ref, zba_ref, 0, sets[0])

    if lookahead:
        prep(z_ref, zba_ref, L, sets[1])
        heads(0, sets[0])
        prep(zn_ref, zban_ref, 0, sets[0])
        heads(L, sets[1])
    else:
        prep(z_ref, zba_ref, 0, sets[0])
        heads(0, sets[0])


def _gdn(z, zba, conv_w, a_log, dt_bias, norm, init, layer, *, batch, seq_len, qk_heads, v_heads, d):
    L = min(CHUNK, seq_len)
    nc = seq_len // L
    lookahead = nc >= 2 and nc % 2 == 0
    cpb = 2 if lookahead else 1
    nb = nc // cpb
    has_init = init is not None
    conv_ch = (2 * qk_heads + v_heads) * d
    width = conv_w.shape[0]
    lane_pad = lambda x: jnp.zeros((1, LANES), F32).at[0, v_heads:2 * v_heads].set(x.astype(F32))
    row = lambda b, i: (b * nb + i, 0)
    nxt = lambda b, i: (b * nc + jnp.minimum(cpb * (i + 1), nc - 1), 0)
    const = lambda b, i: (0, 0)
    in_specs = [pl.BlockSpec((cpb * L, z.shape[1]), row), pl.BlockSpec((cpb * L, LANES), row)]
    args = [z, zba]
    if lookahead:
        in_specs += [pl.BlockSpec((L, z.shape[1]), nxt), pl.BlockSpec((L, LANES), nxt)]
        args += [z, zba]
    in_specs += [pl.BlockSpec((width, conv_ch), const),
                 pl.BlockSpec((1, LANES), const),
                 pl.BlockSpec((1, LANES), const),
                 pl.BlockSpec((1, d), const)]
    args += [conv_w.astype(F32), lane_pad(-jnp.exp(a_log)), lane_pad(dt_bias), norm.reshape(1, d).astype(F32)]
    state_spec = pl.BlockSpec((1, v_heads, d, d), lambda b, i: (b, 0, 0, 0))
    if has_init:
        in_specs += [pl.BlockSpec((None, 1, v_heads, d, d), lambda b, i: (layer, b, 0, 0, 0)),
                     pl.BlockSpec((None, 1, width - 1, conv_ch), lambda b, i: (layer, b, 0, 0))]
        args += [init[0], init[1]]
    scratch_set = [pltpu.VMEM((L, conv_ch), F32),
                   pltpu.VMEM((v_heads, L, LANES), F32),
                   pltpu.VMEM((qk_heads, SUBLANES, v_heads // qk_heads * L), F32),
                   pltpu.VMEM((v_heads, L, LANES), F32)]
    o, s_new = pl.pallas_call(
        functools.partial(_gdn_kernel, qk_heads=qk_heads, v_heads=v_heads, d=d, has_init=has_init,
                          chunk=L, lookahead=lookahead),
        grid=(batch, nb),
        in_specs=in_specs,
        out_specs=[pl.BlockSpec((cpb * L, v_heads * d), row), state_spec],
        out_shape=[jax.ShapeDtypeStruct((batch * seq_len, v_heads * d), BF16),
                   jax.ShapeDtypeStruct((batch, v_heads, d, d), F32)],
        scratch_shapes=[pltpu.VMEM((L + 2 * GDN_CONV_HISTORY, conv_ch), BF16)] + scratch_set * cpb,
        compiler_params=pltpu.CompilerParams(
            dimension_semantics=("arbitrary", "arbitrary"), vmem_limit_bytes=VMEM_LIMIT_BYTES),
        name="gdn",
    )(*args)
    return o, s_new


def _pad_cols(w, n):
    return jnp.zeros((w.shape[0], n), F32).at[:, :w.shape[1]].set(w.astype(F32))


def _trunk(x, states, pos0, p, *, batch, seq_len):
    m, dm = x.shape
    has_init = states is not None
    if has_init:
        mC, mn, mm, rS, gS, dS, dconv, fconv = states
    depth = p['norm_mix'].shape[0]
    outs = [[] for _ in range(8)]
    for i in range(depth):
        kind, j = i % 4, i // 4
        g_mix = p['norm_mix'][i]
        if kind == 0:
            heads, dk, dv = p['mlstm_n_shape']
            w_in = p['mlstm_w_in'][j]
            nm = 2 * heads * dk + 2 * heads * dv
            z, zg = _matmul(x, w_in[:, :nm].astype(BF16), norm_g=g_mix, gate_w=_pad_cols(w_in[:, nm:], LANES),
                            name="mlstm_in")
            bias = _pad_cols(p['mlstm_b_gates'][j][None], LANES)
            init = (mC, mn, mm) if has_init else None
            a, (c_, n_, m_) = _mlstm(z, zg, bias, p['mlstm_norm'][j], init, j,
                                     batch=batch, seq_len=seq_len, heads=heads, dk=dk, dv=dv)
            outs[0].append(c_); outs[1].append(n_); outs[2].append(m_)
            w_out = p['mlstm_w_out'][j]
        elif kind == 1:
            heads, dk, dv = p['ret_shape']
            z = _matmul(x, p['ret_w_in'][j].astype(BF16), norm_g=g_mix, name="ret_in")
            a, s_ = _retention(z, p['ret_norm'][j], rS if has_init else None, j, pos0,
                               batch=batch, seq_len=seq_len, heads=heads, dk=dk, dv=dv)
            outs[3].append(s_)
            w_out = p['ret_w_out'][j]
        elif kind == 2:
            heads, dk, dv = p['gla_shape']
            w_in = p['gla_w_in'][j]
            nm = 2 * heads * dk + 2 * heads * dv
            z, zlr = _matmul(x, w_in[:, :nm].astype(BF16), norm_g=g_mix, gate_w=_pad_cols(w_in[:, nm:], LANES),
                             name="gla_in")
            a, s_ = _gla(z, zlr, p['gla_w_gate2'][j], p['gla_b_gate'][j], p['gla_norm'][j],
                         gS if has_init else None, j,
                         batch=batch, seq_len=seq_len, heads=heads, dk=dk, dv=dv)
            outs[4].append(s_)
            w_out = p['gla_w_out'][j]
        else:
            qk_heads, v_heads, d = p['gdn_shape']
            w_in = p['gdn_w_in'][j]
            conv_ch = (2 * qk_heads + v_heads) * d
            nm = conv_ch + v_heads * d
            width = p['gdn_conv_w'].shape[1]
            z, zba = _matmul(x, w_in[:, :nm].astype(BF16), norm_g=g_mix, gate_w=_pad_cols(w_in[:, nm:], LANES),
                             name="gdn_in")
            init = (dS, dconv) if has_init else None
            a, s_ = _gdn(z, zba, p['gdn_conv_w'][j], p['gdn_A_log'][j], p['gdn_dt_bias'][j], p['gdn_norm'][j],
                         init, j, batch=batch, seq_len=seq_len, qk_heads=qk_heads, v_heads=v_heads, d=d)
            outs[5].append(s_)
            outs[6].append(z.reshape(batch, seq_len, nm)[:, seq_len - (width - 1):, :conv_ch].astype(F32))
            w_out = p['gdn_w_out'][j]
        x = _matmul(a, w_out.astype(BF16), residual=x, out_dtype=F32, name="mix_out")

        f = p['ffn_w_gate'].shape[2]
        fw = p['ffn_conv_w'].shape[1]
        buf8 = jnp.zeros((batch, SUBLANES, f), F32)
        if has_init:
            buf8 = buf8.at[:, SUBLANES - (fw - 1):].set(fconv[i].astype(F32))
        act, tail = _ffn_in(x, p['norm_ffn'][i], p['ffn_w_gate_bf16'], p['ffn_w_up_bf16'], i,
                            buf8, p['ffn_conv_w'][i], p['ffn_conv_b'][i], seq_len=seq_len)
        outs[7].append(tail[:, SUBLANES - (fw - 1):])
        x = _matmul(act, p['ffn_w_down_bf16'], layer=i, residual=x, out_dtype=F32, name="ffn_out")
    y = _final_norm(x, p['norm_final'])
    return y.reshape(batch, seq_len, dm), tuple(o[0][None] if len(o) == 1 else jnp.stack(o) for o in outs)


def kernel(x_prompt, x_sample, state_mlstm_C, state_mlstm_n, state_mlstm_m, state_ret_S, state_gla_S, state_gdn_S, state_gdn_conv, state_ffn_conv, norm_mix, norm_ffn, norm_final, mlstm_w_in, mlstm_b_gates, mlstm_norm, mlstm_w_out, ret_w_in, ret_norm, ret_w_out, gla_w_in, gla_w_gate2, gla_b_gate, gla_norm, gla_w_out, gdn_w_in, gdn_conv_w, gdn_A_log, gdn_dt_bias, gdn_norm, gdn_w_out, ffn_w_gate, ffn_w_up, ffn_conv_w, ffn_conv_b, ffn_w_down):
    p = dict(norm_mix=norm_mix, norm_ffn=norm_ffn, norm_final=norm_final,
             mlstm_w_in=mlstm_w_in, mlstm_b_gates=mlstm_b_gates, mlstm_norm=mlstm_norm, mlstm_w_out=mlstm_w_out,
             ret_w_in=ret_w_in, ret_norm=ret_norm, ret_w_out=ret_w_out,
             gla_w_in=gla_w_in, gla_w_gate2=gla_w_gate2, gla_b_gate=gla_b_gate, gla_norm=gla_norm, gla_w_out=gla_w_out,
             gdn_w_in=gdn_w_in, gdn_conv_w=gdn_conv_w, gdn_A_log=gdn_A_log, gdn_dt_bias=gdn_dt_bias,
             gdn_norm=gdn_norm, gdn_w_out=gdn_w_out,
             ffn_w_gate=ffn_w_gate, ffn_w_up=ffn_w_up, ffn_conv_w=ffn_conv_w, ffn_conv_b=ffn_conv_b,
             ffn_w_down=ffn_w_down)
    for name in ('ffn_w_gate', 'ffn_w_up', 'ffn_w_down'):
        p[name + '_bf16'] = p[name].astype(BF16)
    p['mlstm_n_shape'] = state_mlstm_C.shape[2:]
    p['ret_shape'] = state_ret_S.shape[2:]
    p['gla_shape'] = state_gla_S.shape[2:]
    d = state_gdn_S.shape[-1]
    v_heads = state_gdn_S.shape[2]
    p['gdn_shape'] = ((state_gdn_conv.shape[-1] // d - v_heads) // 2, v_heads, d)

    bp, tp, dm = x_prompt.shape
    bs, ts, _ = x_sample.shape
    y_prompt, p_states = _trunk(x_prompt.reshape(bp * tp, dm), None, 0, p, batch=bp, seq_len=tp)
    cache = (state_mlstm_C, state_mlstm_n, state_mlstm_m, state_ret_S, state_gla_S, state_gdn_S,
             state_gdn_conv, state_ffn_conv)
    y_sample, s_states = _trunk(x_sample.reshape(bs * ts, dm), cache, PAST_LEN, p, batch=bs, seq_len=ts)
    return (y_prompt, y_sample) + p_states + s_states
```
